```python
import jax, jax.numpy as jnp
from jax import lax
import numpy as np

D_MODEL = 2048
BATCH = 1
SEQ = 8192
DEPTH = 1

HEAD_DIM = 128
N_SPARSE_HEADS = 8
SPARSE_WIDTH = N_SPARSE_HEADS * HEAD_DIM
N_IDX_HEADS = 16
IDX_HEAD_DIM = 64
TOPK_MAX = 256
DIL_GROUPS = ((128, 1), (512, 4), (2048, 16))
HEADS_PER_DIL_GROUP = 4
N_DIL_HEADS = HEADS_PER_DIL_GROUP * len(DIL_GROUPS)
DIL_WIDTH = N_DIL_HEADS * HEAD_DIM
DIL_OUT_WIDTH = HEADS_PER_DIL_GROUP * HEAD_DIM
N_ALIBI_HEADS = N_DIL_HEADS + N_SPARSE_HEADS
BLOCK = 128
D_FF = ((-(-8 * D_MODEL // 3)) + 255) // 256 * 256
DEEPNORM_ALPHA = (2 * DEPTH) ** 0.25
DEEPNORM_BETA = (8 * DEPTH) ** -0.25
LN_EPS = 1e-5
IN_SIZES = (SPARSE_WIDTH, SPARSE_WIDTH, SPARSE_WIDTH,
            N_IDX_HEADS * IDX_HEAD_DIM, IDX_HEAD_DIM, N_IDX_HEADS,
            DIL_WIDTH, DIL_WIDTH, DIL_WIDTH,
            D_MODEL, D_MODEL)
IN_WIDTH = sum(IN_SIZES)
V_COLUMN_SLOTS = (2, 8)

kernel_name = "hybrid_dsa_dilated_gated_deepnorm"


def layer_norm(x, g, b):
    xf = x.astype(jnp.float32)
    mu = jnp.mean(xf, -1, keepdims=True)
    var = jnp.mean(jnp.square(xf - mu), -1, keepdims=True)
    y = (xf - mu) * lax.rsqrt(var + LN_EPS) * g.astype(jnp.float32) + b.astype(jnp.float32)
    return y.astype(x.dtype)


def alibi_slopes():
    return jnp.exp2(-8.0 * jnp.arange(1, N_ALIBI_HEADS + 1, dtype=jnp.float32) / N_ALIBI_HEADS)


def dsa_attention(q, k, v, q_idx, k_idx, w_idx, slopes):
    B, S, H, Dh = q.shape
    topk = min(TOPK_MAX, S // 4)
    nb = S // BLOCK
    key_pos = jnp.arange(S)
    k_idx32 = k_idx.astype(jnp.float32)
    gather = jax.vmap(lambda a, i: a[i])

    def to_blocks(a):
        return jnp.moveaxis(a.reshape((B, nb, BLOCK) + a.shape[2:]), 1, 0)

    def block_fn(args):
        n, qb, qib, wb = args
        qpos = n * BLOCK + jnp.arange(BLOCK)
        rel = jnp.einsum('bqhd,bsd->bqhs', qib.astype(jnp.float32), k_idx32) * IDX_HEAD_DIM ** -0.5
        score = jnp.einsum('bqh,bqhs->bqs', wb.astype(jnp.float32), jax.nn.relu(rel))
        causal = key_pos[None, :] <= qpos[:, None]
        score = jnp.where(causal[None], score, -jnp.inf)
        _, sel = lax.top_k(score, topk)
        ks = gather(k, sel)
        vs = gather(v, sel)
        logits = jnp.einsum('bqhd,bqkhd->bhqk', qb, ks,
                            preferred_element_type=jnp.float32) * Dh ** -0.5
        dist = (qpos[None, :, None] - sel).astype(jnp.float32)
        logits = logits - slopes[None, :, None, None] * dist[:, None]
        valid = sel <= qpos[None, :, None]
        logits = jnp.where(valid[:, None], logits, -jnp.inf)
        p = jax.nn.softmax(logits, axis=-1).astype(v.dtype)
        return jnp.einsum('bhqk,bqkhd->bqhd', p, vs)

    out = lax.map(block_fn, (jnp.arange(nb), to_blocks(q), to_blocks(q_idx), to_blocks(w_idx)))
    return jnp.moveaxis(out, 0, 1).reshape(B, S, H * Dh)


def dilated_group_attention(q, k, v, window, dilation, slopes):
    B, S, h, Dh = q.shape
    span = window // dilation
    assert span <= BLOCK
    n_sub = -(-S // dilation)
    n_pad = -(-n_sub // BLOCK) * BLOCK
    s_pad = n_pad * dilation
    nb = n_pad // BLOCK

    def to_sub(a):
        a = jnp.pad(a, ((0, 0), (0, s_pad - S), (0, 0), (0, 0)))
        a = a.reshape(B, n_pad, dilation, h, Dh).transpose(0, 2, 1, 3, 4)
        return a.reshape(B * dilation, nb, BLOCK, h, Dh)

    def prev(a):
        return jnp.pad(a, ((0, 0), (1, 0), (0, 0), (0, 0), (0, 0)))[:, :-1]

    qs, ks, vs = to_sub(q), to_sub(k), to_sub(v)
    kk = jnp.concatenate([prev(ks), ks], axis=2)
    vv = jnp.concatenate([prev(vs), vs], axis=2)
    logits = jnp.einsum('znqhd,znkhd->zhnqk', qs, kk,
                        preferred_element_type=jnp.float32) * Dh ** -0.5
    step = BLOCK + jnp.arange(BLOCK)[:, None] - jnp.arange(2 * BLOCK)[None, :]
    key_sub = jnp.arange(nb)[:, None] * BLOCK - BLOCK + jnp.arange(2 * BLOCK)[None, :]
    mask = ((step >= 0) & (step <= span))[None] & (key_sub >= 0)[:, None, :]
    logits = logits - slopes[None, :, None, None, None] * (step * dilation).astype(jnp.float32)
    logits = jnp.where(mask[None, None], logits, -jnp.inf)
    lse = jax.nn.logsumexp(logits, axis=-1)
    p = jnp.exp(logits - lse[..., None]).astype(v.dtype)
    o = jnp.einsum('zhnqk,znkhd->znqhd', p, vv)

    def from_sub(a):
        tail = a.shape[4:]
        a = a.reshape((B, dilation, n_pad, h) + tail)
        a = jnp.swapaxes(a, 1, 2).reshape((B, s_pad, h) + tail)
        return a[:, :S]

    return from_sub(o), from_sub(jnp.moveaxis(lse, 1, -1))


def hybrid_mixer(x, w_in, w_a, w_b, w_out, slopes):
    B, S, _ = x.shape
    split_points = [int(c) for c in np.cumsum(IN_SIZES)[:-1]]
    proj = x @ w_in
    aq, ak, av, iq, ik, iw, bq, bk, bv, ga, gb = jnp.split(proj, split_points, axis=-1)

    def heads(a, n):
        return a.reshape(B, S, n, -1)

    o_a = dsa_attention(heads(aq, N_SPARSE_HEADS), heads(ak, N_SPARSE_HEADS), heads(av, N_SPARSE_HEADS),
                        heads(iq, N_IDX_HEADS), ik, iw * N_IDX_HEADS ** -0.5,
                        slopes[N_DIL_HEADS:])

    bq4, bk4, bv4 = heads(bq, N_DIL_HEADS), heads(bk, N_DIL_HEADS), heads(bv, N_DIL_HEADS)
    outs, lses = [], []
    for g, (win, dil) in enumerate(DIL_GROUPS):
        sl = slice(g * HEADS_PER_DIL_GROUP, (g + 1) * HEADS_PER_DIL_GROUP)
        o_g, l_g = dilated_group_attention(bq4[:, :, sl], bk4[:, :, sl], bv4[:, :, sl], win, dil, slopes[sl])
        outs.append(o_g)
        lses.append(l_g)
    wts = jax.nn.softmax(jnp.stack(lses), axis=0).astype(x.dtype)
    o_b = jnp.einsum('gbsh,gbshd->bshd', wts, jnp.stack(outs)).reshape(B, S, DIL_OUT_WIDTH)

    merged = jax.nn.sigmoid(ga) * (o_a @ w_a) + jax.nn.sigmoid(gb) * (o_b @ w_b)
    return merged @ w_out


def swiglu_ffn(h, w_gate, w_up, w_down):
    return (jax.nn.silu(h @ w_gate) * (h @ w_up)) @ w_down


def setup_inputs(seed: int = 0) -> dict:
    key = jax.random.key(seed)
    ks = jax.random.split(key, 12)
    f32 = jnp.float32
    x = jax.random.normal(ks[0], (BATCH, SEQ, D_MODEL), f32)
    col_scale = jnp.concatenate([
        jnp.full((n,), DEEPNORM_BETA if i in V_COLUMN_SLOTS else 1.0, f32)
        for i, n in enumerate(IN_SIZES)])
    w_in = jax.random.normal(ks[1], (DEPTH, D_MODEL, IN_WIDTH), f32) * (D_MODEL ** -0.5) * col_scale
    w_a = jax.random.normal(ks[2], (DEPTH, SPARSE_WIDTH, D_MODEL), f32) * (SPARSE_WIDTH ** -0.5 * DEEPNORM_BETA)
    w_b = jax.random.normal(ks[3], (DEPTH, DIL_OUT_WIDTH, D_MODEL), f32) * (DIL_OUT_WIDTH ** -0.5 * DEEPNORM_BETA)
    w_out = jax.random.normal(ks[4], (DEPTH, D_MODEL, D_MODEL), f32) * (D_MODEL ** -0.5 * DEEPNORM_BETA)
    ln1_g = 1.0 + 0.01 * jax.random.normal(ks[5], (DEPTH, D_MODEL), f32)
    ln1_b = 0.01 * jax.random.normal(ks[6], (DEPTH, D_MODEL), f32)
    w_gate = jax.random.normal(ks[7], (DEPTH, D_MODEL, D_FF), f32) * (D_MODEL ** -0.5)
    w_up = jax.random.normal(ks[8], (DEPTH, D_MODEL, D_FF), f32) * (D_MODEL ** -0.5 * DEEPNORM_BETA)
    w_down = jax.random.normal(ks[9], (DEPTH, D_FF, D_MODEL), f32) * (D_FF ** -0.5 * DEEPNORM_BETA)
    ln2_g = 1.0 + 0.01 * jax.random.normal(ks[10], (DEPTH, D_MODEL), f32)
    ln2_b = 0.01 * jax.random.normal(ks[11], (DEPTH, D_MODEL), f32)
    return {"x": x, "w_in": w_in, "w_a": w_a, "w_b": w_b, "w_out": w_out,
            "ln1_g": ln1_g, "ln1_b": ln1_b, "w_gate": w_gate, "w_up": w_up,
            "w_down": w_down, "ln2_g": ln2_g, "ln2_b": ln2_b}


def reference(x, w_in, w_a, w_b, w_out, ln1_g, ln1_b, w_gate, w_up, w_down, ln2_g, ln2_b):
    slopes = alibi_slopes()
    for l in range(DEPTH):
        h = layer_norm(DEEPNORM_ALPHA * x + hybrid_mixer(x, w_in[l], w_a[l], w_b[l], w_out[l], slopes),
                       ln1_g[l], ln1_b[l])
        x = layer_norm(DEEPNORM_ALPHA * h + swiglu_ffn(h, w_gate[l], w_up[l], w_down[l]),
                       ln2_g[l], ln2_b[l])
    return x
```

```python
import functools

import numpy as np
import jax
import jax.numpy as jnp
from jax import lax
from jax.experimental import pallas as pl
from jax.experimental.pallas import tpu as pltpu

F32 = jnp.float32
BF16 = jnp.bfloat16

D_MODEL = 2048
HEAD_DIM = 128
N_SPARSE_HEADS = 8
SPARSE_WIDTH = N_SPARSE_HEADS * HEAD_DIM
N_IDX_HEADS = 16
IDX_HEAD_DIM = 64
TOPK_MAX = 256
DIL_GROUPS = ((128, 1), (512, 4), (2048, 16))
HEADS_PER_DIL_GROUP = 4
N_DIL_HEADS = HEADS_PER_DIL_GROUP * len(DIL_GROUPS)
DIL_WIDTH = N_DIL_HEADS * HEAD_DIM
DIL_OUT_WIDTH = HEADS_PER_DIL_GROUP * HEAD_DIM
N_ALIBI_HEADS = N_DIL_HEADS + N_SPARSE_HEADS
BLOCK = 128
D_FF = ((-(-8 * D_MODEL // 3)) + 255) // 256 * 256
DEPTH = 1
DEEPNORM_ALPHA = (2 * DEPTH) ** 0.25
LN_EPS = 1e-5
IN_SIZES = (SPARSE_WIDTH, SPARSE_WIDTH, SPARSE_WIDTH,
            N_IDX_HEADS * IDX_HEAD_DIM, IDX_HEAD_DIM, N_IDX_HEADS,
            DIL_WIDTH, DIL_WIDTH, DIL_WIDTH, D_MODEL, D_MODEL)
IN_OFFS = tuple(int(c) for c in np.cumsum((0,) + IN_SIZES))

VMEM_LIMIT_BYTES = 56 * 1024 * 1024
NEG_BIG = -1e30
F32_LOWEST = float(np.finfo(np.float32).min)
F32_HIGHEST = float(np.finfo(np.float32).max)


def _alibi_slopes():
    return [2.0 ** (-8.0 * (i + 1) / N_ALIBI_HEADS) for i in range(N_ALIBI_HEADS)]


def _cparams(*sem):
    return pltpu.CompilerParams(dimension_semantics=sem,
                                vmem_limit_bytes=VMEM_LIMIT_BYTES)


def _proj_kernel(x_ref, w_ref, o_ref, *, scale, sigmoid, transpose_out):
    acc = jnp.dot(x_ref[...], w_ref[...].astype(BF16), preferred_element_type=F32)
    if scale != 1.0:
        acc = acc * scale
    if sigmoid:
        acc = jax.nn.sigmoid(acc)
    if transpose_out:
        acc = acc.T
    o_ref[...] = acc.astype(o_ref.dtype)


def _project(xb, w, col0, ncols, *, tm, tn, out_dtype, scale=1.0, sigmoid=False,
             transpose_out=False, name):
    S, K = xb.shape
    assert col0 % tn == 0 and ncols % tn == 0 and S % tm == 0
    jb = col0 // tn
    grid = (S // tm, ncols // tn)
    if transpose_out:
        out_shape = jax.ShapeDtypeStruct((ncols, S), out_dtype)
        out_spec = pl.BlockSpec((tn, tm), lambda i, j: (j, i))
    else:
        out_shape = jax.ShapeDtypeStruct((S, ncols), out_dtype)
        out_spec = pl.BlockSpec((tm, tn), lambda i, j: (i, j))
    return pl.pallas_call(
        functools.partial(_proj_kernel, scale=scale, sigmoid=sigmoid,
                          transpose_out=transpose_out),
        grid=grid,
        in_specs=[pl.BlockSpec((tm, K), lambda i, j: (i, 0)),
                  pl.BlockSpec((K, tn), lambda i, j: (0, j + jb))],
        out_specs=out_spec,
        out_shape=out_shape,
        compiler_params=_cparams("parallel", "arbitrary"),
        name=name,
    )(xb, w)


DSA_QB = 256
DSA_KC = 256
SEARCH_MAX_ITERS = 512


def _dsa_kernel(qi_tab, kj_tab,
                qT_ref, k_ref, vT_ref, qiT_ref, ki_ref, wT_ref, slope_ref,
                o_ref,
                sc_ref, thr_ref, m_ref, l_ref, acc_ref, *, topk):
    QB, KC = DSA_QB, DSA_KC
    t = pl.program_id(0)
    i = qi_tab[t]
    j = kj_tab[t]

    def chunk_scores(c, diagonal):
        row0 = pl.multiple_of(c * KC, KC)
        kic = ki_ref[pl.ds(row0, KC), :]
        acc = jnp.zeros((KC, QB), F32)
        for h in range(N_IDX_HEADS):
            r = jnp.dot(kic, qiT_ref[h * IDX_HEAD_DIM:(h + 1) * IDX_HEAD_DIM, :],
                        preferred_element_type=F32)
            acc = acc + wT_ref[h:h + 1, :] * jnp.maximum(r, 0.0)
        if diagonal:
            kpos = lax.broadcasted_iota(jnp.int32, (KC, QB), 0)
            qpos = lax.broadcasted_iota(jnp.int32, (KC, QB), 1)
            causal = kpos <= qpos
            mx = jnp.max(jnp.where(causal, acc, -jnp.inf), axis=0, keepdims=True)
            mn = jnp.min(jnp.where(causal, acc, jnp.inf), axis=0, keepdims=True)
            acc = jnp.where(causal, acc, -jnp.inf)
        else:
            mx = jnp.max(acc, axis=0, keepdims=True)
            mn = jnp.min(acc, axis=0, keepdims=True)
        sc_ref[pl.ds(row0, KC), :] = acc
        return mx, mn

    def count_ge(cand):
        def body(c, part):
            row0 = pl.multiple_of(c * KC, KC)
            tile = sc_ref[pl.ds(row0, KC), :]
            ind = jnp.where(tile >= cand, 1.0, 0.0)
            return part + jnp.sum(ind.reshape(KC // 8, 8, QB), axis=0)
        part = lax.fori_loop(0, i + 1, body, jnp.zeros((8, QB), F32))
        return jnp.sum(part, axis=0, keepdims=True)

    @pl.when(j == 0)
    def _prepare():
        def body(c, carry):
            mx, mn = carry
            cmx, cmn = chunk_scores(c, False)
            return jnp.maximum(mx, cmx), jnp.minimum(mn, cmn)
        mx0 = jnp.full((1, QB), -jnp.inf, F32)
        mn0 = jnp.full((1, QB), jnp.inf, F32)
        mx, mn = lax.fori_loop(0, i, body, (mx0, mn0))
        dmx, dmn = chunk_scores(i, True)
        mx = jnp.maximum(mx, dmx)
        mn = jnp.minimum(mn, dmn)

        kf = float(topk)
        n_causal = (i * QB + 1 + lax.broadcasted_iota(jnp.int32, (1, QB), 1)).astype(F32)
        few = n_causal <= kf
        lo = jnp.where(few, F32_LOWEST, mn)
        clo = n_causal
        hi = jnp.full((1, QB), F32_HIGHEST, F32)
        chi = jnp.zeros((1, QB), F32)
        done_f = jnp.where(few | (clo == kf), 1.0, 0.0)

        def cond(st):
            it, lo, hi, clo, chi, done_f = st
            return jnp.logical_and(it < SEARCH_MAX_ITERS, jnp.min(done_f) < 0.5)

        def step(st):
            it, lo, hi, clo, chi, done_f = st
            done = done_f > 0.5
            mid = 0.5 * lo + 0.5 * hi
            frac = (clo - (kf - 0.5)) / jnp.maximum(clo - chi, 1.0)
            interp = lo + (hi - lo) * frac
            first = it == 0
            use_interp = jnp.logical_and(it % 2 == 0, jnp.logical_not(first))
            cand = jnp.where(use_interp, interp, mid)
            cand = jnp.where(first, mx, cand)
            bad = jnp.logical_not((cand > lo) & (cand < hi))
            cand = jnp.where(bad, mid, cand)
            cand = jnp.where(done, lo, cand)
            cnt = count_ge(cand)
            ge = cnt >= kf
            upd = jnp.logical_not(done)
            lo = jnp.where(upd & ge, cand, lo)
            clo = jnp.where(upd & ge, cnt, clo)
            hi = jnp.where(upd & jnp.logical_not(ge), cand, hi)
            chi = jnp.where(upd & jnp.logical_not(ge), cnt, chi)
            mid2 = 0.5 * lo + 0.5 * hi
            stuck = jnp.logical_not((mid2 > lo) & (mid2 < hi))
            done_f = jnp.where(done | (clo == kf) | stuck, 1.0, 0.0)
            return it + 1, lo, hi, clo, chi, done_f

        _, lo, hi, clo, chi, _ = lax.while_loop(
            cond, step, (jnp.int32(0), lo, hi, clo, chi, done_f))

        need = kf - chi
        tied = jnp.logical_and(clo > kf, jnp.logical_not(few))

        @pl.when(jnp.max(tied.astype(F32)) > 0.5)
        def _break_ties():
            def count_eq_le(cut):
                def body(c, part):
                    row0 = pl.multiple_of(c * KC, KC)
                    tile = sc_ref[pl.ds(row0, KC), :]
                    kpos = (row0 + lax.broadcasted_iota(jnp.int32, (KC, QB), 0)).astype(F32)
                    ind = jnp.where((tile == lo) & (kpos <= cut), 1.0, 0.0)
                    return part + jnp.sum(ind.reshape(KC // 8, 8, QB), axis=0)
                part = lax.fori_loop(0, i + 1, body, jnp.zeros((8, QB), F32))
                return jnp.sum(part, axis=0, keepdims=True)

            def bis(_, st):
                a, b = st
                mid = jnp.floor(0.5 * (a + b))
                ok = count_eq_le(mid) <= need
                return jnp.where(ok, mid, a), jnp.where(ok, b, mid)

            a0 = jnp.full((1, QB), -1.0, F32)
            b0 = ((i + 1) * KC).astype(F32) + jnp.zeros((1, QB), F32)
            n_bits = int(np.ceil(np.log2(sc_ref.shape[0] + 2))) + 1
            cut, _ = lax.fori_loop(0, n_bits, bis, (a0, b0))

            def drop(c, _):
                row0 = pl.multiple_of(c * KC, KC)
                tile = sc_ref[pl.ds(row0, KC), :]
                kpos = (row0 + lax.broadcasted_iota(jnp.int32, (KC, QB), 0)).astype(F32)
                kill = tied & (tile == lo) & (kpos > cut)
                sc_ref[pl.ds(row0, KC), :] = jnp.where(kill, -jnp.inf, tile)
                return 0
            lax.fori_loop(0, i + 1, drop, 0)

        thr_ref[...] = jnp.broadcast_to(lo, thr_ref.shape)
        m_ref[...] = jnp.full(m_ref.shape, NEG_BIG, F32)
        l_ref[...] = jnp.zeros(l_ref.shape, F32)
        acc_ref[...] = jnp.zeros(acc_ref.shape, F32)

    row0 = pl.multiple_of(j * KC, KC)
    sel = sc_ref[pl.ds(row0, KC), :] >= thr_ref[0:1, :]
    kposf = (row0 + lax.broadcasted_iota(jnp.int32, (KC, QB), 0)).astype(F32)
    for h in range(N_SPARSE_HEADS):
        hs = slice(h * HEAD_DIM, (h + 1) * HEAD_DIM)
        s = jnp.dot(k_ref[:, hs], qT_ref[hs, :], preferred_element_type=F32)
        s = s + slope_ref[h] * kposf
        s = jnp.where(sel, s, NEG_BIG)
        m_old = m_ref[h:h + 1, :]
        m_new = jnp.maximum(m_old, jnp.max(s, axis=0, keepdims=True))
        p = jnp.exp(s - m_new)
        alpha = jnp.exp(m_old - m_new)
        l_ref[h:h + 1, :] = alpha * l_ref[h:h + 1, :] + jnp.sum(p, axis=0, keepdims=True)
        m_ref[h:h + 1, :] = m_new
        pv = jnp.dot(vT_ref[hs, :], p.astype(BF16), preferred_element_type=F32)
        acc_ref[hs, :] = alpha * acc_ref[hs, :] + pv

    @pl.when(j == i)
    def _finish():
        for h in range(N_SPARSE_HEADS):
            hs = slice(h * HEAD_DIM, (h + 1) * HEAD_DIM)
            o = acc_ref[hs, :] / l_ref[h:h + 1, :]
            o_ref[:, hs] = o.T.astype(o_ref.dtype)


def _dsa_attention(qT, k, vT, qiT, ki, wT, slopes, *, topk):
    S = k.shape[0]
    QB, KC = DSA_QB, DSA_KC
    assert S % QB == 0 and QB == KC
    nq = S // QB
    qi_tab = np.concatenate([np.full(i + 1, i, np.int32) for i in range(nq)])
    kj_tab = np.concatenate([np.arange(i + 1, dtype=np.int32) for i in range(nq)])
    n_steps = int(qi_tab.shape[0])
    W = N_SPARSE_HEADS * HEAD_DIM
    WI = N_IDX_HEADS * IDX_HEAD_DIM
    grid_spec = pltpu.PrefetchScalarGridSpec(
        num_scalar_prefetch=2,
        grid=(n_steps,),
        in_specs=[
            pl.BlockSpec((W, QB), lambda t, qi, kj: (0, qi[t])),
            pl.BlockSpec((KC, W), lambda t, qi, kj: (kj[t], 0)),
            pl.BlockSpec((W, KC), lambda t, qi, kj: (0, kj[t])),
            pl.BlockSpec((WI, QB), lambda t, qi, kj: (0, qi[t])),
            pl.BlockSpec((S, IDX_HEAD_DIM), lambda t, qi, kj: (0, 0)),
            pl.BlockSpec((N_IDX_HEADS, QB), lambda t, qi, kj: (0, qi[t])),
            pl.BlockSpec(memory_space=pltpu.SMEM),
        ],
        out_specs=pl.BlockSpec((QB, W), lambda t, qi, kj: (qi[t], 0)),
        scratch_shapes=[
            pltpu.VMEM((S, QB), F32),
            pltpu.VMEM((8, QB), F32),
            pltpu.VMEM((N_SPARSE_HEADS, QB), F32),
            pltpu.VMEM((N_SPARSE_HEADS, QB), F32),
            pltpu.VMEM((W, QB), F32),
        ],
    )
    return pl.pallas_call(
        functools.partial(_dsa_kernel, topk=topk),
        grid_spec=grid_spec,
        out_shape=jax.ShapeDtypeStruct((S, W), BF16),
        compiler_params=_cparams("arbitrary"),
        name="dsa_attention",
    )(jnp.asarray(qi_tab), jnp.asarray(kj_tab), qT, k, vT, qiT, ki, wT, slopes)


def _dil_kernel(slope_ref, q_ref, kc_ref, kp_ref, vc_ref, vp_ref, o_ref, lse_ref,
                *, span, dilation):
    m_blk = pl.program_id(1)
    Q = BLOCK
    qi = lax.broadcasted_iota(jnp.int32, (Q, 2 * Q), 0)
    kj = lax.broadcasted_iota(jnp.int32, (Q, 2 * Q), 1)
    step = Q + qi - kj
    mask = (step >= 0) & (step <= span)
    mask = mask & ((kj >= Q) | (m_blk > 0))
    dist = (step * dilation).astype(F32)
    scale = HEAD_DIM ** -0.5
    for h in range(HEADS_PER_DIL_GROUP):
        hs = slice(h * HEAD_DIM, (h + 1) * HEAD_DIM)
        q = q_ref[:, hs]
        kk = jnp.concatenate([kp_ref[:, hs], kc_ref[:, hs]], axis=0)
        vv = jnp.concatenate([vp_ref[:, hs], vc_ref[:, hs]], axis=0)
        s = lax.dot_general(q, kk, (((1,), (1,)), ((), ())),
                            preferred_element_type=F32) * scale
        s = s - slope_ref[h] * dist
        s = jnp.where(mask, s, NEG_BIG)
        mx = jnp.max(s, axis=-1, keepdims=True)
        p = jnp.exp(s - mx)
        l = jnp.sum(p, axis=-1, keepdims=True)
        o = jnp.dot(p.astype(BF16), vv, preferred_element_type=F32) / l
        o_ref[:, hs] = o
        lse_ref[:, hs] = jnp.broadcast_to(mx + jnp.log(l), (Q, HEAD_DIM))


def _dilated_group(bqkv, g, window, dilation, slopes_g):
    S = bqkv.shape[0]
    d = dilation
    span = window // d
    assert span <= BLOCK and S % (d * BLOCK) == 0
    n_sub = S // d
    nb = n_sub // BLOCK
    GW = HEADS_PER_DIL_GROUP * HEAD_DIM
    row_w = 3 * DIL_WIDTH
    view = bqkv.reshape(n_sub, d * row_w)
    cpb = row_w // GW
    qcol = g
    kcol = DIL_WIDTH // GW + g
    vcol = 2 * DIL_WIDTH // GW + g

    def cur(col):
        return pl.BlockSpec((BLOCK, GW), lambda r, m: (m, r * cpb + col))

    def prev(col):
        return pl.BlockSpec((BLOCK, GW), lambda r, m: (jnp.maximum(m - 1, 0), r * cpb + col))

    out_spec = pl.BlockSpec((BLOCK, GW), lambda r, m: (m, r))
    o, lse = pl.pallas_call(
        functools.partial(_dil_kernel, span=span, dilation=d),
        grid=(d, nb),
        in_specs=[pl.BlockSpec(memory_space=pltpu.SMEM),
                  cur(qcol), cur(kcol), prev(kcol), cur(vcol), prev(vcol)],
        out_specs=[out_spec, out_spec],
        out_shape=[jax.ShapeDtypeStruct((n_sub, d * GW), F32)] * 2,
        compiler_params=_cparams("parallel", "arbitrary"),
        name=f"dilated_attention_d{d}",
    )(slopes_g, view, view, view, view, view)
    return o.reshape(S, GW), lse.reshape(S, GW)


def _layer_norm(y, g, b):
    mu = jnp.mean(y, axis=-1, keepdims=True)
    yc = y - mu
    var = jnp.mean(yc * yc, axis=-1, keepdims=True)
    return yc * lax.rsqrt(var + LN_EPS) * g + b


def _merge_kernel(oa_ref, o1_ref, o2_ref, o3_ref, l1_ref, l2_ref, l3_ref,
                  ga_ref, gb_ref, x_ref, wa_ref, wb_ref, wo_ref, g_ref, b_ref,
                  h_ref, hb_ref):
    l1, l2, l3 = l1_ref[...], l2_ref[...], l3_ref[...]
    lm = jnp.maximum(jnp.maximum(l1, l2), l3)
    e1, e2, e3 = jnp.exp(l1 - lm), jnp.exp(l2 - lm), jnp.exp(l3 - lm)
    den = e1 + e2 + e3
    ob = (e1 / den) * o1_ref[...] + (e2 / den) * o2_ref[...] + (e3 / den) * o3_ref[...]
    ya = jnp.dot(oa_ref[...], wa_ref[...], preferred_element_type=F32)
    yb = jnp.dot(ob.astype(BF16), wb_ref[...], preferred_element_type=F32)
    merged = ga_ref[...] * ya + gb_ref[...] * yb
    mix = jnp.dot(merged.astype(BF16), wo_ref[...], preferred_element_type=F32)
    h = _layer_norm(DEEPNORM_ALPHA * x_ref[...] + mix, g_ref[...], b_ref[...])
    h_ref[...] = h
    hb_ref[...] = h.astype(BF16)


def _merge(oa, dil, gates, x2, wa, wb, wo, ln_g, ln_b, *, tm):
    S = x2.shape[0]
    (o1, l1), (o2, l2), (o3, l3) = dil
    GW = DIL_OUT_WIDTH

    def rows(w, col=0):
        return pl.BlockSpec((tm, w), lambda i: (i, col))

    def whole(a):
        return pl.BlockSpec(a.shape, lambda i: (0,) * a.ndim, pipeline_mode=pl.Buffered(1))

    return pl.pallas_call(
        _merge_kernel,
        grid=(S // tm,),
        in_specs=[rows(SPARSE_WIDTH), rows(GW), rows(GW), rows(GW), rows(GW), rows(GW), rows(GW),
                  rows(D_MODEL, 0), rows(D_MODEL, 1), rows(D_MODEL),
                  whole(wa), whole(wb), whole(wo), whole(ln_g), whole(ln_b)],
        out_specs=[rows(D_MODEL), rows(D_MODEL)],
        out_shape=[jax.ShapeDtypeStruct((S, D_MODEL), F32),
                   jax.ShapeDtypeStruct((S, D_MODEL), BF16)],
        compiler_params=_cparams("parallel"),
        name="merge_outproj_ln",
    )(oa, o1, o2, o3, l1, l2, l3, gates, gates, x2, wa, wb, wo, ln_g, ln_b)


def _ffn_kernel(hb_ref, h_ref, wg_ref, wu_ref, wd_ref, g_ref, b_ref, o_ref, acc_ref):
    f = pl.program_id(1)
    hb = hb_ref[...]
    gate = jnp.dot(hb, wg_ref[...].astype(BF16), preferred_element_type=F32)
    up = jnp.dot(hb, wu_ref[...].astype(BF16), preferred_element_type=F32)
    act = (gate * jax.nn.sigmoid(gate) * up).astype(BF16)
    part = jnp.dot(act, wd_ref[...].astype(BF16), preferred_element_type=F32)

    @pl.when(f == 0)
    def _():
        acc_ref[...] = part

    @pl.when(f > 0)
    def _():
        acc_ref[...] += part

    @pl.when(f == pl.num_programs(1) - 1)
    def _():
        o_ref[...] = _layer_norm(DEEPNORM_ALPHA * h_ref[...] + acc_ref[...],
                                 g_ref[...], b_ref[...])


def _ffn(hb, h, wg, wu, wd, ln_g, ln_b, *, tm, tf):
    S = h.shape[0]
    F = wg.shape[1]
    assert S % tm == 0 and F % tf == 0
    return pl.pallas_call(
        _ffn_kernel,
        grid=(S // tm, F // tf),
        in_specs=[pl.BlockSpec((tm, D_MODEL), lambda i, f: (i, 0)),
                  pl.BlockSpec((tm, D_MODEL), lambda i, f: (i, 0)),
                  pl.BlockSpec((D_MODEL, tf), lambda i, f: (0, f)),
                  pl.BlockSpec((D_MODEL, tf), lambda i, f: (0, f)),
                  pl.BlockSpec((tf, D_MODEL), lambda i, f: (f, 0)),
                  pl.BlockSpec((1, D_MODEL), lambda i, f: (0, 0)),
                  pl.BlockSpec((1, D_MODEL), lambda i, f: (0, 0))],
        out_specs=pl.BlockSpec((tm, D_MODEL), lambda i, f: (i, 0)),
        out_shape=jax.ShapeDtypeStruct((S, D_MODEL), F32),
        scratch_shapes=[pltpu.VMEM((tm, D_MODEL), F32)],
        compiler_params=_cparams("parallel", "arbitrary"),
        name="swiglu_ffn_ln",
    )(hb, h, wg, wu, wd, ln_g, ln_b)


def _mixer_inputs(xb, w_in):
    O = IN_OFFS
    tm = 512
    qT = _project(xb, w_in, O[0], SPARSE_WIDTH, tm=tm, tn=512, out_dtype=BF16,
                  scale=HEAD_DIM ** -0.5, transpose_out=True, name="proj_aqT")
    k = _project(xb, w_in, O[1], SPARSE_WIDTH, tm=tm, tn=512, out_dtype=BF16, name="proj_ak")
    vT = _project(xb, w_in, O[2], SPARSE_WIDTH, tm=tm, tn=512, out_dtype=BF16,
                  transpose_out=True, name="proj_avT")
    qiT = _project(xb, w_in, O[3], N_IDX_HEADS * IDX_HEAD_DIM, tm=tm, tn=512, out_dtype=BF16,
                   scale=IDX_HEAD_DIM ** -0.5, transpose_out=True, name="proj_iqT")
    kw = _project(xb, w_in, O[4], 128, tm=tm, tn=128, out_dtype=F32, name="proj_ik_iw")
    kwT = _project(xb, w_in, O[4], 128, tm=tm, tn=128, out_dtype=F32,
                   scale=N_IDX_HEADS ** -0.5, transpose_out=True, name="proj_ik_iwT")
    ki = kw[:, :IDX_HEAD_DIM].astype(BF16)
    wT = kwT[IDX_HEAD_DIM:IDX_HEAD_DIM + N_IDX_HEADS, :]
    w2 = w_in[:, O[6]:].astype(BF16)
    bqkv = _project(xb, w2, 0, 3 * DIL_WIDTH, tm=tm, tn=512, out_dtype=BF16, name="proj_bqkv")
    gates = _project(xb, w2, 3 * DIL_WIDTH, 2 * D_MODEL, tm=tm, tn=512, out_dtype=F32,
                     sigmoid=True, name="proj_gates")
    return qT, k, vT, qiT, ki, wT, bqkv, gates


def _layer(x2, w_in, w_a, w_b, w_out, ln1_g, ln1_b, w_gate, w_up, w_down, ln2_g, ln2_b):
    S = x2.shape[0]
    slopes = _alibi_slopes()
    xb = x2.astype(BF16)
    qT, k, vT, qiT, ki, wT, bqkv, gates = _mixer_inputs(xb, w_in)

    topk = min(TOPK_MAX, S // 4)
    sp_slopes = jnp.asarray(slopes[N_DIL_HEADS:], F32)
    oa = _dsa_attention(qT, k, vT, qiT, ki, wT, sp_slopes, topk=topk)

    dil = []
    for g, (win, d) in enumerate(DIL_GROUPS):
        sl = jnp.asarray(slopes[g * HEADS_PER_DIL_GROUP:(g + 1) * HEADS_PER_DIL_GROUP], F32)
        dil.append(_dilated_group(bqkv, g, win, d, sl))

    h, hb = _merge(oa, dil, gates, x2, w_a.astype(BF16), w_b.astype(BF16), w_out.astype(BF16),
                   ln1_g.reshape(1, -1), ln1_b.reshape(1, -1), tm=256)
    return _ffn(hb, h, w_gate, w_up, w_down, ln2_g.reshape(1, -1), ln2_b.reshape(1, -1),
                tm=512, tf=256)


def kernel(x, w_in, w_a, w_b, w_out, ln1_g, ln1_b, w_gate, w_up, w_down, ln2_g, ln2_b):
    B, S, D = x.shape
    outs = []
    for b in range(B):
        y = x[b]
        for l in range(w_in.shape[0]):
            y = _layer(y, w_in[l], w_a[l], w_b[l], w_out[l], ln1_g[l], ln1_b[l],
                       w_gate[l], w_up[l], w_down[l], ln2_g[l], ln2_b[l])
        outs.append(y)
    return jnp.stack(outs)
```

```python
import functools

import numpy as np
import jax
import jax.numpy as jnp
from jax import lax
from jax.experimental import pallas as pl
from jax.experimental.pallas import tpu as pltpu

F32 = jnp.float32
BF16 = jnp.bfloat16

D_MODEL = 2048
HEAD_DIM = 128
N_SPARSE_HEADS = 8
SPARSE_WIDTH = N_SPARSE_HEADS * HEAD_DIM
N_IDX_HEADS = 16
IDX_HEAD_DIM = 64
TOPK_MAX = 256
DIL_GROUPS = ((128, 1), (512, 4), (2048, 16))
HEADS_PER_DIL_GROUP = 4
N_DIL_HEADS = HEADS_PER_DIL_GROUP * len(DIL_GROUPS)
DIL_WIDTH = N_DIL_HEADS * HEAD_DIM
DIL_OUT_WIDTH = HEADS_PER_DIL_GROUP * HEAD_DIM
N_ALIBI_HEADS = N_DIL_HEADS + N_SPARSE_HEADS
BLOCK = 128
D_FF = ((-(-8 * D_MODEL // 3)) + 255) // 256 * 256
DEPTH = 1
DEEPNORM_ALPHA = (2 * DEPTH) ** 0.25
LN_EPS = 1e-5
IN_SIZES = (SPARSE_WIDTH, SPARSE_WIDTH, SPARSE_WIDTH,
            N_IDX_HEADS * IDX_HEAD_DIM, IDX_HEAD_DIM, N_IDX_HEADS,
            DIL_WIDTH, DIL_WIDTH, DIL_WIDTH, D_MODEL, D_MODEL)
IN_OFFS = tuple(int(c) for c in np.cumsum((0,) + IN_SIZES))

VMEM_LIMIT_BYTES = 56 * 1024 * 1024
LOG2_E = float(np.log2(np.e))
NEG_BIG = -1e30
F32_LOWEST = float(np.finfo(np.float32).min)
F32_HIGHEST = float(np.finfo(np.float32).max)


def _alibi_slopes():
    return [2.0 ** (-8.0 * (i + 1) / N_ALIBI_HEADS) for i in range(N_ALIBI_HEADS)]


def _cparams(*sem):
    return pltpu.CompilerParams(dimension_semantics=sem,
                                vmem_limit_bytes=VMEM_LIMIT_BYTES)


LANES = 128
ROW_ALIGN = 16
PROJ_TM = 1024
PROJ_TN = 512
IDX_BLOCK = 128


def _x_dot_wT(x_ref, w_ref):
    return lax.dot_general(x_ref[...], w_ref[...].astype(BF16), (((1,), (1,)), ((), ())),
                           preferred_element_type=F32)


def _w_rows_spec(tn, K, row_of_step):
    return pl.BlockSpec((pl.Element(tn), pl.Element(K)),
                        lambda i, j: (pl.multiple_of(row_of_step(j), ROW_ALIGN), 0))


def _proj_kernel(x_ref, w_ref, o_ref, *, scale, sigmoid, transpose_out):
    acc = _x_dot_wT(x_ref, w_ref)
    if scale != 1.0:
        acc = acc * scale
    if sigmoid:
        acc = jax.nn.sigmoid(acc)
    if transpose_out:
        acc = acc.T
    o_ref[...] = acc.astype(o_ref.dtype)


def _project(xb, wT, row0, ncols, *, out_dtype, scale=1.0, sigmoid=False,
             transpose_out=False, name):
    S, K = xb.shape
    tm, tn = min(PROJ_TM, S), PROJ_TN
    assert row0 % ROW_ALIGN == 0 and ncols % tn == 0 and S % tm == 0
    if transpose_out:
        out_shape = jax.ShapeDtypeStruct((ncols, S), out_dtype)
        out_spec = pl.BlockSpec((tn, tm), lambda i, j: (j, i))
    else:
        out_shape = jax.ShapeDtypeStruct((S, ncols), out_dtype)
        out_spec = pl.BlockSpec((tm, tn), lambda i, j: (i, j))
    return pl.pallas_call(
        functools.partial(_proj_kernel, scale=scale, sigmoid=sigmoid,
                          transpose_out=transpose_out),
        grid=(S // tm, ncols // tn),
        in_specs=[pl.BlockSpec((tm, K), lambda i, j: (i, 0)),
                  _w_rows_spec(tn, K, lambda j: row0 + j * tn)],
        out_specs=out_spec,
        out_shape=out_shape,
        compiler_params=_cparams("parallel", "arbitrary"),
        name=name,
    )(xb, wT)


def _proj_idx_kernel(x_ref, w_ref, k_ref, wT_ref, *, w_scale):
    acc = _x_dot_wT(x_ref, w_ref)
    k_ref[...] = acc.astype(k_ref.dtype)
    wT_ref[...] = (acc * w_scale).T


def _project_indexer_kw(xb, wT, row0, *, w_scale):
    S, K = xb.shape
    tm = min(PROJ_TM, S)
    return pl.pallas_call(
        functools.partial(_proj_idx_kernel, w_scale=w_scale),
        grid=(S // tm, 1),
        in_specs=[pl.BlockSpec((tm, K), lambda i, j: (i, 0)),
                  _w_rows_spec(IDX_BLOCK, K, lambda j: row0)],
        out_specs=[pl.BlockSpec((tm, IDX_BLOCK), lambda i, j: (i, 0)),
                   pl.BlockSpec((IDX_BLOCK, tm), lambda i, j: (0, i))],
        out_shape=[jax.ShapeDtypeStruct((S, IDX_BLOCK), BF16),
                   jax.ShapeDtypeStruct((IDX_BLOCK, S), F32)],
        compiler_params=_cparams("parallel", "arbitrary"),
        name="proj_ik_iw",
    )(xb, wT)


def _proj_dilated_kernel(x_ref, w_ref, o_ref, scr_ref, *, dilation):
    acc = _x_dot_wT(x_ref, w_ref)
    tm, tn = acc.shape
    if dilation == 1:
        o_ref[0] = acc.astype(o_ref.dtype)
        return
    for c in range(tn // LANES):
        scr_ref[c] = acc[:, c * LANES:(c + 1) * LANES]
    for r in range(dilation):
        for c in range(tn // LANES):
            rows = scr_ref[c, pl.ds(r, tm // dilation, stride=dilation), :]
            o_ref[r, :, c * LANES:(c + 1) * LANES] = rows.astype(o_ref.dtype)


def _project_dilated_qkv(xb, wT, row0, g, dilation):
    S, K = xb.shape
    d = dilation
    tm, tn = min(PROJ_TM, S), DIL_OUT_WIDTH
    assert S % tm == 0 and tm % (d * ROW_ALIGN) == 0
    return pl.pallas_call(
        functools.partial(_proj_dilated_kernel, dilation=d),
        grid=(S // tm, 3),
        in_specs=[pl.BlockSpec((tm, K), lambda i, j: (i, 0)),
                  _w_rows_spec(tn, K, lambda j: row0 + j * DIL_WIDTH + g * tn)],
        out_specs=pl.BlockSpec((None, d, tm // d, tn), lambda i, j: (j, 0, i, 0)),
        out_shape=jax.ShapeDtypeStruct((3, d, S // d, tn), BF16),
        scratch_shapes=[pltpu.VMEM((tn // LANES, tm, LANES), F32)],
        compiler_params=_cparams("parallel", "arbitrary"),
        name=f"proj_dil_qkv_d{d}",
    )(xb, wT)


DSA_QB = 256
DSA_KC = 256
SEARCH_MAX_ITERS = 512


def _dsa_kernel(qi_tab, kj_tab,
                qT_ref, k_ref, vT_ref, qiT_ref, ki_ref, wT_ref, pos_ref, srow_ref,
                o_ref,
                sc_ref, thr_ref, m_ref, l_ref, acc_ref, s_buf, p_buf, *, topk):
    QB, KC = DSA_QB, DSA_KC
    t = pl.program_id(0)
    i = qi_tab[t]
    j = kj_tab[t]

    def chunk_scores(c, diagonal):
        row0 = pl.multiple_of(c * KC, KC)
        kic = ki_ref[pl.ds(row0, KC), 0:IDX_HEAD_DIM]
        acc = jnp.zeros((KC, QB), F32)
        for h in range(N_IDX_HEADS):
            r = jnp.dot(kic, qiT_ref[h * IDX_HEAD_DIM:(h + 1) * IDX_HEAD_DIM, :],
                        preferred_element_type=F32)
            w_h = wT_ref[IDX_HEAD_DIM + h:IDX_HEAD_DIM + h + 1, :]
            acc = acc + w_h * jnp.maximum(r, 0.0)
        if diagonal:
            kpos = lax.broadcasted_iota(jnp.int32, (KC, QB), 0)
            qpos = lax.broadcasted_iota(jnp.int32, (KC, QB), 1)
            causal = kpos <= qpos
            mx = jnp.max(jnp.where(causal, acc, -jnp.inf), axis=0, keepdims=True)
            mn = jnp.min(jnp.where(causal, acc, jnp.inf), axis=0, keepdims=True)
            acc = jnp.where(causal, acc, -jnp.inf)
        else:
            mx = jnp.max(acc, axis=0, keepdims=True)
            mn = jnp.min(acc, axis=0, keepdims=True)
        sc_ref[pl.ds(row0, KC), :] = acc
        return mx, mn

    def count_ge(cand):
        def body(c, part):
            row0 = pl.multiple_of(c * KC, KC)
            tile = sc_ref[pl.ds(row0, KC), :]
            ind = jnp.where(tile >= cand, 1.0, 0.0)
            return part + jnp.sum(ind.reshape(KC // 8, 8, QB), axis=0)
        part = lax.fori_loop(0, i + 1, body, jnp.zeros((8, QB), F32))
        return jnp.sum(part, axis=0, keepdims=True)

    @pl.when(j == 0)
    def _prepare():
        def body(c, carry):
            mx, mn = carry
            cmx, cmn = chunk_scores(c, False)
            return jnp.maximum(mx, cmx), jnp.minimum(mn, cmn)
        mx0 = jnp.full((1, QB), -jnp.inf, F32)
        mn0 = jnp.full((1, QB), jnp.inf, F32)
        mx, mn = lax.fori_loop(0, i, body, (mx0, mn0))
        dmx, dmn = chunk_scores(i, True)
        mx = jnp.maximum(mx, dmx)
        mn = jnp.minimum(mn, dmn)

        kf = float(topk)
        n_causal = (i * QB + 1 + lax.broadcasted_iota(jnp.int32, (1, QB), 1)).astype(F32)
        few = n_causal <= kf
        lo = jnp.where(few, F32_LOWEST, mn)
        clo = n_causal
        hi = jnp.full((1, QB), F32_HIGHEST, F32)
        chi = jnp.zeros((1, QB), F32)
        done_f = jnp.where(few | (clo == kf), 1.0, 0.0)

        def cond(st):
            it, lo, hi, clo, chi, done_f = st
            return jnp.logical_and(it < SEARCH_MAX_ITERS, jnp.min(done_f) < 0.5)

        def step(st):
            it, lo, hi, clo, chi, done_f = st
            done = done_f > 0.5
            mid = 0.5 * lo + 0.5 * hi
            frac = (clo - (kf - 0.5)) / jnp.maximum(clo - chi, 1.0)
            interp = lo + (hi - lo) * frac
            first = it == 0
            use_interp = jnp.logical_and(it % 2 == 0, jnp.logical_not(first))
            cand = jnp.where(use_interp, interp, mid)
            cand = jnp.where(first, mx, cand)
            bad = jnp.logical_not((cand > lo) & (cand < hi))
            cand = jnp.where(bad, mid, cand)
            cand = jnp.where(done, lo, cand)
            cnt = count_ge(cand)
            ge = cnt >= kf
            upd = jnp.logical_not(done)
            lo = jnp.where(upd & ge, cand, lo)
            clo = jnp.where(upd & ge, cnt, clo)
            hi = jnp.where(upd & jnp.logical_not(ge), cand, hi)
            chi = jnp.where(upd & jnp.logical_not(ge), cnt, chi)
            mid2 = 0.5 * lo + 0.5 * hi
            stuck = jnp.logical_not((mid2 > lo) & (mid2 < hi))
            done_f = jnp.where(done | (clo == kf) | stuck, 1.0, 0.0)
            return it + 1, lo, hi, clo, chi, done_f

        _, lo, hi, clo, chi, _ = lax.while_loop(
            cond, step, (jnp.int32(0), lo, hi, clo, chi, done_f))

        need = kf - chi
        tied = jnp.logical_and(clo > kf, jnp.logical_not(few))

        @pl.when(jnp.max(tied.astype(F32)) > 0.5)
        def _break_ties():
            def count_eq_le(cut):
                def body(c, part):
                    row0 = pl.multiple_of(c * KC, KC)
                    tile = sc_ref[pl.ds(row0, KC), :]
                    kpos = (row0 + lax.broadcasted_iota(jnp.int32, (KC, QB), 0)).astype(F32)
                    ind = jnp.where((tile == lo) & (kpos <= cut), 1.0, 0.0)
                    return part + jnp.sum(ind.reshape(KC // 8, 8, QB), axis=0)
                part = lax.fori_loop(0, i + 1, body, jnp.zeros((8, QB), F32))
                return jnp.sum(part, axis=0, keepdims=True)

            def bis(_, st):
                a, b = st
                mid = jnp.floor(0.5 * (a + b))
                ok = count_eq_le(mid) <= need
                return jnp.where(ok, mid, a), jnp.where(ok, b, mid)

            a0 = jnp.full((1, QB), -1.0, F32)
            b0 = ((i + 1) * KC).astype(F32) + jnp.zeros((1, QB), F32)
            n_bits = int(np.ceil(np.log2(sc_ref.shape[0] + 2))) + 1
            cut, _ = lax.fori_loop(0, n_bits, bis, (a0, b0))

            def drop(c, _):
                row0 = pl.multiple_of(c * KC, KC)
                tile = sc_ref[pl.ds(row0, KC), :]
                kpos = (row0 + lax.broadcasted_iota(jnp.int32, (KC, QB), 0)).astype(F32)
                kill = tied & (tile == lo) & (kpos > cut)
                sc_ref[pl.ds(row0, KC), :] = jnp.where(kill, -jnp.inf, tile)
                return 0
            lax.fori_loop(0, i + 1, drop, 0)

        thr_ref[...] = jnp.broadcast_to(lo, thr_ref.shape)
        m_ref[...] = jnp.full(m_ref.shape, NEG_BIG, F32)
        l_ref[...] = jnp.zeros(l_ref.shape, F32)
        acc_ref[...] = jnp.zeros(acc_ref.shape, F32)

    row0 = pl.multiple_of(j * KC, KC)
    mask_add = jnp.where(sc_ref[pl.ds(row0, KC), :] >= thr_ref[0:1, :], 0.0, NEG_BIG)
    posk = pos_ref[...]
    col_max = []
    for h in range(N_SPARSE_HEADS):
        hs = slice(h * HEAD_DIM, (h + 1) * HEAD_DIM)
        lhs = jnp.concatenate([k_ref[:, hs], posk], axis=1)
        rhs = jnp.concatenate([qT_ref[hs, :], srow_ref[h]], axis=0)
        s = jnp.dot(lhs, rhs, preferred_element_type=F32) + mask_add
        s_buf[h] = s
        col_max.append(jnp.max(s, axis=0, keepdims=True))
    m_old = m_ref[...]
    m_new = jnp.maximum(m_old, jnp.concatenate(col_max, axis=0))
    alpha = jnp.exp2(m_old - m_new)
    m_ref[...] = m_new
    col_sum = []
    for h in range(N_SPARSE_HEADS):
        p = jnp.exp2(s_buf[h] - m_new[h:h + 1, :])
        col_sum.append(jnp.sum(p, axis=0, keepdims=True))
        p_buf[h] = p.astype(BF16)
    l_ref[...] = alpha * l_ref[...] + jnp.concatenate(col_sum, axis=0)
    for h in range(N_SPARSE_HEADS):
        hs = slice(h * HEAD_DIM, (h + 1) * HEAD_DIM)
        pv = jnp.dot(vT_ref[hs, :], p_buf[h], preferred_element_type=F32)
        acc_ref[hs, :] = alpha[h:h + 1, :] * acc_ref[hs, :] + pv

    @pl.when(j == i)
    def _finish():
        for h in range(N_SPARSE_HEADS):
            hs = slice(h * HEAD_DIM, (h + 1) * HEAD_DIM)
            o = acc_ref[hs, :] / l_ref[h:h + 1, :]
            o_ref[:, hs] = o.T.astype(o_ref.dtype)


N_SLOPE_PIECES = 3
POS_SPLIT = 128


def _alibi_mxu_operands(S, slopes_log2):
    assert S <= POS_SPLIT * 256
    kpos = jnp.arange(S, dtype=jnp.int32)[:, None]
    col = jnp.arange(128, dtype=jnp.int32)[None, :]
    part = jnp.where(col % 2 == 0, kpos // POS_SPLIT, kpos % POS_SPLIT)
    pos = jnp.where(col < 2 * N_SLOPE_PIECES, part, 0).astype(BF16)
    rows = np.zeros((len(slopes_log2), 128), np.float32)
    for h, s in enumerate(slopes_log2):
        rest = np.float64(s)
        for p in range(N_SLOPE_PIECES):
            piece = np.float64(np.asarray(rest, np.float32).astype(jnp.bfloat16).astype(np.float32))
            rows[h, 2 * p] = POS_SPLIT * piece
            rows[h, 2 * p + 1] = piece
            rest = rest - piece
    srow = jnp.broadcast_to(jnp.asarray(rows, F32)[:, :, None],
                            (len(slopes_log2), 128, DSA_QB)).astype(BF16)
    return pos, srow


def _dsa_attention(qT, k, vT, qiT, ki, wT, slopes, *, topk):
    S = k.shape[0]
    QB, KC = DSA_QB, DSA_KC
    assert S % QB == 0 and QB == KC
    nq = S // QB
    pos, srow = _alibi_mxu_operands(S, [s * LOG2_E for s in slopes])
    qi_tab = np.concatenate([np.full(i + 1, i, np.int32) for i in range(nq)])
    kj_tab = np.concatenate([np.arange(i + 1, dtype=np.int32) for i in range(nq)])
    n_steps = int(qi_tab.shape[0])
    W = N_SPARSE_HEADS * HEAD_DIM
    WI = N_IDX_HEADS * IDX_HEAD_DIM
    grid_spec = pltpu.PrefetchScalarGridSpec(
        num_scalar_prefetch=2,
        grid=(n_steps,),
        in_specs=[
            pl.BlockSpec((W, QB), lambda t, qi, kj: (0, qi[t])),
            pl.BlockSpec((KC, W), lambda t, qi, kj: (kj[t], 0)),
            pl.BlockSpec((W, KC), lambda t, qi, kj: (0, kj[t])),
            pl.BlockSpec((WI, QB), lambda t, qi, kj: (0, qi[t])),
            pl.BlockSpec((S, IDX_BLOCK), lambda t, qi, kj: (0, 0)),
            pl.BlockSpec((IDX_BLOCK, QB), lambda t, qi, kj: (0, qi[t])),
            pl.BlockSpec((KC, 128), lambda t, qi, kj: (kj[t], 0)),
            pl.BlockSpec(srow.shape, lambda t, qi, kj: (0, 0, 0)),
        ],
        out_specs=pl.BlockSpec((QB, W), lambda t, qi, kj: (qi[t], 0)),
        scratch_shapes=[
            pltpu.VMEM((S, QB), F32),
            pltpu.VMEM((8, QB), F32),
            pltpu.VMEM((N_SPARSE_HEADS, QB), F32),
            pltpu.VMEM((N_SPARSE_HEADS, QB), F32),
            pltpu.VMEM((W, QB), F32),
            pltpu.VMEM((N_SPARSE_HEADS, KC, QB), F32),
            pltpu.VMEM((N_SPARSE_HEADS, KC, QB), BF16),
        ],
    )
    return pl.pallas_call(
        functools.partial(_dsa_kernel, topk=topk),
        grid_spec=grid_spec,
        out_shape=jax.ShapeDtypeStruct((S, W), BF16),
        compiler_params=_cparams("arbitrary"),
        name="dsa_attention",
    )(jnp.asarray(qi_tab), jnp.asarray(kj_tab), qT, k, vT, qiT, ki, wT, pos, srow)


def _dil_kernel(slope_ref, q_ref, kc_ref, kp_ref, vc_ref, vp_ref, o_ref, lse_ref,
                *, span, dilation):
    m_blk = pl.program_id(1)
    Q = BLOCK
    qi = lax.broadcasted_iota(jnp.int32, (Q, 2 * Q), 0)
    kj = lax.broadcasted_iota(jnp.int32, (Q, 2 * Q), 1)
    step = Q + qi - kj
    mask = (step >= 0) & (step <= span)
    mask = mask & ((kj >= Q) | (m_blk > 0))
    dist = (step * dilation).astype(F32)
    scale = HEAD_DIM ** -0.5
    for h in range(HEADS_PER_DIL_GROUP):
        hs = slice(h * HEAD_DIM, (h + 1) * HEAD_DIM)
        q = q_ref[:, hs]
        kk = jnp.concatenate([kp_ref[:, hs], kc_ref[:, hs]], axis=0)
        vv = jnp.concatenate([vp_ref[:, hs], vc_ref[:, hs]], axis=0)
        s = lax.dot_general(q, kk, (((1,), (1,)), ((), ())),
                            preferred_element_type=F32) * scale
        s = s - slope_ref[h] * dist
        s = jnp.where(mask, s, NEG_BIG)
        mx = jnp.max(s, axis=-1, keepdims=True)
        p = jnp.exp(s - mx)
        l = jnp.sum(p, axis=-1, keepdims=True)
        o = jnp.dot(p.astype(BF16), vv, preferred_element_type=F32) / l
        o_ref[:, hs] = o
        lse_ref[:, hs] = jnp.broadcast_to(mx + jnp.log(l), (Q, HEAD_DIM))


def _dilated_group(qkv, window, dilation, slopes_g):
    _, d, n_sub, GW = qkv.shape
    assert d == dilation
    span = window // d
    assert span <= BLOCK and n_sub % BLOCK == 0
    nb = n_sub // BLOCK

    def cur(which):
        return pl.BlockSpec((None, None, BLOCK, GW), lambda r, m: (which, r, m, 0))

    def prev(which):
        return pl.BlockSpec((None, None, BLOCK, GW),
                            lambda r, m: (which, r, jnp.maximum(m - 1, 0), 0))

    out_spec = pl.BlockSpec((None, BLOCK, GW), lambda r, m: (r, m, 0))
    return pl.pallas_call(
        functools.partial(_dil_kernel, span=span, dilation=d),
        grid=(d, nb),
        in_specs=[pl.BlockSpec(memory_space=pltpu.SMEM),
                  cur(0), cur(1), prev(1), cur(2), prev(2)],
        out_specs=[out_spec, out_spec],
        out_shape=[jax.ShapeDtypeStruct((d, n_sub, GW), F32)] * 2,
        compiler_params=_cparams("parallel", "arbitrary"),
        name=f"dilated_attention_d{d}",
    )(slopes_g, qkv, qkv, qkv, qkv, qkv)


def _layer_norm(y, g, b):
    mu = jnp.mean(y, axis=-1, keepdims=True)
    yc = y - mu
    var = jnp.mean(yc * yc, axis=-1, keepdims=True)
    return yc * lax.rsqrt(var + LN_EPS) * g + b


def _merge_kernel(oa_ref, o1_ref, o2_ref, o3_ref, l1_ref, l2_ref, l3_ref,
                  ga_ref, gb_ref, x_ref, wa_ref, wb_ref, wo_ref, g_ref, b_ref,
                  h_ref, hb_ref, nat_ref):
    tm = x_ref.shape[0]
    n_slab = DIL_OUT_WIDTH // LANES

    def natural(ref, slot):
        d = ref.shape[0]
        if d == 1:
            return [ref[0, :, c * LANES:(c + 1) * LANES] for c in range(n_slab)]
        for r in range(d):
            for c in range(n_slab):
                nat_ref[slot, c, pl.ds(r, tm // d, stride=d), :] = ref[r, :, c * LANES:(c + 1) * LANES]
        return [nat_ref[slot, c] for c in range(n_slab)]

    o1, l1 = natural(o1_ref, 0), natural(l1_ref, 1)
    o2, l2 = natural(o2_ref, 2), natural(l2_ref, 3)
    o3, l3 = natural(o3_ref, 4), natural(l3_ref, 5)
    slabs = []
    for c in range(n_slab):
        lm = jnp.maximum(jnp.maximum(l1[c], l2[c]), l3[c])
        e1, e2, e3 = jnp.exp(l1[c] - lm), jnp.exp(l2[c] - lm), jnp.exp(l3[c] - lm)
        den = e1 + e2 + e3
        slabs.append((e1 / den) * o1[c] + (e2 / den) * o2[c] + (e3 / den) * o3[c])
    ob = jnp.concatenate(slabs, axis=1)
    ya = jnp.dot(oa_ref[...], wa_ref[...], preferred_element_type=F32)
    yb = jnp.dot(ob.astype(BF16), wb_ref[...], preferred_element_type=F32)
    merged = ga_ref[...] * ya + gb_ref[...] * yb
    mix = jnp.dot(merged.astype(BF16), wo_ref[...], preferred_element_type=F32)
    h = _layer_norm(DEEPNORM_ALPHA * x_ref[...] + mix, g_ref[...], b_ref[...])
    h_ref[...] = h
    hb_ref[...] = h.astype(BF16)


MERGE_TM = 256


def _merge(oa, dil, gates, x2, wa, wb, wo, ln_g, ln_b):
    S = x2.shape[0]
    tm = MERGE_TM
    (o1, l1), (o2, l2), (o3, l3) = dil
    GW = DIL_OUT_WIDTH

    def rows(w, col=0):
        return pl.BlockSpec((tm, w), lambda i: (i, col))

    def residue_major(a):
        d = a.shape[0]
        assert tm % (8 * d) == 0
        return pl.BlockSpec((d, tm // d, GW), lambda i: (0, i, 0))

    def whole(a):
        return pl.BlockSpec(a.shape, lambda i: (0,) * a.ndim, pipeline_mode=pl.Buffered(1))

    return pl.pallas_call(
        _merge_kernel,
        grid=(S // tm,),
        in_specs=[rows(SPARSE_WIDTH),
                  residue_major(o1), residue_major(o2), residue_major(o3),
                  residue_major(l1), residue_major(l2), residue_major(l3),
                  rows(D_MODEL, 0), rows(D_MODEL, 1), rows(D_MODEL),
                  whole(wa), whole(wb), whole(wo), whole(ln_g), whole(ln_b)],
        out_specs=[rows(D_MODEL), rows(D_MODEL)],
        out_shape=[jax.ShapeDtypeStruct((S, D_MODEL), F32),
                   jax.ShapeDtypeStruct((S, D_MODEL), BF16)],
        scratch_shapes=[pltpu.VMEM((6, GW // LANES, tm, LANES), F32)],
        compiler_params=_cparams("parallel"),
        name="merge_outproj_ln",
    )(oa, o1, o2, o3, l1, l2, l3, gates, gates, x2, wa, wb, wo, ln_g, ln_b)


def _ffn_kernel(hb_ref, h_ref, wg_ref, wu_ref, wd_ref, g_ref, b_ref, o_ref):
    f = pl.program_id(1)
    hb = hb_ref[...]
    gate = jnp.dot(hb, wg_ref[...].astype(BF16), preferred_element_type=F32)
    up = jnp.dot(hb, wu_ref[...].astype(BF16), preferred_element_type=F32)
    act = (gate * jax.nn.sigmoid(gate) * up).astype(BF16)
    half = o_ref.shape[1] // 2
    for c in range(2):
        cs = slice(c * half, (c + 1) * half)
        part = jnp.dot(act, wd_ref[:, cs].astype(BF16), preferred_element_type=F32)

        @pl.when(f == 0)
        def _():
            o_ref[:, cs] = part

        @pl.when(f > 0)
        def _():
            o_ref[:, cs] += part

    @pl.when(f == pl.num_programs(1) - 1)
    def _():
        o_ref[...] = _layer_norm(DEEPNORM_ALPHA * h_ref[...] + o_ref[...],
                                 g_ref[...], b_ref[...])


FFN_TM = 1024
FFN_TF = 256


def _ffn(hb, h, wg, wu, wd, ln_g, ln_b):
    S = h.shape[0]
    F = wg.shape[1]
    tm, tf = min(FFN_TM, S), FFN_TF
    assert S % tm == 0 and F % tf == 0

    per_row_tile = pl.BlockSpec((tm, D_MODEL), lambda i, f: (i, 0), pipeline_mode=pl.Buffered(1))

    return pl.pallas_call(
        _ffn_kernel,
        grid=(S // tm, F // tf),
        in_specs=[per_row_tile, per_row_tile,
                  pl.BlockSpec((D_MODEL, tf), lambda i, f: (0, f)),
                  pl.BlockSpec((D_MODEL, tf), lambda i, f: (0, f)),
                  pl.BlockSpec((tf, D_MODEL), lambda i, f: (f, 0)),
                  pl.BlockSpec((1, D_MODEL), lambda i, f: (0, 0)),
                  pl.BlockSpec((1, D_MODEL), lambda i, f: (0, 0))],
        out_specs=pl.BlockSpec((tm, D_MODEL), lambda i, f: (i, 0)),
        out_shape=jax.ShapeDtypeStruct((S, D_MODEL), F32),
        compiler_params=_cparams("parallel", "arbitrary"),
        name="swiglu_ffn_ln",
    )(hb, h, wg, wu, wd, ln_g, ln_b)


def _mixer_inputs(xb, w_in):
    O = IN_OFFS
    wT = jnp.transpose(w_in)
    qT = _project(xb, wT, O[0], SPARSE_WIDTH, out_dtype=BF16,
                  scale=LOG2_E * HEAD_DIM ** -0.5, transpose_out=True, name="proj_aqT")
    k = _project(xb, wT, O[1], SPARSE_WIDTH, out_dtype=BF16, name="proj_ak")
    vT = _project(xb, wT, O[2], SPARSE_WIDTH, out_dtype=BF16, transpose_out=True,
                  name="proj_avT")
    qiT = _project(xb, wT, O[3], N_IDX_HEADS * IDX_HEAD_DIM, out_dtype=BF16,
                   scale=IDX_HEAD_DIM ** -0.5, transpose_out=True, name="proj_iqT")
    ki, iwT = _project_indexer_kw(xb, wT, O[4], w_scale=N_IDX_HEADS ** -0.5)
    dil_qkv = [_project_dilated_qkv(xb, wT, O[6], g, d) for g, (_, d) in enumerate(DIL_GROUPS)]
    gates = _project(xb, wT, O[9], 2 * D_MODEL, out_dtype=F32, sigmoid=True, name="proj_gates")
    return qT, k, vT, qiT, ki, iwT, dil_qkv, gates


def _layer(x2, w_in, w_a, w_b, w_out, ln1_g, ln1_b, w_gate, w_up, w_down, ln2_g, ln2_b):
    S = x2.shape[0]
    slopes = _alibi_slopes()
    xb = x2.astype(BF16)
    qT, k, vT, qiT, ki, iwT, dil_qkv, gates = _mixer_inputs(xb, w_in)

    topk = min(TOPK_MAX, S // 4)
    oa = _dsa_attention(qT, k, vT, qiT, ki, iwT, slopes[N_DIL_HEADS:], topk=topk)

    dil = []
    for g, (win, d) in enumerate(DIL_GROUPS):
        sl = jnp.asarray(slopes[g * HEADS_PER_DIL_GROUP:(g + 1) * HEADS_PER_DIL_GROUP], F32)
        dil.append(_dilated_group(dil_qkv[g], win, d, sl))

    h, hb = _merge(oa, dil, gates, x2, w_a.astype(BF16), w_b.astype(BF16), w_out.astype(BF16),
                   ln1_g.reshape(1, -1), ln1_b.reshape(1, -1))
    return _ffn(hb, h, w_gate, w_up, w_down, ln2_g.reshape(1, -1), ln2_b.reshape(1, -1))


def kernel(x, w_in, w_a, w_b, w_out, ln1_g, ln1_b, w_gate, w_up, w_down, ln2_g, ln2_b):
    B, S, D = x.shape
    outs = []
    for b in range(B):
        y = x[b]
        for l in range(w_in.shape[0]):
            y = _layer(y, w_in[l], w_a[l], w_b[l], w_out[l], ln1_g[l], ln1_b[l],
                       w_gate[l], w_up[l], w_down[l], ln2_g[l], ln2_b[l])
        outs.append(y)
    return jnp.stack(outs)
```

```python
import functools
import statistics

import numpy as np
import jax
import jax.numpy as jnp
from jax import lax
from jax.experimental import pallas as pl
from jax.experimental.pallas import tpu as pltpu

F32 = jnp.float32
BF16 = jnp.bfloat16

D_MODEL = 2048
HEAD_DIM = 128
N_SPARSE_HEADS = 8
SPARSE_WIDTH = N_SPARSE_HEADS * HEAD_DIM
N_IDX_HEADS = 16
IDX_HEAD_DIM = 64
TOPK_MAX = 256
DIL_GROUPS = ((128, 1), (512, 4), (2048, 16))
HEADS_PER_DIL_GROUP = 4
N_DIL_HEADS = HEADS_PER_DIL_GROUP * len(DIL_GROUPS)
DIL_WIDTH = N_DIL_HEADS * HEAD_DIM
DIL_OUT_WIDTH = HEADS_PER_DIL_GROUP * HEAD_DIM
N_ALIBI_HEADS = N_DIL_HEADS + N_SPARSE_HEADS
BLOCK = 128
D_FF = ((-(-8 * D_MODEL // 3)) + 255) // 256 * 256
DEPTH = 1
DEEPNORM_ALPHA = (2 * DEPTH) ** 0.25
LN_EPS = 1e-5
IN_SIZES = (SPARSE_WIDTH, SPARSE_WIDTH, SPARSE_WIDTH,
            N_IDX_HEADS * IDX_HEAD_DIM, IDX_HEAD_DIM, N_IDX_HEADS,
            DIL_WIDTH, DIL_WIDTH, DIL_WIDTH, D_MODEL, D_MODEL)
IN_OFFS = tuple(int(c) for c in np.cumsum((0,) + IN_SIZES))

VMEM_LIMIT_BYTES = 56 * 1024 * 1024
LOG2_E = float(np.log2(np.e))
NEG_BIG = -1e30
F32_LOWEST = float(np.finfo(np.float32).min)
F32_HIGHEST = float(np.finfo(np.float32).max)


def _alibi_slopes():
    return [2.0 ** (-8.0 * (i + 1) / N_ALIBI_HEADS) for i in range(N_ALIBI_HEADS)]


def _cparams(*sem):
    return pltpu.CompilerParams(dimension_semantics=sem,
                                vmem_limit_bytes=VMEM_LIMIT_BYTES)


LANES = 128
ROW_ALIGN = 16
PROJ_TM = 1024
PROJ_TN = 512
IDX_BLOCK = 128


def _x_dot_wT(x_ref, w_ref):
    return lax.dot_general(x_ref[...], w_ref[...].astype(BF16), (((1,), (1,)), ((), ())),
                           preferred_element_type=F32)


def _w_rows_spec(tn, K, row_of_step):
    return pl.BlockSpec((pl.Element(tn), pl.Element(K)),
                        lambda i, j: (pl.multiple_of(row_of_step(j), ROW_ALIGN), 0))


def _proj_kernel(x_ref, w_ref, o_ref, *, scale, sigmoid, transpose_out):
    acc = _x_dot_wT(x_ref, w_ref)
    if scale != 1.0:
        acc = acc * scale
    if sigmoid:
        acc = jax.nn.sigmoid(acc)
    if transpose_out:
        acc = acc.T
    o_ref[...] = acc.astype(o_ref.dtype)


def _project(xb, wT, row0, ncols, *, out_dtype, scale=1.0, sigmoid=False,
             transpose_out=False, name):
    S, K = xb.shape
    tm, tn = min(PROJ_TM, S), PROJ_TN
    assert row0 % ROW_ALIGN == 0 and ncols % tn == 0 and S % tm == 0
    if transpose_out:
        out_shape = jax.ShapeDtypeStruct((ncols, S), out_dtype)
        out_spec = pl.BlockSpec((tn, tm), lambda i, j: (j, i))
    else:
        out_shape = jax.ShapeDtypeStruct((S, ncols), out_dtype)
        out_spec = pl.BlockSpec((tm, tn), lambda i, j: (i, j))
    return pl.pallas_call(
        functools.partial(_proj_kernel, scale=scale, sigmoid=sigmoid,
                          transpose_out=transpose_out),
        grid=(S // tm, ncols // tn),
        in_specs=[pl.BlockSpec((tm, K), lambda i, j: (i, 0)),
                  _w_rows_spec(tn, K, lambda j: row0 + j * tn)],
        out_specs=out_spec,
        out_shape=out_shape,
        compiler_params=_cparams("parallel", "arbitrary"),
        name=name,
    )(xb, wT)


def _proj_idx_kernel(x_ref, w_ref, k_ref, wT_ref, *, w_scale):
    acc = _x_dot_wT(x_ref, w_ref)
    k_ref[...] = acc.astype(k_ref.dtype)
    wT_ref[...] = (acc * w_scale).T


def _project_indexer_kw(xb, wT, row0, *, w_scale):
    S, K = xb.shape
    tm = min(PROJ_TM, S)
    return pl.pallas_call(
        functools.partial(_proj_idx_kernel, w_scale=w_scale),
        grid=(S // tm, 1),
        in_specs=[pl.BlockSpec((tm, K), lambda i, j: (i, 0)),
                  _w_rows_spec(IDX_BLOCK, K, lambda j: row0)],
        out_specs=[pl.BlockSpec((tm, IDX_BLOCK), lambda i, j: (i, 0)),
                   pl.BlockSpec((IDX_BLOCK, tm), lambda i, j: (0, i))],
        out_shape=[jax.ShapeDtypeStruct((S, IDX_BLOCK), BF16),
                   jax.ShapeDtypeStruct((IDX_BLOCK, S), F32)],
        compiler_params=_cparams("parallel", "arbitrary"),
        name="proj_ik_iw",
    )(xb, wT)


def _proj_dilated_kernel(x_ref, w_ref, o_ref, scr_ref, *, dilation):
    acc = _x_dot_wT(x_ref, w_ref)
    tm, tn = acc.shape
    if dilation == 1:
        o_ref[0] = acc.astype(o_ref.dtype)
        return
    for c in range(tn // LANES):
        scr_ref[c] = acc[:, c * LANES:(c + 1) * LANES]
    for r in range(dilation):
        for c in range(tn // LANES):
            rows = scr_ref[c, pl.ds(r, tm // dilation, stride=dilation), :]
            o_ref[r, :, c * LANES:(c + 1) * LANES] = rows.astype(o_ref.dtype)


def _project_dilated_qkv(xb, wT, row0, g, dilation):
    S, K = xb.shape
    d = dilation
    tm, tn = min(PROJ_TM, S), DIL_OUT_WIDTH
    assert S % tm == 0 and tm % (d * ROW_ALIGN) == 0
    return pl.pallas_call(
        functools.partial(_proj_dilated_kernel, dilation=d),
        grid=(S // tm, 3),
        in_specs=[pl.BlockSpec((tm, K), lambda i, j: (i, 0)),
                  _w_rows_spec(tn, K, lambda j: row0 + j * DIL_WIDTH + g * tn)],
        out_specs=pl.BlockSpec((None, d, tm // d, tn), lambda i, j: (j, 0, i, 0)),
        out_shape=jax.ShapeDtypeStruct((3, d, S // d, tn), BF16),
        scratch_shapes=[pltpu.VMEM((tn // LANES, tm, LANES), F32)],
        compiler_params=_cparams("parallel", "arbitrary"),
        name=f"proj_dil_qkv_d{d}",
    )(xb, wT)


DSA_QB = 256
DSA_KC = 256
SEARCH_MAX_ITERS = 512
SEARCH_BISECT_EVERY = 4
SEARCH_FIRST_STEP = 0.0625
ATTN_SLAB = 16
ATTN_HEAD_GROUP = 8
COUNT_SLAB = 16
COUNT_ACCUMULATORS = 2


def _dsa_kernel(qi_tab, kj_tab,
                qT_ref, k_ref, vT_ref, qiT_ref, ki_ref, wT_ref, pos_ref, srow_ref, zq_ref,
                o_ref,
                sc_ref, thr_ref, m_ref, l_ref, acc_ref, s_buf, p_buf, mask_buf, *, topk):
    QB, KC = DSA_QB, DSA_KC
    t = pl.program_id(0)
    i = qi_tab[t]
    j = kj_tab[t]

    def chunk_scores(c, diagonal):
        row0 = pl.multiple_of(c * KC, KC)
        kic = ki_ref[pl.ds(row0, KC), 0:IDX_HEAD_DIM]
        acc = jnp.zeros((KC, QB), F32)
        for h in range(N_IDX_HEADS):
            r = jnp.dot(kic, qiT_ref[h * IDX_HEAD_DIM:(h + 1) * IDX_HEAD_DIM, :],
                        preferred_element_type=F32)
            w_h = wT_ref[IDX_HEAD_DIM + h:IDX_HEAD_DIM + h + 1, :]
            acc = acc + w_h * jnp.maximum(r, 0.0)
        if diagonal:
            kpos = lax.broadcasted_iota(jnp.int32, (KC, QB), 0)
            qpos = lax.broadcasted_iota(jnp.int32, (KC, QB), 1)
            causal = kpos <= qpos
            live = jnp.where(causal, acc, 0.0)
            acc = jnp.where(causal, acc, -jnp.inf)
        else:
            live = acc
        sc_ref[pl.ds(row0, KC), :] = acc
        s1 = jnp.sum(live.reshape(KC // 8, 8, QB), axis=0)
        s2 = jnp.sum((live * live).reshape(KC // 8, 8, QB), axis=0)
        return s1, s2

    def count_ge(cand):
        candb = jnp.broadcast_to(cand, (COUNT_SLAB, QB))

        def body(c, parts):
            row0 = pl.multiple_of(c * (2 * KC), 2 * KC)
            parts = list(parts)
            for r in range(2 * KC // COUNT_SLAB):
                slab = sc_ref[pl.ds(row0 + r * COUNT_SLAB, COUNT_SLAB), :]
                parts[r % len(parts)] = parts[r % len(parts)] + jnp.where(slab >= candb, 1.0, 0.0)
            return tuple(parts)
        zero = jnp.zeros((COUNT_SLAB, QB), F32)
        parts = lax.fori_loop(0, (i + 2) // 2, body, (zero,) * COUNT_ACCUMULATORS)
        return jnp.sum(sum(parts), axis=0, keepdims=True)

    @pl.when(j == 0)
    def _prepare():
        def body(c, carry):
            s1, s2 = carry
            c1, c2 = chunk_scores(c, False)
            return s1 + c1, s2 + c2
        zero8 = jnp.zeros((8, QB), F32)
        s1, s2 = lax.fori_loop(0, i, body, (zero8, zero8))
        d1, d2 = chunk_scores(i, True)
        s1 = jnp.sum(s1 + d1, axis=0, keepdims=True)
        s2 = jnp.sum(s2 + d2, axis=0, keepdims=True)

        @pl.when(i % 2 == 0)
        def _pad():
            sc_ref[pl.ds(pl.multiple_of((i + 1) * KC, KC), KC), :] = jnp.full((KC, QB), -jnp.inf, F32)

        kf = float(topk)
        log_k = float(np.log(topk - 0.5))
        n_causal = (i * QB + 1 + lax.broadcasted_iota(jnp.int32, (1, QB), 1)).astype(F32)
        mean = s1 / n_causal
        std = jnp.sqrt(jnp.maximum(s2 / n_causal - mean * mean, 0.0))
        zq = jnp.max(zq_ref[...], axis=0, keepdims=True)
        guess = mean + zq * std
        few = n_causal <= kf
        lo = jnp.full((1, QB), F32_LOWEST, F32)
        clo = n_causal
        hi = jnp.full((1, QB), F32_HIGHEST, F32)
        chi = jnp.zeros((1, QB), F32)
        done_f = jnp.where(few | (clo == kf), 1.0, 0.0)
        zero = jnp.zeros((1, QB), F32)
        one = jnp.ones((1, QB), F32)

        def cond(st):
            return jnp.logical_and(st[0] < SEARCH_MAX_ITERS, jnp.min(st[5]) < 0.5)

        def step(st):
            it, lo, hi, clo, chi, done_f, n_up, n_dn, w_lo, w_hi, last = st
            done = done_f > 0.5
            mid = 0.5 * lo + 0.5 * hi
            f_lo = (jnp.log(clo) - log_k) * w_lo
            f_hi = (log_k - jnp.log(jnp.maximum(chi, 0.5))) * w_hi
            interp = lo + (hi - lo) * (f_lo / jnp.maximum(f_lo + f_hi, 1e-9))
            open_hi = hi >= F32_HIGHEST
            open_lo = lo <= F32_LOWEST
            cand = jnp.where(it % SEARCH_BISECT_EVERY == SEARCH_BISECT_EVERY - 1, mid, interp)
            cand = jnp.where(open_hi, lo + std * (SEARCH_FIRST_STEP * jnp.exp2(n_up)), cand)
            cand = jnp.where(open_lo, hi - std * (SEARCH_FIRST_STEP * jnp.exp2(n_dn)), cand)
            cand = jnp.where(open_lo & open_hi, guess, cand)
            bad = jnp.logical_not((cand > lo) & (cand < hi))
            cand = jnp.where(bad, mid, cand)
            cand = jnp.where(done, lo, cand)
            cnt = count_ge(cand)
            up = jnp.logical_not(done) & (cnt >= kf)
            dn = jnp.logical_not(done) & (cnt < kf)
            n_up = n_up + jnp.where(up & open_hi, 1.0, 0.0)
            n_dn = n_dn + jnp.where(dn & open_lo, 1.0, 0.0)
            w_hi = jnp.where(up, jnp.where(last == 1.0, 0.5 * w_hi, 1.0), jnp.where(dn, 1.0, w_hi))
            w_lo = jnp.where(dn, jnp.where(last == 2.0, 0.5 * w_lo, 1.0), jnp.where(up, 1.0, w_lo))
            last = jnp.where(up, 1.0, jnp.where(dn, 2.0, last))
            lo = jnp.where(up, cand, lo)
            clo = jnp.where(up, cnt, clo)
            hi = jnp.where(dn, cand, hi)
            chi = jnp.where(dn, cnt, chi)
            mid2 = 0.5 * lo + 0.5 * hi
            stuck = jnp.logical_not((mid2 > lo) & (mid2 < hi))
            done_f = jnp.where(done | (clo == kf) | stuck, 1.0, 0.0)
            return it + 1, lo, hi, clo, chi, done_f, n_up, n_dn, w_lo, w_hi, last

        st = lax.while_loop(cond, step, (jnp.int32(0), lo, hi, clo, chi, done_f,
                                         zero, zero, one, one, zero))
        lo, hi, clo, chi = st[1], st[2], st[3], st[4]
        lo = jnp.where(few, F32_LOWEST, lo)

        need = kf - chi
        tied = jnp.logical_and(clo > kf, jnp.logical_not(few))

        @pl.when(jnp.max(tied.astype(F32)) > 0.5)
        def _break_ties():
            def count_eq_le(cut):
                def body(c, part):
                    row0 = pl.multiple_of(c * KC, KC)
                    tile = sc_ref[pl.ds(row0, KC), :]
                    kpos = (row0 + lax.broadcasted_iota(jnp.int32, (KC, QB), 0)).astype(F32)
                    ind = jnp.where((tile == lo) & (kpos <= cut), 1.0, 0.0)
                    return part + jnp.sum(ind.reshape(KC // 8, 8, QB), axis=0)
                part = lax.fori_loop(0, i + 1, body, jnp.zeros((8, QB), F32))
                return jnp.sum(part, axis=0, keepdims=True)

            def bis(_, st):
                a, b = st
                mid = jnp.floor(0.5 * (a + b))
                ok = count_eq_le(mid) <= need
                return jnp.where(ok, mid, a), jnp.where(ok, b, mid)

            a0 = jnp.full((1, QB), -1.0, F32)
            b0 = ((i + 1) * KC).astype(F32) + jnp.zeros((1, QB), F32)
            n_bits = int(np.ceil(np.log2(sc_ref.shape[0] + 2))) + 1
            cut, _ = lax.fori_loop(0, n_bits, bis, (a0, b0))

            def drop(c, _):
                row0 = pl.multiple_of(c * KC, KC)
                tile = sc_ref[pl.ds(row0, KC), :]
                kpos = (row0 + lax.broadcasted_iota(jnp.int32, (KC, QB), 0)).astype(F32)
                kill = tied & (tile == lo) & (kpos > cut)
                sc_ref[pl.ds(row0, KC), :] = jnp.where(kill, -jnp.inf, tile)
                return 0
            lax.fori_loop(0, i + 1, drop, 0)

        thr_ref[...] = jnp.broadcast_to(lo, thr_ref.shape)
        m_ref[...] = jnp.full(m_ref.shape, NEG_BIG, F32)
        l_ref[...] = jnp.zeros(l_ref.shape, F32)
        acc_ref[...] = jnp.zeros(acc_ref.shape, F32)

    row0 = pl.multiple_of(j * KC, KC)
    SL = ATTN_SLAB
    n_slab = KC // SL
    thr = jnp.broadcast_to(thr_ref[0:1, :], (SL, QB))
    for r in range(n_slab):
        rows = slice(r * SL, (r + 1) * SL)
        sc = sc_ref[pl.ds(row0 + r * SL, SL), :]
        mask_buf[rows, :] = jnp.where(sc >= thr, 0.0, NEG_BIG)
    posk = pos_ref[...]
    for g0 in range(0, N_SPARSE_HEADS, ATTN_HEAD_GROUP):
        heads = range(g0, g0 + ATTN_HEAD_GROUP)
        col_max = {}
        for h in heads:
            hs = slice(h * HEAD_DIM, (h + 1) * HEAD_DIM)
            lhs = jnp.concatenate([k_ref[:, hs], posk], axis=1)
            rhs = jnp.concatenate([qT_ref[hs, :], srow_ref[h]], axis=0)
            s = jnp.dot(lhs, rhs, preferred_element_type=F32)
            cm = [jnp.full((SL, QB), NEG_BIG, F32)] * 2
            for r in range(n_slab):
                rows = slice(r * SL, (r + 1) * SL)
                blk = s[rows, :] + mask_buf[rows, :]
                s_buf[h, rows, :] = blk
                cm[r % 2] = jnp.maximum(cm[r % 2], blk)
            col_max[h] = jnp.max(jnp.maximum(cm[0], cm[1]), axis=0, keepdims=True)
        alphas = {}
        for h in heads:
            m_old = m_ref[h]
            m_new = jnp.maximum(m_old, col_max[h])
            alphas[h] = jnp.exp2(m_old - m_new)
            m_ref[h] = m_new
            mb = jnp.concatenate([m_new] * (SL // 8), axis=0)
            cs = [jnp.zeros((SL, QB), F32)] * 2
            for r in range(n_slab):
                rows = slice(r * SL, (r + 1) * SL)
                p = jnp.exp2(s_buf[h, rows, :] - mb)
                cs[r % 2] = cs[r % 2] + p
                p_buf[h, rows, :] = p.astype(BF16)
            l_ref[h] = alphas[h] * l_ref[h] + jnp.sum(cs[0] + cs[1], axis=0, keepdims=True)
        for h in heads:
            hs = slice(h * HEAD_DIM, (h + 1) * HEAD_DIM)
            pv = jnp.dot(vT_ref[hs, :], p_buf[h], preferred_element_type=F32)
            acc_ref[hs, :] = alphas[h][0:1, :] * acc_ref[hs, :] + pv

    @pl.when(j == i)
    def _finish():
        for h in range(N_SPARSE_HEADS):
            hs = slice(h * HEAD_DIM, (h + 1) * HEAD_DIM)
            o = acc_ref[hs, :] / l_ref[h, 0:1, :]
            o_ref[:, hs] = o.T.astype(o_ref.dtype)


N_SLOPE_PIECES = 3
POS_SPLIT = 128


def _alibi_mxu_operands(S, slopes_log2):
    assert S <= POS_SPLIT * 256
    kpos = jnp.arange(S, dtype=jnp.int32)[:, None]
    col = jnp.arange(128, dtype=jnp.int32)[None, :]
    part = jnp.where(col % 2 == 0, kpos // POS_SPLIT, kpos % POS_SPLIT)
    pos = jnp.where(col < 2 * N_SLOPE_PIECES, part, 0).astype(BF16)
    rows = np.zeros((len(slopes_log2), 128), np.float32)
    for h, s in enumerate(slopes_log2):
        rest = np.float64(s)
        for p in range(N_SLOPE_PIECES):
            piece = np.float64(np.asarray(rest, np.float32).astype(jnp.bfloat16).astype(np.float32))
            rows[h, 2 * p] = POS_SPLIT * piece
            rows[h, 2 * p + 1] = piece
            rest = rest - piece
    srow = jnp.broadcast_to(jnp.asarray(rows, F32)[:, :, None],
                            (len(slopes_log2), 128, DSA_QB)).astype(BF16)
    return pos, srow


def _dsa_attention(qT, k, vT, qiT, ki, wT, slopes, *, topk):
    S = k.shape[0]
    QB, KC = DSA_QB, DSA_KC
    assert S % (2 * QB) == 0 and QB == KC
    nq = S // QB
    pos, srow = _alibi_mxu_operands(S, [s * LOG2_E for s in slopes])
    quantile = statistics.NormalDist().inv_cdf
    zq = np.array([quantile(1.0 - (topk - 0.5) / n) if n > topk else 0.0
                   for n in range(1, S + 1)], np.float32)
    zq = np.broadcast_to(zq, (8, S))
    qi_tab = np.concatenate([np.full(i + 1, i, np.int32) for i in range(nq)])
    kj_tab = np.concatenate([np.arange(i + 1, dtype=np.int32) for i in range(nq)])
    n_steps = int(qi_tab.shape[0])
    W = N_SPARSE_HEADS * HEAD_DIM
    WI = N_IDX_HEADS * IDX_HEAD_DIM
    grid_spec = pltpu.PrefetchScalarGridSpec(
        num_scalar_prefetch=2,
        grid=(n_steps,),
        in_specs=[
            pl.BlockSpec((W, QB), lambda t, qi, kj: (0, qi[t])),
            pl.BlockSpec((KC, W), lambda t, qi, kj: (kj[t], 0)),
            pl.BlockSpec((W, KC), lambda t, qi, kj: (0, kj[t])),
            pl.BlockSpec((WI, QB), lambda t, qi, kj: (0, qi[t])),
            pl.BlockSpec((S, IDX_BLOCK), lambda t, qi, kj: (0, 0)),
            pl.BlockSpec((IDX_BLOCK, QB), lambda t, qi, kj: (0, qi[t])),
            pl.BlockSpec((KC, 128), lambda t, qi, kj: (kj[t], 0)),
            pl.BlockSpec(srow.shape, lambda t, qi, kj: (0, 0, 0)),
            pl.BlockSpec((8, QB), lambda t, qi, kj: (0, qi[t])),
        ],
        out_specs=pl.BlockSpec((QB, W), lambda t, qi, kj: (qi[t], 0)),
        scratch_shapes=[
            pltpu.VMEM((S, QB), F32),
            pltpu.VMEM((8, QB), F32),
            pltpu.VMEM((N_SPARSE_HEADS, 8, QB), F32),
            pltpu.VMEM((N_SPARSE_HEADS, 8, QB), F32),
            pltpu.VMEM((W, QB), F32),
            pltpu.VMEM((N_SPARSE_HEADS, KC, QB), F32),
            pltpu.VMEM((N_SPARSE_HEADS, KC, QB), BF16),
            pltpu.VMEM((KC, QB), F32),
        ],
    )
    return pl.pallas_call(
        functools.partial(_dsa_kernel, topk=topk),
        grid_spec=grid_spec,
        out_shape=jax.ShapeDtypeStruct((S, W), BF16),
        compiler_params=_cparams("arbitrary"),
        name="dsa_attention",
    )(jnp.asarray(qi_tab), jnp.asarray(kj_tab), qT, k, vT, qiT, ki, wT, pos, srow,
      jnp.asarray(zq))


def _dil_kernel(slope_ref, q_ref, kc_ref, kp_ref, vc_ref, vp_ref, o_ref, lse_ref,
                *, span, dilation):
    m_blk = pl.program_id(1)
    Q = BLOCK
    qi = lax.broadcasted_iota(jnp.int32, (Q, 2 * Q), 0)
    kj = lax.broadcasted_iota(jnp.int32, (Q, 2 * Q), 1)
    step = Q + qi - kj
    mask = (step >= 0) & (step <= span)
    mask = mask & ((kj >= Q) | (m_blk > 0))
    dist = (step * dilation).astype(F32)
    scale = HEAD_DIM ** -0.5
    for h in range(HEADS_PER_DIL_GROUP):
        hs = slice(h * HEAD_DIM, (h + 1) * HEAD_DIM)
        q = q_ref[:, hs]
        kk = jnp.concatenate([kp_ref[:, hs], kc_ref[:, hs]], axis=0)
        vv = jnp.concatenate([vp_ref[:, hs], vc_ref[:, hs]], axis=0)
        s = lax.dot_general(q, kk, (((1,), (1,)), ((), ())),
                            preferred_element_type=F32) * scale
        s = s - slope_ref[h] * dist
        s = jnp.where(mask, s, NEG_BIG)
        mx = jnp.max(s, axis=-1, keepdims=True)
        p = jnp.exp(s - mx)
        l = jnp.sum(p, axis=-1, keepdims=True)
        o = jnp.dot(p.astype(BF16), vv, preferred_element_type=F32) / l
        o_ref[:, hs] = o
        lse_ref[:, hs] = jnp.broadcast_to(mx + jnp.log(l), (Q, HEAD_DIM))


def _dilated_group(qkv, window, dilation, slopes_g):
    _, d, n_sub, GW = qkv.shape
    assert d == dilation
    span = window // d
    assert span <= BLOCK and n_sub % BLOCK == 0
    nb = n_sub // BLOCK

    def cur(which):
        return pl.BlockSpec((None, None, BLOCK, GW), lambda r, m: (which, r, m, 0))

    def prev(which):
        return pl.BlockSpec((None, None, BLOCK, GW),
                            lambda r, m: (which, r, jnp.maximum(m - 1, 0), 0))

    out_spec = pl.BlockSpec((None, BLOCK, GW), lambda r, m: (r, m, 0))
    return pl.pallas_call(
        functools.partial(_dil_kernel, span=span, dilation=d),
        grid=(d, nb),
        in_specs=[pl.BlockSpec(memory_space=pltpu.SMEM),
                  cur(0), cur(1), prev(1), cur(2), prev(2)],
        out_specs=[out_spec, out_spec],
        out_shape=[jax.ShapeDtypeStruct((d, n_sub, GW), F32)] * 2,
        compiler_params=_cparams("parallel", "arbitrary"),
        name=f"dilated_attention_d{d}",
    )(slopes_g, qkv, qkv, qkv, qkv, qkv)


def _layer_norm(y, g, b):
    mu = jnp.mean(y, axis=-1, keepdims=True)
    yc = y - mu
    var = jnp.mean(yc * yc, axis=-1, keepdims=True)
    return yc * lax.rsqrt(var + LN_EPS) * g + b


def _merge_kernel(oa_ref, o1_ref, o2_ref, o3_ref, l1_ref, l2_ref, l3_ref,
                  ga_ref, gb_ref, x_ref, wa_ref, wb_ref, wo_ref, g_ref, b_ref,
                  h_ref, hb_ref, nat_ref):
    tm = x_ref.shape[0]
    n_slab = DIL_OUT_WIDTH // LANES

    def natural(ref, slot):
        d = ref.shape[0]
        if d == 1:
            return [ref[0, :, c * LANES:(c + 1) * LANES] for c in range(n_slab)]
        for r in range(d):
            for c in range(n_slab):
                nat_ref[slot, c, pl.ds(r, tm // d, stride=d), :] = ref[r, :, c * LANES:(c + 1) * LANES]
        return [nat_ref[slot, c] for c in range(n_slab)]

    o1, l1 = natural(o1_ref, 0), natural(l1_ref, 1)
    o2, l2 = natural(o2_ref, 2), natural(l2_ref, 3)
    o3, l3 = natural(o3_ref, 4), natural(l3_ref, 5)
    slabs = []
    for c in range(n_slab):
        lm = jnp.maximum(jnp.maximum(l1[c], l2[c]), l3[c])
        e1, e2, e3 = jnp.exp(l1[c] - lm), jnp.exp(l2[c] - lm), jnp.exp(l3[c] - lm)
        den = e1 + e2 + e3
        slabs.append((e1 / den) * o1[c] + (e2 / den) * o2[c] + (e3 / den) * o3[c])
    ob = jnp.concatenate(slabs, axis=1)
    ya = jnp.dot(oa_ref[...], wa_ref[...], preferred_element_type=F32)
    yb = jnp.dot(ob.astype(BF16), wb_ref[...], preferred_element_type=F32)
    merged = ga_ref[...] * ya + gb_ref[...] * yb
    mix = jnp.dot(merged.astype(BF16), wo_ref[...], preferred_element_type=F32)
    h = _layer_norm(DEEPNORM_ALPHA * x_ref[...] + mix, g_ref[...], b_ref[...])
    h_ref[...] = h
    hb_ref[...] = h.astype(BF16)


MERGE_TM = 256


def _merge(oa, dil, gates, x2, wa, wb, wo, ln_g, ln_b):
    S = x2.shape[0]
    tm = MERGE_TM
    (o1, l1), (o2, l2), (o3, l3) = dil
    GW = DIL_OUT_WIDTH

    def rows(w, col=0):
        return pl.BlockSpec((tm, w), lambda i: (i, col))

    def residue_major(a):
        d = a.shape[0]
        assert tm % (8 * d) == 0
        return pl.BlockSpec((d, tm // d, GW), lambda i: (0, i, 0))

    def whole(a):
        return pl.BlockSpec(a.shape, lambda i: (0,) * a.ndim, pipeline_mode=pl.Buffered(1))

    return pl.pallas_call(
        _merge_kernel,
        grid=(S // tm,),
        in_specs=[rows(SPARSE_WIDTH),
                  residue_major(o1), residue_major(o2), residue_major(o3),
                  residue_major(l1), residue_major(l2), residue_major(l3),
                  rows(D_MODEL, 0), rows(D_MODEL, 1), rows(D_MODEL),
                  whole(wa), whole(wb), whole(wo), whole(ln_g), whole(ln_b)],
        out_specs=[rows(D_MODEL), rows(D_MODEL)],
        out_shape=[jax.ShapeDtypeStruct((S, D_MODEL), F32),
                   jax.ShapeDtypeStruct((S, D_MODEL), BF16)],
        scratch_shapes=[pltpu.VMEM((6, GW // LANES, tm, LANES), F32)],
        compiler_params=_cparams("parallel"),
        name="merge_outproj_ln",
    )(oa, o1, o2, o3, l1, l2, l3, gates, gates, x2, wa, wb, wo, ln_g, ln_b)


FFN_TM = 1024
FFN_UP_TN = 512
FFN_DOWN_TN = 256


def _ffn_up_kernel(hb_ref, wg_ref, wu_ref, a_ref):
    hb = hb_ref[...]
    gate = jnp.dot(hb, wg_ref[...].astype(BF16), preferred_element_type=F32)
    up = jnp.dot(hb, wu_ref[...].astype(BF16), preferred_element_type=F32)
    a_ref[...] = (gate * jax.nn.sigmoid(gate) * up).astype(a_ref.dtype)


def _ffn_down_kernel(a_ref, h_ref, wd_ref, g_ref, b_ref, o_ref):
    j = pl.program_id(1)
    tn = wd_ref.shape[1]
    cols = pl.ds(pl.multiple_of(j * tn, tn), tn)
    o_ref[:, cols] = jnp.dot(a_ref[...], wd_ref[...].astype(BF16), preferred_element_type=F32)

    @pl.when(j == pl.num_programs(1) - 1)
    def _():
        o_ref[...] = _layer_norm(DEEPNORM_ALPHA * h_ref[...] + o_ref[...],
                                 g_ref[...], b_ref[...])


def _ffn(hb, h, wg, wu, wd, ln_g, ln_b):
    S = h.shape[0]
    F = wg.shape[1]
    tm = min(FFN_TM, S)
    assert S % tm == 0 and F % FFN_UP_TN == 0 and D_MODEL % FFN_DOWN_TN == 0
    act = pl.pallas_call(
        _ffn_up_kernel,
        grid=(S // tm, F // FFN_UP_TN),
        in_specs=[pl.BlockSpec((tm, D_MODEL), lambda i, j: (i, 0)),
                  pl.BlockSpec((D_MODEL, FFN_UP_TN), lambda i, j: (0, j)),
                  pl.BlockSpec((D_MODEL, FFN_UP_TN), lambda i, j: (0, j))],
        out_specs=pl.BlockSpec((tm, FFN_UP_TN), lambda i, j: (i, j)),
        out_shape=jax.ShapeDtypeStruct((S, F), BF16),
        compiler_params=_cparams("parallel", "arbitrary"),
        name="ffn_gate_up",
    )(hb, wg, wu)

    def per_row_tile(width):
        return pl.BlockSpec((tm, width), lambda i, j: (i, 0), pipeline_mode=pl.Buffered(1))

    return pl.pallas_call(
        _ffn_down_kernel,
        grid=(S // tm, D_MODEL // FFN_DOWN_TN),
        in_specs=[per_row_tile(F), per_row_tile(D_MODEL),
                  pl.BlockSpec((F, FFN_DOWN_TN), lambda i, j: (0, j)),
                  pl.BlockSpec((1, D_MODEL), lambda i, j: (0, 0)),
                  pl.BlockSpec((1, D_MODEL), lambda i, j: (0, 0))],
        out_specs=pl.BlockSpec((tm, D_MODEL), lambda i, j: (i, 0)),
        out_shape=jax.ShapeDtypeStruct((S, D_MODEL), F32),
        compiler_params=_cparams("parallel", "arbitrary"),
        name="ffn_down_ln",
    )(act, h, wd, ln_g, ln_b)


def _mixer_inputs(xb, w_in):
    O = IN_OFFS
    wT = jnp.transpose(w_in)
    qT = _project(xb, wT, O[0], SPARSE_WIDTH, out_dtype=BF16,
                  scale=LOG2_E * HEAD_DIM ** -0.5, transpose_out=True, name="proj_aqT")
    k = _project(xb, wT, O[1], SPARSE_WIDTH, out_dtype=BF16, name="proj_ak")
    vT = _project(xb, wT, O[2], SPARSE_WIDTH, out_dtype=BF16, transpose_out=True,
                  name="proj_avT")
    qiT = _project(xb, wT, O[3], N_IDX_HEADS * IDX_HEAD_DIM, out_dtype=BF16,
                   scale=IDX_HEAD_DIM ** -0.5, transpose_out=True, name="proj_iqT")
    ki, iwT = _project_indexer_kw(xb, wT, O[4], w_scale=N_IDX_HEADS ** -0.5)
    dil_qkv = [_project_dilated_qkv(xb, wT, O[6], g, d) for g, (_, d) in enumerate(DIL_GROUPS)]
    gates = _project(xb, wT, O[9], 2 * D_MODEL, out_dtype=F32, sigmoid=True, name="proj_gates")
    return qT, k, vT, qiT, ki, iwT, dil_qkv, gates


def _layer(x2, w_in, w_a, w_b, w_out, ln1_g, ln1_b, w_gate, w_up, w_down, ln2_g, ln2_b):
    S = x2.shape[0]
    slopes = _alibi_slopes()
    xb = x2.astype(BF16)
    qT, k, vT, qiT, ki, iwT, dil_qkv, gates = _mixer_inputs(xb, w_in)

    topk = min(TOPK_MAX, S // 4)
    oa = _dsa_attention(qT, k, vT, qiT, ki, iwT, slopes[N_DIL_HEADS:], topk=topk)

    dil = []
    for g, (win, d) in enumerate(DIL_GROUPS):
        sl = jnp.asarray(slopes[g * HEADS_PER_DIL_GROUP:(g + 1) * HEADS_PER_DIL_GROUP], F32)
        dil.append(_dilated_group(dil_qkv[g], win, d, sl))

    h, hb = _merge(oa, dil, gates, x2, w_a.astype(BF16), w_b.astype(BF16), w_out.astype(BF16),
                   ln1_g.reshape(1, -1), ln1_b.reshape(1, -1))
    return _ffn(hb, h, w_gate, w_up, w_down, ln2_g.reshape(1, -1), ln2_b.reshape(1, -1))


def kernel(x, w_in, w_a, w_b, w_out, ln1_g, ln1_b, w_gate, w_up, w_down, ln2_g, ln2_b):
    B, S, D = x.shape
    outs = []
    for b in range(B):
        y = x[b]
        for l in range(w_in.shape[0]):
            y = _layer(y, w_in[l], w_a[l], w_b[l], w_out[l], ln1_g[l], ln1_b[l],
                       w_gate[l], w_up[l], w_down[l], ln2_g[l], ln2_b[l])
        outs.append(y)
    return jnp.stack(outs)
```

```python
import functools
import statistics

import numpy as np
import jax
import jax.numpy as jnp
from jax import lax
from jax.experimental import pallas as pl
from jax.experimental.pallas import tpu as pltpu

F32 = jnp.float32
BF16 = jnp.bfloat16

D_MODEL = 2048
HEAD_DIM = 128
N_SPARSE_HEADS = 8
SPARSE_WIDTH = N_SPARSE_HEADS * HEAD_DIM
N_IDX_HEADS = 16
IDX_HEAD_DIM = 64
TOPK_MAX = 256
DIL_GROUPS = ((128, 1), (512, 4), (2048, 16))
HEADS_PER_DIL_GROUP = 4
N_DIL_HEADS = HEADS_PER_DIL_GROUP * len(DIL_GROUPS)
DIL_WIDTH = N_DIL_HEADS * HEAD_DIM
DIL_OUT_WIDTH = HEADS_PER_DIL_GROUP * HEAD_DIM
N_ALIBI_HEADS = N_DIL_HEADS + N_SPARSE_HEADS
BLOCK = 128
D_FF = ((-(-8 * D_MODEL // 3)) + 255) // 256 * 256
DEPTH = 1
DEEPNORM_ALPHA = (2 * DEPTH) ** 0.25
LN_EPS = 1e-5
IN_SIZES = (SPARSE_WIDTH, SPARSE_WIDTH, SPARSE_WIDTH,
            N_IDX_HEADS * IDX_HEAD_DIM, IDX_HEAD_DIM, N_IDX_HEADS,
            DIL_WIDTH, DIL_WIDTH, DIL_WIDTH, D_MODEL, D_MODEL)
IN_OFFS = tuple(int(c) for c in np.cumsum((0,) + IN_SIZES))

VMEM_LIMIT_BYTES = 56 * 1024 * 1024
LOG2_E = float(np.log2(np.e))
NEG_BIG = -1e30
F32_LOWEST = float(np.finfo(np.float32).min)
F32_HIGHEST = float(np.finfo(np.float32).max)


def _alibi_slopes():
    return [2.0 ** (-8.0 * (i + 1) / N_ALIBI_HEADS) for i in range(N_ALIBI_HEADS)]


def _cparams(*sem):
    return pltpu.CompilerParams(dimension_semantics=sem,
                                vmem_limit_bytes=VMEM_LIMIT_BYTES)


LANES = 128
ROW_ALIGN = 16
PROJ_TM = 1024
PROJ_TN = 1024
IDX_BLOCK = 128


def _x_dot_wT(x_ref, w_ref):
    return lax.dot_general(x_ref[...], w_ref[...].astype(BF16), (((1,), (1,)), ((), ())),
                           preferred_element_type=F32)


def _w_rows_spec(tn, K, row_of_step):
    return pl.BlockSpec((pl.Element(tn), pl.Element(K)),
                        lambda i, j: (pl.multiple_of(row_of_step(j), ROW_ALIGN), 0))


def _proj_kernel(x_ref, w_ref, o_ref, *, scale, sigmoid, transpose_out):
    acc = _x_dot_wT(x_ref, w_ref)
    if scale != 1.0:
        acc = acc * scale
    if sigmoid:
        acc = jax.nn.sigmoid(acc)
    if transpose_out:
        acc = acc.T
    o_ref[...] = acc.astype(o_ref.dtype)


def _project(xb, wT, row0, ncols, *, out_dtype, scale=1.0, sigmoid=False,
             transpose_out=False, name):
    S, K = xb.shape
    tm, tn = min(PROJ_TM, S), PROJ_TN
    assert row0 % ROW_ALIGN == 0 and ncols % tn == 0 and S % tm == 0
    if transpose_out:
        out_shape = jax.ShapeDtypeStruct((ncols, S), out_dtype)
        out_spec = pl.BlockSpec((tn, tm), lambda i, j: (j, i))
    else:
        out_shape = jax.ShapeDtypeStruct((S, ncols), out_dtype)
        out_spec = pl.BlockSpec((tm, tn), lambda i, j: (i, j))
    return pl.pallas_call(
        functools.partial(_proj_kernel, scale=scale, sigmoid=sigmoid,
                          transpose_out=transpose_out),
        grid=(S // tm, ncols // tn),
        in_specs=[pl.BlockSpec((tm, K), lambda i, j: (i, 0)),
                  _w_rows_spec(tn, K, lambda j: row0 + j * tn)],
        out_specs=out_spec,
        out_shape=out_shape,
        compiler_params=_cparams("parallel", "arbitrary"),
        name=name,
    )(xb, wT)


def _proj_idx_kernel(x_ref, w_ref, k_ref, wT_ref, *, w_scale):
    acc = _x_dot_wT(x_ref, w_ref)
    k_ref[...] = acc.astype(k_ref.dtype)
    wT_ref[...] = (acc * w_scale).T


def _project_indexer_kw(xb, wT, row0, *, w_scale):
    S, K = xb.shape
    tm = min(PROJ_TM, S)
    return pl.pallas_call(
        functools.partial(_proj_idx_kernel, w_scale=w_scale),
        grid=(S // tm, 1),
        in_specs=[pl.BlockSpec((tm, K), lambda i, j: (i, 0)),
                  _w_rows_spec(IDX_BLOCK, K, lambda j: row0)],
        out_specs=[pl.BlockSpec((tm, IDX_BLOCK), lambda i, j: (i, 0)),
                   pl.BlockSpec((IDX_BLOCK, tm), lambda i, j: (0, i))],
        out_shape=[jax.ShapeDtypeStruct((S, IDX_BLOCK), BF16),
                   jax.ShapeDtypeStruct((IDX_BLOCK, S), F32)],
        compiler_params=_cparams("parallel", "arbitrary"),
        name="proj_ik_iw",
    )(xb, wT)


def _proj_dilated_kernel(x_ref, wq_ref, wk_ref, wv_ref, o_ref, scr_ref, *, dilation):
    for which, w_ref in enumerate((wq_ref, wk_ref, wv_ref)):
        acc = _x_dot_wT(x_ref, w_ref)
        if which == 0:
            acc = acc * (LOG2_E * HEAD_DIM ** -0.5)
        tm, tn = acc.shape
        if dilation == 1:
            o_ref[which, 0] = acc.astype(o_ref.dtype)
            continue
        for c in range(tn // LANES):
            scr_ref[which, c] = acc[:, c * LANES:(c + 1) * LANES]
        for r in range(dilation):
            for c in range(tn // LANES):
                rows = scr_ref[which, c, pl.ds(r, tm // dilation, stride=dilation), :]
                o_ref[which, r, :, c * LANES:(c + 1) * LANES] = rows.astype(o_ref.dtype)


def _project_dilated_qkv(xb, wT, row0, g, dilation):
    S, K = xb.shape
    d = dilation
    tm, tn = min(PROJ_TM, S), DIL_OUT_WIDTH
    assert S % tm == 0 and tm % (d * ROW_ALIGN) == 0

    def w_spec(which):
        return _w_rows_spec(tn, K, lambda j: row0 + which * DIL_WIDTH + g * tn)

    return pl.pallas_call(
        functools.partial(_proj_dilated_kernel, dilation=d),
        grid=(S // tm, 1),
        in_specs=[pl.BlockSpec((tm, K), lambda i, j: (i, 0)), w_spec(0), w_spec(1), w_spec(2)],
        out_specs=pl.BlockSpec((3, d, tm // d, tn), lambda i, j: (0, 0, i, 0)),
        out_shape=jax.ShapeDtypeStruct((3, d, S // d, tn), BF16),
        scratch_shapes=[pltpu.VMEM((3, tn // LANES, tm, LANES), F32)],
        compiler_params=_cparams("parallel", "arbitrary"),
        name=f"proj_dil_qkv_d{d}",
    )(xb, wT, wT, wT)


DSA_QB = 256
DSA_KC = 256
SEARCH_MAX_ITERS = 512
SEARCH_BISECT_EVERY = 4
SEARCH_FIRST_STEP = 0.0625
ATTN_SLAB = 16
ATTN_HEAD_GROUP = 8
IDX_PER_ATTN_HEAD = N_IDX_HEADS // N_SPARSE_HEADS
IDX_IN_SOFTMAX = 1
COUNT_SLAB = 16
COUNT_ACCUMULATORS = 2


def _dsa_kernel(qi_tab, kj_tab,
                qT_ref, k_ref, vT_ref, qiT_ref, qiT_next_ref, ki_ref, wT_ref, wT_next_ref,
                pos_ref, srow_ref, zq_ref,
                o_ref,
                sc2_ref, thr_ref, m_ref, l_ref, acc_ref, s_buf, p_buf, mask_buf, mom_ref, *, topk):
    QB, KC = DSA_QB, DSA_KC
    t = pl.program_id(0)
    i = qi_tab[t]
    j = kj_tab[t]
    cur = i % 2
    sc_ref = sc2_ref.at[cur]

    @pl.when(t == 0)
    def _first():
        mom_ref[...] = jnp.zeros(mom_ref.shape, F32)

    def score_heads(acc, kic, q_ref, w_ref, heads):
        for h in heads:
            r = jnp.dot(kic, q_ref[h * IDX_HEAD_DIM:(h + 1) * IDX_HEAD_DIM, :],
                        preferred_element_type=F32)
            w_h = w_ref[IDX_HEAD_DIM + h:IDX_HEAD_DIM + h + 1, :]
            acc = acc + w_h * jnp.maximum(r, 0.0)
        return acc

    def chunk_keys(c):
        row0 = pl.multiple_of(c * KC, KC)
        return ki_ref[pl.ds(row0, KC), 0:IDX_HEAD_DIM]

    def chunk_scores(q_ref, w_ref, dst_ref, c, diagonal):
        acc = score_heads(jnp.zeros((KC, QB), F32), chunk_keys(c), q_ref, w_ref,
                          range(N_IDX_HEADS))
        return finish_scores(acc, dst_ref, c, diagonal)

    def finish_scores(acc, dst_ref, c, diagonal):
        row0 = pl.multiple_of(c * KC, KC)
        if diagonal:
            kpos = lax.broadcasted_iota(jnp.int32, (KC, QB), 0)
            qpos = lax.broadcasted_iota(jnp.int32, (KC, QB), 1)
            causal = kpos <= qpos
            live = jnp.where(causal, acc, 0.0)
            acc = jnp.where(causal, acc, -jnp.inf)
        else:
            live = acc
        dst_ref[pl.ds(row0, KC), :] = acc
        s1 = jnp.sum(live.reshape(KC // 8, 8, QB), axis=0)
        s2 = jnp.sum((live * live).reshape(KC // 8, 8, QB), axis=0)
        return s1, s2

    def count_ge(cand):
        candb = jnp.broadcast_to(cand, (COUNT_SLAB, QB))

        def body(c, parts):
            row0 = pl.multiple_of(c * (2 * KC), 2 * KC)
            parts = list(parts)
            for r in range(2 * KC // COUNT_SLAB):
                slab = sc_ref[pl.ds(row0 + r * COUNT_SLAB, COUNT_SLAB), :]
                parts[r % len(parts)] = parts[r % len(parts)] + jnp.where(slab >= candb, 1.0, 0.0)
            return tuple(parts)
        zero = jnp.zeros((COUNT_SLAB, QB), F32)
        parts = lax.fori_loop(0, (i + 2) // 2, body, (zero,) * COUNT_ACCUMULATORS)
        return jnp.sum(sum(parts), axis=0, keepdims=True)

    @pl.when(j == 0)
    def _prepare():
        d1, d2 = chunk_scores(qiT_ref, wT_ref, sc_ref, i, True)
        s1 = jnp.sum(mom_ref[cur, 0] + d1, axis=0, keepdims=True)
        s2 = jnp.sum(mom_ref[cur, 1] + d2, axis=0, keepdims=True)
        mom_ref[1 - cur] = jnp.zeros(mom_ref.shape[1:], F32)

        @pl.when(i % 2 == 0)
        def _pad():
            sc_ref[pl.ds(pl.multiple_of((i + 1) * KC, KC), KC), :] = jnp.full((KC, QB), -jnp.inf, F32)

        kf = float(topk)
        log_k = float(np.log(topk - 0.5))
        n_causal = (i * QB + 1 + lax.broadcasted_iota(jnp.int32, (1, QB), 1)).astype(F32)
        mean = s1 / n_causal
        std = jnp.sqrt(jnp.maximum(s2 / n_causal - mean * mean, 0.0))
        zq = jnp.max(zq_ref[...], axis=0, keepdims=True)
        guess = mean + zq * std
        few = n_causal <= kf
        lo = jnp.full((1, QB), F32_LOWEST, F32)
        clo = n_causal
        hi = jnp.full((1, QB), F32_HIGHEST, F32)
        chi = jnp.zeros((1, QB), F32)
        done_f = jnp.where(few | (clo == kf), 1.0, 0.0)
        zero = jnp.zeros((1, QB), F32)
        one = jnp.ones((1, QB), F32)

        def cond(st):
            return jnp.logical_and(st[0] < SEARCH_MAX_ITERS, jnp.min(st[5]) < 0.5)

        def step(st):
            it, lo, hi, clo, chi, done_f, n_up, n_dn, w_lo, w_hi, last = st
            done = done_f > 0.5
            mid = 0.5 * lo + 0.5 * hi
            f_lo = (jnp.log(clo) - log_k) * w_lo
            f_hi = (log_k - jnp.log(jnp.maximum(chi, 0.5))) * w_hi
            interp = lo + (hi - lo) * (f_lo / jnp.maximum(f_lo + f_hi, 1e-9))
            open_hi = hi >= F32_HIGHEST
            open_lo = lo <= F32_LOWEST
            cand = jnp.where(it % SEARCH_BISECT_EVERY == SEARCH_BISECT_EVERY - 1, mid, interp)
            cand = jnp.where(open_hi, lo + std * (SEARCH_FIRST_STEP * jnp.exp2(n_up)), cand)
            cand = jnp.where(open_lo, hi - std * (SEARCH_FIRST_STEP * jnp.exp2(n_dn)), cand)
            cand = jnp.where(open_lo & open_hi, guess, cand)
            bad = jnp.logical_not((cand > lo) & (cand < hi))
            cand = jnp.where(bad, mid, cand)
            cand = jnp.where(done, lo, cand)
            cnt = count_ge(cand)
            up = jnp.logical_not(done) & (cnt >= kf)
            dn = jnp.logical_not(done) & (cnt < kf)
            n_up = n_up + jnp.where(up & open_hi, 1.0, 0.0)
            n_dn = n_dn + jnp.where(dn & open_lo, 1.0, 0.0)
            w_hi = jnp.where(up, jnp.where(last == 1.0, 0.5 * w_hi, 1.0), jnp.where(dn, 1.0, w_hi))
            w_lo = jnp.where(dn, jnp.where(last == 2.0, 0.5 * w_lo, 1.0), jnp.where(up, 1.0, w_lo))
            last = jnp.where(up, 1.0, jnp.where(dn, 2.0, last))
            lo = jnp.where(up, cand, lo)
            clo = jnp.where(up, cnt, clo)
            hi = jnp.where(dn, cand, hi)
            chi = jnp.where(dn, cnt, chi)
            mid2 = 0.5 * lo + 0.5 * hi
            stuck = jnp.logical_not((mid2 > lo) & (mid2 < hi))
            done_f = jnp.where(done | (clo == kf) | stuck, 1.0, 0.0)
            return it + 1, lo, hi, clo, chi, done_f, n_up, n_dn, w_lo, w_hi, last

        st = lax.while_loop(cond, step, (jnp.int32(0), lo, hi, clo, chi, done_f,
                                         zero, zero, one, one, zero))
        lo, hi, clo, chi = st[1], st[2], st[3], st[4]
        lo = jnp.where(few, F32_LOWEST, lo)

        need = kf - chi
        tied = jnp.logical_and(clo > kf, jnp.logical_not(few))

        @pl.when(jnp.max(tied.astype(F32)) > 0.5)
        def _break_ties():
            def count_eq_le(cut):
                def body(c, part):
                    row0 = pl.multiple_of(c * KC, KC)
                    tile = sc_ref[pl.ds(row0, KC), :]
                    kpos = (row0 + lax.broadcasted_iota(jnp.int32, (KC, QB), 0)).astype(F32)
                    ind = jnp.where((tile == lo) & (kpos <= cut), 1.0, 0.0)
                    return part + jnp.sum(ind.reshape(KC // 8, 8, QB), axis=0)
                part = lax.fori_loop(0, i + 1, body, jnp.zeros((8, QB), F32))
                return jnp.sum(part, axis=0, keepdims=True)

            def bis(_, st):
                a, b = st
                mid = jnp.floor(0.5 * (a + b))
                ok = count_eq_le(mid) <= need
                return jnp.where(ok, mid, a), jnp.where(ok, b, mid)

            a0 = jnp.full((1, QB), -1.0, F32)
            b0 = ((i + 1) * KC).astype(F32) + jnp.zeros((1, QB), F32)
            n_bits = int(np.ceil(np.log2(sc_ref.shape[0] + 2))) + 1
            cut, _ = lax.fori_loop(0, n_bits, bis, (a0, b0))

            def drop(c, _):
                row0 = pl.multiple_of(c * KC, KC)
                tile = sc_ref[pl.ds(row0, KC), :]
                kpos = (row0 + lax.broadcasted_iota(jnp.int32, (KC, QB), 0)).astype(F32)
                kill = tied & (tile == lo) & (kpos > cut)
                sc_ref[pl.ds(row0, KC), :] = jnp.where(kill, -jnp.inf, tile)
                return 0
            lax.fori_loop(0, i + 1, drop, 0)

        thr_ref[...] = jnp.broadcast_to(lo, thr_ref.shape)
        m_ref[...] = jnp.full(m_ref.shape, NEG_BIG, F32)
        l_ref[...] = jnp.zeros(l_ref.shape, F32)
        acc_ref[...] = jnp.zeros(acc_ref.shape, F32)

    row0 = pl.multiple_of(j * KC, KC)
    SL = ATTN_SLAB
    n_slab = KC // SL
    thr = jnp.broadcast_to(thr_ref[0:1, :], (SL, QB))
    for r in range(n_slab):
        rows = slice(r * SL, (r + 1) * SL)
        sc = sc_ref[pl.ds(row0 + r * SL, SL), :]
        mask_buf[rows, :] = jnp.where(sc >= thr, 0.0, NEG_BIG)
    posk = pos_ref[...]
    nxt_keys = chunk_keys(j)
    nxt_acc = jnp.zeros((KC, QB), F32)
    for g0 in range(0, N_SPARSE_HEADS, ATTN_HEAD_GROUP):
        heads = range(g0, g0 + ATTN_HEAD_GROUP)
        col_max = {}
        for h in heads:
            hs = slice(h * HEAD_DIM, (h + 1) * HEAD_DIM)
            lhs = jnp.concatenate([k_ref[:, hs], posk], axis=1)
            rhs = jnp.concatenate([qT_ref[hs, :], srow_ref[h]], axis=0)
            s = jnp.dot(lhs, rhs, preferred_element_type=F32)
            cm = [jnp.full((SL, QB), NEG_BIG, F32)] * 2
            for r in range(n_slab):
                rows = slice(r * SL, (r + 1) * SL)
                blk = s[rows, :] + mask_buf[rows, :]
                s_buf[h, rows, :] = blk
                cm[r % 2] = jnp.maximum(cm[r % 2], blk)
            col_max[h] = jnp.max(jnp.maximum(cm[0], cm[1]), axis=0, keepdims=True)
        alphas = {}
        for h in heads:
            nxt_acc = score_heads(nxt_acc, nxt_keys, qiT_next_ref, wT_next_ref,
                                  range(h * IDX_PER_ATTN_HEAD, h * IDX_PER_ATTN_HEAD + IDX_IN_SOFTMAX))
            m_old = m_ref[h]
            m_new = jnp.maximum(m_old, col_max[h])
            alphas[h] = jnp.exp2(m_old - m_new)
            m_ref[h] = m_new
            mb = jnp.concatenate([m_new] * (SL // 8), axis=0)
            cs = [jnp.zeros((SL, QB), F32)] * 2
            for r in range(n_slab):
                rows = slice(r * SL, (r + 1) * SL)
                p = jnp.exp2(s_buf[h, rows, :] - mb)
                cs[r % 2] = cs[r % 2] + p
                p_buf[h, rows, :] = p.astype(BF16)
            l_ref[h] = alphas[h] * l_ref[h] + jnp.sum(cs[0] + cs[1], axis=0, keepdims=True)
        for h in heads:
            nxt_acc = score_heads(nxt_acc, nxt_keys, qiT_next_ref, wT_next_ref,
                                  range(h * IDX_PER_ATTN_HEAD + IDX_IN_SOFTMAX,
                                        (h + 1) * IDX_PER_ATTN_HEAD))
            hs = slice(h * HEAD_DIM, (h + 1) * HEAD_DIM)
            pv = jnp.dot(vT_ref[hs, :], p_buf[h], preferred_element_type=F32)
            acc_ref[hs, :] = alphas[h][0:1, :] * acc_ref[hs, :] + pv

    n1, n2 = finish_scores(nxt_acc, sc2_ref.at[1 - cur], j, False)
    mom_ref[1 - cur, 0] += n1
    mom_ref[1 - cur, 1] += n2

    @pl.when(j == i)
    def _finish():
        for h in range(N_SPARSE_HEADS):
            hs = slice(h * HEAD_DIM, (h + 1) * HEAD_DIM)
            o = acc_ref[hs, :] / l_ref[h, 0:1, :]
            o_ref[:, hs] = o.T.astype(o_ref.dtype)


N_SLOPE_PIECES = 3
POS_SPLIT = 128


def _alibi_mxu_operands(S, slopes_log2):
    assert S <= POS_SPLIT * 256
    kpos = jnp.arange(S, dtype=jnp.int32)[:, None]
    col = jnp.arange(128, dtype=jnp.int32)[None, :]
    part = jnp.where(col % 2 == 0, kpos // POS_SPLIT, kpos % POS_SPLIT)
    pos = jnp.where(col < 2 * N_SLOPE_PIECES, part, 0).astype(BF16)
    rows = np.zeros((len(slopes_log2), 128), np.float32)
    for h, s in enumerate(slopes_log2):
        rest = np.float64(s)
        for p in range(N_SLOPE_PIECES):
            piece = np.float64(np.asarray(rest, np.float32).astype(jnp.bfloat16).astype(np.float32))
            rows[h, 2 * p] = POS_SPLIT * piece
            rows[h, 2 * p + 1] = piece
            rest = rest - piece
    srow = jnp.broadcast_to(jnp.asarray(rows, F32)[:, :, None],
                            (len(slopes_log2), 128, DSA_QB)).astype(BF16)
    return pos, srow


def _dsa_attention(qT, k, vT, qiT, ki, wT, slopes, *, topk):
    S = k.shape[0]
    QB, KC = DSA_QB, DSA_KC
    assert S % (2 * QB) == 0 and QB == KC
    nq = S // QB
    pos, srow = _alibi_mxu_operands(S, [s * LOG2_E for s in slopes])
    quantile = statistics.NormalDist().inv_cdf
    zq = np.array([quantile(1.0 - (topk - 0.5) / n) if n > topk else 0.0
                   for n in range(1, S + 1)], np.float32)
    zq = np.broadcast_to(zq, (8, S))
    qi_tab = np.concatenate([np.full(i + 1, i, np.int32) for i in range(nq)])
    kj_tab = np.concatenate([np.arange(i + 1, dtype=np.int32) for i in range(nq)])
    n_steps = int(qi_tab.shape[0])
    W = N_SPARSE_HEADS * HEAD_DIM
    WI = N_IDX_HEADS * IDX_HEAD_DIM
    grid_spec = pltpu.PrefetchScalarGridSpec(
        num_scalar_prefetch=2,
        grid=(n_steps,),
        in_specs=[
            pl.BlockSpec((W, QB), lambda t, qi, kj: (0, qi[t])),
            pl.BlockSpec((KC, W), lambda t, qi, kj: (kj[t], 0)),
            pl.BlockSpec((W, KC), lambda t, qi, kj: (0, kj[t])),
            pl.BlockSpec((WI, QB), lambda t, qi, kj: (0, qi[t])),
            pl.BlockSpec((WI, QB), lambda t, qi, kj: (0, jnp.minimum(qi[t] + 1, nq - 1))),
            pl.BlockSpec((S, IDX_BLOCK), lambda t, qi, kj: (0, 0)),
            pl.BlockSpec((IDX_BLOCK, QB), lambda t, qi, kj: (0, qi[t])),
            pl.BlockSpec((IDX_BLOCK, QB), lambda t, qi, kj: (0, jnp.minimum(qi[t] + 1, nq - 1))),
            pl.BlockSpec((KC, 128), lambda t, qi, kj: (kj[t], 0)),
            pl.BlockSpec(srow.shape, lambda t, qi, kj: (0, 0, 0)),
            pl.BlockSpec((8, QB), lambda t, qi, kj: (0, qi[t])),
        ],
        out_specs=pl.BlockSpec((QB, W), lambda t, qi, kj: (qi[t], 0)),
        scratch_shapes=[
            pltpu.VMEM((2, S, QB), F32),
            pltpu.VMEM((8, QB), F32),
            pltpu.VMEM((N_SPARSE_HEADS, 8, QB), F32),
            pltpu.VMEM((N_SPARSE_HEADS, 8, QB), F32),
            pltpu.VMEM((W, QB), F32),
            pltpu.VMEM((N_SPARSE_HEADS, KC, QB), F32),
            pltpu.VMEM((N_SPARSE_HEADS, KC, QB), BF16),
            pltpu.VMEM((KC, QB), F32),
            pltpu.VMEM((2, 2, 8, QB), F32),
        ],
    )
    return pl.pallas_call(
        functools.partial(_dsa_kernel, topk=topk),
        grid_spec=grid_spec,
        out_shape=jax.ShapeDtypeStruct((S, W), BF16),
        compiler_params=_cparams("arbitrary"),
        name="dsa_attention",
    )(jnp.asarray(qi_tab), jnp.asarray(kj_tab), qT, k, vT, qiT, qiT, ki, wT, wT, pos, srow,
      jnp.asarray(zq))


DIL_BLOCKS_PER_STEP = 4
def _dil_kernel(bias_ref, q_ref, kc_ref, kp_ref, vc_ref, vp_ref, o_ref, lse_ref, *, n_blocks):
    first = jnp.where(pl.program_id(1) == 0, 1, 0)
    Q = BLOCK
    for b in range(n_blocks):
        rows = slice(b * Q, (b + 1) * Q)
        prev_rows = slice((b - 1) * Q, b * Q)
        for h in range(HEADS_PER_DIL_GROUP):
            hs = slice(h * HEAD_DIM, (h + 1) * HEAD_DIM)
            q = q_ref[rows, hs]
            k_prev = kp_ref[:, hs] if b == 0 else kc_ref[prev_rows, hs]
            v_prev = vp_ref[:, hs] if b == 0 else vc_ref[prev_rows, hs]
            kk = jnp.concatenate([k_prev, kc_ref[rows, hs]], axis=0)
            vv = jnp.concatenate([v_prev, vc_ref[rows, hs]], axis=0)
            bias = bias_ref[first, h] if b == 0 else bias_ref[0, h]
            s = lax.dot_general(q, kk, (((1,), (1,)), ((), ())),
                                preferred_element_type=F32) + bias
            mx = jnp.max(s, axis=-1, keepdims=True)
            p = jnp.exp2(s - mx)
            l = jnp.sum(p, axis=-1, keepdims=True)
            o = jnp.dot(p.astype(BF16), vv, preferred_element_type=F32) / l
            o_ref[rows, hs] = o
            lse_ref[rows, hs] = jnp.broadcast_to(mx + jnp.log2(l), (Q, HEAD_DIM))


def _dilated_bias(span, dilation, slopes_log2):
    Q = BLOCK
    step = Q + np.arange(Q)[:, None] - np.arange(2 * Q)[None, :]
    band = (step >= 0) & (step <= span)
    has_prev = np.stack([np.ones_like(band), np.broadcast_to(np.arange(2 * Q)[None, :] >= Q, band.shape)])
    dist = (step * dilation).astype(np.float64)
    bias = -np.asarray(slopes_log2, np.float64)[None, :, None, None] * dist[None, None]
    keep = (band[None] & has_prev)[:, None]
    return jnp.asarray(np.where(keep, bias, NEG_BIG), F32)


def _dilated_group(qkv, window, dilation, slopes_g):
    _, d, n_sub, GW = qkv.shape
    assert d == dilation
    span = window // d
    nbs = min(DIL_BLOCKS_PER_STEP, n_sub // BLOCK)
    rows = nbs * BLOCK
    assert span <= BLOCK and n_sub % rows == 0
    bias = _dilated_bias(span, d, [s * LOG2_E for s in slopes_g])

    def cur(which):
        return pl.BlockSpec((None, None, rows, GW), lambda r, m: (which, r, m, 0))

    def prev(which):
        return pl.BlockSpec((None, None, BLOCK, GW),
                            lambda r, m: (which, r, jnp.maximum(m * nbs - 1, 0), 0))

    out_spec = pl.BlockSpec((None, rows, GW), lambda r, m: (r, m, 0))
    return pl.pallas_call(
        functools.partial(_dil_kernel, n_blocks=nbs),
        grid=(d, n_sub // rows),
        in_specs=[pl.BlockSpec(bias.shape, lambda r, m: (0, 0, 0, 0)),
                  cur(0), cur(1), prev(1), cur(2), prev(2)],
        out_specs=[out_spec, out_spec],
        out_shape=[jax.ShapeDtypeStruct((d, n_sub, GW), F32)] * 2,
        compiler_params=_cparams("parallel", "arbitrary"),
        name=f"dilated_attention_d{d}",
    )(bias, qkv, qkv, qkv, qkv, qkv)


def _layer_norm(y, g, b):
    mu = jnp.mean(y, axis=-1, keepdims=True)
    yc = y - mu
    var = jnp.mean(yc * yc, axis=-1, keepdims=True)
    return yc * lax.rsqrt(var + LN_EPS) * g + b


def _merge_kernel(oa_ref, o1_ref, o2_ref, o3_ref, l1_ref, l2_ref, l3_ref,
                  ga_ref, gb_ref, x_ref, wa_ref, wb_ref, wo_ref, g_ref, b_ref,
                  h_ref, hb_ref, nat_ref):
    tm = x_ref.shape[0]
    n_slab = DIL_OUT_WIDTH // LANES

    def natural(ref, slot):
        d = ref.shape[0]
        if d == 1:
            return [ref[0, :, c * LANES:(c + 1) * LANES] for c in range(n_slab)]
        for r in range(d):
            for c in range(n_slab):
                nat_ref[slot, c, pl.ds(r, tm // d, stride=d), :] = ref[r, :, c * LANES:(c + 1) * LANES]
        return [nat_ref[slot, c] for c in range(n_slab)]

    o1, l1 = natural(o1_ref, 0), natural(l1_ref, 1)
    o2, l2 = natural(o2_ref, 2), natural(l2_ref, 3)
    o3, l3 = natural(o3_ref, 4), natural(l3_ref, 5)
    slabs = []
    for c in range(n_slab):
        lm = jnp.maximum(jnp.maximum(l1[c], l2[c]), l3[c])
        e1, e2, e3 = jnp.exp2(l1[c] - lm), jnp.exp2(l2[c] - lm), jnp.exp2(l3[c] - lm)
        den = e1 + e2 + e3
        slabs.append((e1 / den) * o1[c] + (e2 / den) * o2[c] + (e3 / den) * o3[c])
    ob = jnp.concatenate(slabs, axis=1)
    ya = jnp.dot(oa_ref[...], wa_ref[...], preferred_element_type=F32)
    yb = jnp.dot(ob.astype(BF16), wb_ref[...], preferred_element_type=F32)
    merged = ga_ref[...] * ya + gb_ref[...] * yb
    mix = jnp.dot(merged.astype(BF16), wo_ref[...], preferred_element_type=F32)
    h = _layer_norm(DEEPNORM_ALPHA * x_ref[...] + mix, g_ref[...], b_ref[...])
    h_ref[...] = h
    hb_ref[...] = h.astype(BF16)


MERGE_TM = 256


def _merge(oa, dil, gates, x2, wa, wb, wo, ln_g, ln_b):
    S = x2.shape[0]
    tm = MERGE_TM
    (o1, l1), (o2, l2), (o3, l3) = dil
    GW = DIL_OUT_WIDTH

    def rows(w, col=0):
        return pl.BlockSpec((tm, w), lambda i: (i, col))

    def residue_major(a):
        d = a.shape[0]
        assert tm % (8 * d) == 0
        return pl.BlockSpec((d, tm // d, GW), lambda i: (0, i, 0))

    def whole(a):
        return pl.BlockSpec(a.shape, lambda i: (0,) * a.ndim, pipeline_mode=pl.Buffered(1))

    return pl.pallas_call(
        _merge_kernel,
        grid=(S // tm,),
        in_specs=[rows(SPARSE_WIDTH),
                  residue_major(o1), residue_major(o2), residue_major(o3),
                  residue_major(l1), residue_major(l2), residue_major(l3),
                  rows(D_MODEL, 0), rows(D_MODEL, 1), rows(D_MODEL),
                  whole(wa), whole(wb), whole(wo), whole(ln_g), whole(ln_b)],
        out_specs=[rows(D_MODEL), rows(D_MODEL)],
        out_shape=[jax.ShapeDtypeStruct((S, D_MODEL), F32),
                   jax.ShapeDtypeStruct((S, D_MODEL), BF16)],
        scratch_shapes=[pltpu.VMEM((6, GW // LANES, tm, LANES), F32)],
        compiler_params=_cparams("parallel"),
        name="merge_outproj_ln",
    )(oa, o1, o2, o3, l1, l2, l3, gates, gates, x2, wa, wb, wo, ln_g, ln_b)


FFN_TM = 1024
FFN_UP_TN = 512
FFN_DOWN_TN = 256


def _ffn_up_kernel(hb_ref, wg_ref, wu_ref, a_ref):
    hb = hb_ref[...]
    gate = jnp.dot(hb, wg_ref[...].astype(BF16), preferred_element_type=F32)
    up = jnp.dot(hb, wu_ref[...].astype(BF16), preferred_element_type=F32)
    a_ref[...] = (gate * jax.nn.sigmoid(gate) * up).astype(a_ref.dtype)


def _ffn_down_kernel(a_ref, h_ref, wd_ref, g_ref, b_ref, o_ref):
    j = pl.program_id(1)
    tn = wd_ref.shape[1]
    cols = pl.ds(pl.multiple_of(j * tn, tn), tn)
    o_ref[:, cols] = jnp.dot(a_ref[...], wd_ref[...].astype(BF16), preferred_element_type=F32)

    @pl.when(j == pl.num_programs(1) - 1)
    def _():
        o_ref[...] = _layer_norm(DEEPNORM_ALPHA * h_ref[...] + o_ref[...],
                                 g_ref[...], b_ref[...])


def _ffn(hb, h, wg, wu, wd, ln_g, ln_b):
    S = h.shape[0]
    F = wg.shape[1]
    tm = min(FFN_TM, S)
    assert S % tm == 0 and F % FFN_UP_TN == 0 and D_MODEL % FFN_DOWN_TN == 0
    act = pl.pallas_call(
        _ffn_up_kernel,
        grid=(S // tm, F // FFN_UP_TN),
        in_specs=[pl.BlockSpec((tm, D_MODEL), lambda i, j: (i, 0)),
                  pl.BlockSpec((D_MODEL, FFN_UP_TN), lambda i, j: (0, j)),
                  pl.BlockSpec((D_MODEL, FFN_UP_TN), lambda i, j: (0, j))],
        out_specs=pl.BlockSpec((tm, FFN_UP_TN), lambda i, j: (i, j)),
        out_shape=jax.ShapeDtypeStruct((S, F), BF16),
        compiler_params=_cparams("parallel", "arbitrary"),
        name="ffn_gate_up",
    )(hb, wg, wu)

    def per_row_tile(width):
        return pl.BlockSpec((tm, width), lambda i, j: (i, 0), pipeline_mode=pl.Buffered(1))

    return pl.pallas_call(
        _ffn_down_kernel,
        grid=(S // tm, D_MODEL // FFN_DOWN_TN),
        in_specs=[per_row_tile(F), per_row_tile(D_MODEL),
                  pl.BlockSpec((F, FFN_DOWN_TN), lambda i, j: (0, j)),
                  pl.BlockSpec((1, D_MODEL), lambda i, j: (0, 0)),
                  pl.BlockSpec((1, D_MODEL), lambda i, j: (0, 0))],
        out_specs=pl.BlockSpec((tm, D_MODEL), lambda i, j: (i, 0)),
        out_shape=jax.ShapeDtypeStruct((S, D_MODEL), F32),
        compiler_params=_cparams("parallel", "arbitrary"),
        name="ffn_down_ln",
    )(act, h, wd, ln_g, ln_b)


def _mixer_inputs(xb, w_in):
    O = IN_OFFS
    wT = jnp.transpose(w_in)
    qT = _project(xb, wT, O[0], SPARSE_WIDTH, out_dtype=BF16,
                  scale=LOG2_E * HEAD_DIM ** -0.5, transpose_out=True, name="proj_aqT")
    k = _project(xb, wT, O[1], SPARSE_WIDTH, out_dtype=BF16, name="proj_ak")
    vT = _project(xb, wT, O[2], SPARSE_WIDTH, out_dtype=BF16, transpose_out=True,
                  name="proj_avT")
    qiT = _project(xb, wT, O[3], N_IDX_HEADS * IDX_HEAD_DIM, out_dtype=BF16,
                   scale=IDX_HEAD_DIM ** -0.5, transpose_out=True, name="proj_iqT")
    ki, iwT = _project_indexer_kw(xb, wT, O[4], w_scale=N_IDX_HEADS ** -0.5)
    dil_qkv = [_project_dilated_qkv(xb, wT, O[6], g, d) for g, (_, d) in enumerate(DIL_GROUPS)]
    gates = _project(xb, wT, O[9], 2 * D_MODEL, out_dtype=F32, sigmoid=True, name="proj_gates")
    return qT, k, vT, qiT, ki, iwT, dil_qkv, gates


def _layer(x2, w_in, w_a, w_b, w_out, ln1_g, ln1_b, w_gate, w_up, w_down, ln2_g, ln2_b):
    S = x2.shape[0]
    slopes = _alibi_slopes()
    xb = x2.astype(BF16)
    qT, k, vT, qiT, ki, iwT, dil_qkv, gates = _mixer_inputs(xb, w_in)

    topk = min(TOPK_MAX, S // 4)
    oa = _dsa_attention(qT, k, vT, qiT, ki, iwT, slopes[N_DIL_HEADS:], topk=topk)

    dil = []
    for g, (win, d) in enumerate(DIL_GROUPS):
        sl = slopes[g * HEADS_PER_DIL_GROUP:(g + 1) * HEADS_PER_DIL_GROUP]
        dil.append(_dilated_group(dil_qkv[g], win, d, sl))

    h, hb = _merge(oa, dil, gates, x2, w_a.astype(BF16), w_b.astype(BF16), w_out.astype(BF16),
                   ln1_g.reshape(1, -1), ln1_b.reshape(1, -1))
    return _ffn(hb, h, w_gate, w_up, w_down, ln2_g.reshape(1, -1), ln2_b.reshape(1, -1))


def kernel(x, w_in, w_a, w_b, w_out, ln1_g, ln1_b, w_gate, w_up, w_down, ln2_g, ln2_b):
    B, S, D = x.shape
    outs = []
    for b in range(B):
        y = x[b]
        for l in range(w_in.shape[0]):
            y = _layer(y, w_in[l], w_a[l], w_b[l], w_out[l], ln1_g[l], ln1_b[l],
                       w_gate[l], w_up[l], w_down[l], ln2_g[l], ln2_b[l])
        outs.append(y)
    return jnp.stack(outs)
```

```python
import functools
import statistics

import numpy as np
import jax
import jax.numpy as jnp
from jax import lax
from jax.experimental import pallas as pl
from jax.experimental.pallas import tpu as pltpu

F32 = jnp.float32
BF16 = jnp.bfloat16

D_MODEL = 2048
HEAD_DIM = 128
N_SPARSE_HEADS = 8
SPARSE_WIDTH = N_SPARSE_HEADS * HEAD_DIM
N_IDX_HEADS = 16
IDX_HEAD_DIM = 64
TOPK_MAX = 256
DIL_GROUPS = ((128, 1), (512, 4), (2048, 16))
HEADS_PER_DIL_GROUP = 4
N_DIL_HEADS = HEADS_PER_DIL_GROUP * len(DIL_GROUPS)
DIL_WIDTH = N_DIL_HEADS * HEAD_DIM
DIL_OUT_WIDTH = HEADS_PER_DIL_GROUP * HEAD_DIM
N_ALIBI_HEADS = N_DIL_HEADS + N_SPARSE_HEADS
BLOCK = 128
D_FF = ((-(-8 * D_MODEL // 3)) + 255) // 256 * 256
DEPTH = 1
DEEPNORM_ALPHA = (2 * DEPTH) ** 0.25
LN_EPS = 1e-5
IN_SIZES = (SPARSE_WIDTH, SPARSE_WIDTH, SPARSE_WIDTH,
            N_IDX_HEADS * IDX_HEAD_DIM, IDX_HEAD_DIM, N_IDX_HEADS,
            DIL_WIDTH, DIL_WIDTH, DIL_WIDTH, D_MODEL, D_MODEL)
IN_OFFS = tuple(int(c) for c in np.cumsum((0,) + IN_SIZES))

VMEM_LIMIT_BYTES = 56 * 1024 * 1024
LOG2_E = float(np.log2(np.e))
NEG_BIG = -1e30
F32_LOWEST = float(np.finfo(np.float32).min)
F32_HIGHEST = float(np.finfo(np.float32).max)


def _alibi_slopes():
    return [2.0 ** (-8.0 * (i + 1) / N_ALIBI_HEADS) for i in range(N_ALIBI_HEADS)]


def _cparams(*sem):
    return pltpu.CompilerParams(dimension_semantics=sem,
                                vmem_limit_bytes=VMEM_LIMIT_BYTES)


LANES = 128
ROW_ALIGN = 16
PROJ_TM = 1024
PROJ_TN = 1024
IDX_BLOCK = 128


def _x_dot_wT(x_ref, w_ref):
    return lax.dot_general(x_ref[...], w_ref[...].astype(BF16), (((1,), (1,)), ((), ())),
                           preferred_element_type=F32)


def _w_rows_spec(tn, K, row_of_step):
    return pl.BlockSpec((pl.Element(tn), pl.Element(K)),
                        lambda i, j: (pl.multiple_of(row_of_step(j), ROW_ALIGN), 0))


def _proj_kernel(x_ref, w_ref, o_ref, *, scale, sigmoid, transpose_out):
    acc = _x_dot_wT(x_ref, w_ref)
    if scale != 1.0:
        acc = acc * scale
    if sigmoid:
        acc = jax.nn.sigmoid(acc)
    if transpose_out:
        acc = acc.T
    o_ref[...] = acc.astype(o_ref.dtype)


def _project(xb, wT, row0, ncols, *, out_dtype, scale=1.0, sigmoid=False,
             transpose_out=False, name):
    S, K = xb.shape
    tm, tn = min(PROJ_TM, S), PROJ_TN
    assert row0 % ROW_ALIGN == 0 and ncols % tn == 0 and S % tm == 0
    if transpose_out:
        out_shape = jax.ShapeDtypeStruct((ncols, S), out_dtype)
        out_spec = pl.BlockSpec((tn, tm), lambda i, j: (j, i))
    else:
        out_shape = jax.ShapeDtypeStruct((S, ncols), out_dtype)
        out_spec = pl.BlockSpec((tm, tn), lambda i, j: (i, j))
    return pl.pallas_call(
        functools.partial(_proj_kernel, scale=scale, sigmoid=sigmoid,
                          transpose_out=transpose_out),
        grid=(S // tm, ncols // tn),
        in_specs=[pl.BlockSpec((tm, K), lambda i, j: (i, 0)),
                  _w_rows_spec(tn, K, lambda j: row0 + j * tn)],
        out_specs=out_spec,
        out_shape=out_shape,
        compiler_params=_cparams("parallel", "arbitrary"),
        name=name,
    )(xb, wT)


def _proj_idx_kernel(x_ref, w_ref, k_ref, wT_ref, *, w_scale):
    acc = _x_dot_wT(x_ref, w_ref)
    k_ref[...] = acc.astype(k_ref.dtype)
    wT_ref[...] = (acc * w_scale).T


def _project_indexer_kw(xb, wT, row0, *, w_scale):
    S, K = xb.shape
    tm = min(PROJ_TM, S)
    return pl.pallas_call(
        functools.partial(_proj_idx_kernel, w_scale=w_scale),
        grid=(S // tm, 1),
        in_specs=[pl.BlockSpec((tm, K), lambda i, j: (i, 0)),
                  _w_rows_spec(IDX_BLOCK, K, lambda j: row0)],
        out_specs=[pl.BlockSpec((tm, IDX_BLOCK), lambda i, j: (i, 0)),
                   pl.BlockSpec((IDX_BLOCK, tm), lambda i, j: (0, i))],
        out_shape=[jax.ShapeDtypeStruct((S, IDX_BLOCK), BF16),
                   jax.ShapeDtypeStruct((IDX_BLOCK, S), F32)],
        compiler_params=_cparams("parallel", "arbitrary"),
        name="proj_ik_iw",
    )(xb, wT)


def _proj_dilated_kernel(x_ref, wq_ref, wk_ref, wv_ref, o_ref, scr_ref, *, dilation):
    for which, w_ref in enumerate((wq_ref, wk_ref, wv_ref)):
        acc = _x_dot_wT(x_ref, w_ref)
        if which == 0:
            acc = acc * (LOG2_E * HEAD_DIM ** -0.5)
        tm, tn = acc.shape
        if dilation == 1:
            o_ref[which, 0] = acc.astype(o_ref.dtype)
            continue
        for c in range(tn // LANES):
            scr_ref[which, c] = acc[:, c * LANES:(c + 1) * LANES]
        for r in range(dilation):
            for c in range(tn // LANES):
                rows = scr_ref[which, c, pl.ds(r, tm // dilation, stride=dilation), :]
                o_ref[which, r, :, c * LANES:(c + 1) * LANES] = rows.astype(o_ref.dtype)


def _project_dilated_qkv(xb, wT, row0, g, dilation):
    S, K = xb.shape
    d = dilation
    tm, tn = min(PROJ_TM, S), DIL_OUT_WIDTH
    assert S % tm == 0 and tm % (d * ROW_ALIGN) == 0

    def w_spec(which):
        return _w_rows_spec(tn, K, lambda j: row0 + which * DIL_WIDTH + g * tn)

    return pl.pallas_call(
        functools.partial(_proj_dilated_kernel, dilation=d),
        grid=(S // tm, 1),
        in_specs=[pl.BlockSpec((tm, K), lambda i, j: (i, 0)), w_spec(0), w_spec(1), w_spec(2)],
        out_specs=pl.BlockSpec((3, d, tm // d, tn), lambda i, j: (0, 0, i, 0)),
        out_shape=jax.ShapeDtypeStruct((3, d, S // d, tn), BF16),
        scratch_shapes=[pltpu.VMEM((3, tn // LANES, tm, LANES), F32)],
        compiler_params=_cparams("parallel", "arbitrary"),
        name=f"proj_dil_qkv_d{d}",
    )(xb, wT, wT, wT)


DSA_QB = 256
DSA_KC = 256
SEARCH_MAX_ITERS = 512
SEARCH_BISECT_EVERY = 4
SEARCH_FIRST_STEP = 0.0625
ATTN_SLAB = 16
ATTN_HEAD_GROUP = 8
IDX_PER_ATTN_HEAD = N_IDX_HEADS // N_SPARSE_HEADS
IDX_IN_SOFTMAX = 2
MOMENT_ROWS = 64
COUNT_SLAB = 16
COUNT_ACCUMULATORS = 2


def _dsa_kernel(qi_tab, kj_tab,
                qT_ref, k_ref, vT_ref, qiT_ref, qiT_next_ref, ki_ref, wT_ref, wT_next_ref,
                pos_ref, srow_ref, zq_ref,
                o_ref,
                sc2_ref, thr_ref, m_ref, l_ref, acc_ref, s_buf, p_buf, mask_buf, mom_ref, *, topk):
    QB, KC = DSA_QB, DSA_KC
    t = pl.program_id(0)
    i = qi_tab[t]
    j = kj_tab[t]
    cur = i % 2
    sc_ref = sc2_ref.at[cur]

    @pl.when(t == 0)
    def _first():
        mom_ref[...] = jnp.zeros(mom_ref.shape, F32)

    def score_heads(acc, kic, q_ref, w_ref, heads):
        for h in heads:
            r = jnp.dot(kic, q_ref[h * IDX_HEAD_DIM:(h + 1) * IDX_HEAD_DIM, :],
                        preferred_element_type=F32)
            w_h = w_ref[IDX_HEAD_DIM + h:IDX_HEAD_DIM + h + 1, :]
            acc = acc + w_h * jnp.maximum(r, 0.0)
        return acc

    def chunk_keys(c):
        row0 = pl.multiple_of(c * KC, KC)
        return ki_ref[pl.ds(row0, KC), 0:IDX_HEAD_DIM]

    def chunk_scores(q_ref, w_ref, dst_ref, c, diagonal):
        acc = score_heads(jnp.zeros((KC, QB), F32), chunk_keys(c), q_ref, w_ref,
                          range(N_IDX_HEADS))
        return finish_scores(acc, dst_ref, c, diagonal)

    def finish_scores(acc, dst_ref, c, diagonal):
        row0 = pl.multiple_of(c * KC, KC)
        if diagonal:
            kpos = lax.broadcasted_iota(jnp.int32, (KC, QB), 0)
            qpos = lax.broadcasted_iota(jnp.int32, (KC, QB), 1)
            causal = kpos <= qpos
            live = jnp.where(causal, acc, 0.0)
            acc = jnp.where(causal, acc, -jnp.inf)
        else:
            live = acc
        dst_ref[pl.ds(row0, KC), :] = acc
        live = live[:MOMENT_ROWS, :]
        s1 = jnp.sum(live.reshape(MOMENT_ROWS // 8, 8, QB), axis=0)
        s2 = jnp.sum((live * live).reshape(MOMENT_ROWS // 8, 8, QB), axis=0)
        return s1, s2

    def count_ge(cand):
        candb = jnp.broadcast_to(cand, (COUNT_SLAB, QB))

        def body(c, parts):
            row0 = pl.multiple_of(c * (2 * KC), 2 * KC)
            parts = list(parts)
            for r in range(2 * KC // COUNT_SLAB):
                slab = sc_ref[pl.ds(row0 + r * COUNT_SLAB, COUNT_SLAB), :]
                parts[r % len(parts)] = parts[r % len(parts)] + jnp.where(slab >= candb, 1.0, 0.0)
            return tuple(parts)
        zero = jnp.zeros((COUNT_SLAB, QB), F32)
        parts = lax.fori_loop(0, (i + 2) // 2, body, (zero,) * COUNT_ACCUMULATORS)
        return jnp.sum(sum(parts), axis=0, keepdims=True)

    @pl.when(j == 0)
    def _prepare():
        d1, d2 = chunk_scores(qiT_ref, wT_ref, sc_ref, i, True)
        s1 = jnp.sum(mom_ref[cur, 0] + d1, axis=0, keepdims=True)
        s2 = jnp.sum(mom_ref[cur, 1] + d2, axis=0, keepdims=True)
        mom_ref[1 - cur] = jnp.zeros(mom_ref.shape[1:], F32)

        @pl.when(i % 2 == 0)
        def _pad():
            sc_ref[pl.ds(pl.multiple_of((i + 1) * KC, KC), KC), :] = jnp.full((KC, QB), -jnp.inf, F32)

        kf = float(topk)
        log_k = float(np.log(topk - 0.5))
        lane = lax.broadcasted_iota(jnp.int32, (1, QB), 1)
        n_causal = (i * QB + 1 + lane).astype(F32)
        n_sample = (i * MOMENT_ROWS + jnp.minimum(lane + 1, MOMENT_ROWS)).astype(F32)
        mean = s1 / n_sample
        std = jnp.sqrt(jnp.maximum(s2 / n_sample - mean * mean, 0.0))
        zq = jnp.max(zq_ref[...], axis=0, keepdims=True)
        guess = mean + zq * std
        few = n_causal <= kf
        lo = jnp.full((1, QB), F32_LOWEST, F32)
        clo = n_causal
        hi = jnp.full((1, QB), F32_HIGHEST, F32)
        chi = jnp.zeros((1, QB), F32)
        done_f = jnp.where(few | (clo == kf), 1.0, 0.0)
        zero = jnp.zeros((1, QB), F32)
        one = jnp.ones((1, QB), F32)

        def cond(st):
            return jnp.logical_and(st[0] < SEARCH_MAX_ITERS, jnp.min(st[5]) < 0.5)

        def step(st):
            it, lo, hi, clo, chi, done_f, n_up, n_dn, w_lo, w_hi, last = st
            done = done_f > 0.5
            mid = 0.5 * lo + 0.5 * hi
            f_lo = (jnp.log(clo) - log_k) * w_lo
            f_hi = (log_k - jnp.log(jnp.maximum(chi, 0.5))) * w_hi
            interp = lo + (hi - lo) * (f_lo / jnp.maximum(f_lo + f_hi, 1e-9))
            open_hi = hi >= F32_HIGHEST
            open_lo = lo <= F32_LOWEST
            cand = jnp.where(it % SEARCH_BISECT_EVERY == SEARCH_BISECT_EVERY - 1, mid, interp)
            cand = jnp.where(open_hi, lo + std * (SEARCH_FIRST_STEP * jnp.exp2(n_up)), cand)
            cand = jnp.where(open_lo, hi - std * (SEARCH_FIRST_STEP * jnp.exp2(n_dn)), cand)
            cand = jnp.where(open_lo & open_hi, guess, cand)
            bad = jnp.logical_not((cand > lo) & (cand < hi))
            cand = jnp.where(bad, mid, cand)
            cand = jnp.where(done, lo, cand)
            cnt = count_ge(cand)
            up = jnp.logical_not(done) & (cnt >= kf)
            dn = jnp.logical_not(done) & (cnt < kf)
            n_up = n_up + jnp.where(up & open_hi, 1.0, 0.0)
            n_dn = n_dn + jnp.where(dn & open_lo, 1.0, 0.0)
            w_hi = jnp.where(up, jnp.where(last == 1.0, 0.5 * w_hi, 1.0), jnp.where(dn, 1.0, w_hi))
            w_lo = jnp.where(dn, jnp.where(last == 2.0, 0.5 * w_lo, 1.0), jnp.where(up, 1.0, w_lo))
            last = jnp.where(up, 1.0, jnp.where(dn, 2.0, last))
            lo = jnp.where(up, cand, lo)
            clo = jnp.where(up, cnt, clo)
            hi = jnp.where(dn, cand, hi)
            chi = jnp.where(dn, cnt, chi)
            mid2 = 0.5 * lo + 0.5 * hi
            stuck = jnp.logical_not((mid2 > lo) & (mid2 < hi))
            done_f = jnp.where(done | (clo == kf) | stuck, 1.0, 0.0)
            return it + 1, lo, hi, clo, chi, done_f, n_up, n_dn, w_lo, w_hi, last

        st = lax.while_loop(cond, step, (jnp.int32(0), lo, hi, clo, chi, done_f,
                                         zero, zero, one, one, zero))
        lo, hi, clo, chi = st[1], st[2], st[3], st[4]
        lo = jnp.where(few, F32_LOWEST, lo)

        need = kf - chi
        tied = jnp.logical_and(clo > kf, jnp.logical_not(few))

        @pl.when(jnp.max(tied.astype(F32)) > 0.5)
        def _break_ties():
            def count_eq_le(cut):
                def body(c, part):
                    row0 = pl.multiple_of(c * KC, KC)
                    tile = sc_ref[pl.ds(row0, KC), :]
                    kpos = (row0 + lax.broadcasted_iota(jnp.int32, (KC, QB), 0)).astype(F32)
                    ind = jnp.where((tile == lo) & (kpos <= cut), 1.0, 0.0)
                    return part + jnp.sum(ind.reshape(KC // 8, 8, QB), axis=0)
                part = lax.fori_loop(0, i + 1, body, jnp.zeros((8, QB), F32))
                return jnp.sum(part, axis=0, keepdims=True)

            def bis(_, st):
                a, b = st
                mid = jnp.floor(0.5 * (a + b))
                ok = count_eq_le(mid) <= need
                return jnp.where(ok, mid, a), jnp.where(ok, b, mid)

            a0 = jnp.full((1, QB), -1.0, F32)
            b0 = ((i + 1) * KC).astype(F32) + jnp.zeros((1, QB), F32)
            n_bits = int(np.ceil(np.log2(sc_ref.shape[0] + 2))) + 1
            cut, _ = lax.fori_loop(0, n_bits, bis, (a0, b0))

            def drop(c, _):
                row0 = pl.multiple_of(c * KC, KC)
                tile = sc_ref[pl.ds(row0, KC), :]
                kpos = (row0 + lax.broadcasted_iota(jnp.int32, (KC, QB), 0)).astype(F32)
                kill = tied & (tile == lo) & (kpos > cut)
                sc_ref[pl.ds(row0, KC), :] = jnp.where(kill, -jnp.inf, tile)
                return 0
            lax.fori_loop(0, i + 1, drop, 0)

        thr_ref[...] = jnp.broadcast_to(lo, thr_ref.shape)
        m_ref[...] = jnp.full(m_ref.shape, NEG_BIG, F32)
        l_ref[...] = jnp.zeros(l_ref.shape, F32)
        acc_ref[...] = jnp.zeros(acc_ref.shape, F32)

    row0 = pl.multiple_of(j * KC, KC)
    SL = ATTN_SLAB
    n_slab = KC // SL
    thr = jnp.broadcast_to(thr_ref[0:1, :], (SL, QB))
    for r in range(n_slab):
        rows = slice(r * SL, (r + 1) * SL)
        sc = sc_ref[pl.ds(row0 + r * SL, SL), :]
        mask_buf[rows, :] = jnp.where(sc >= thr, 0.0, NEG_BIG)
    posk = pos_ref[...]
    nxt_keys = chunk_keys(j)
    nxt_acc = jnp.zeros((KC, QB), F32)
    for g0 in range(0, N_SPARSE_HEADS, ATTN_HEAD_GROUP):
        heads = range(g0, g0 + ATTN_HEAD_GROUP)
        col_max = {}
        for h in heads:
            hs = slice(h * HEAD_DIM, (h + 1) * HEAD_DIM)
            lhs = jnp.concatenate([k_ref[:, hs], posk], axis=1)
            rhs = jnp.concatenate([qT_ref[hs, :], srow_ref[h]], axis=0)
            s = jnp.dot(lhs, rhs, preferred_element_type=F32)
            cm = [jnp.full((SL, QB), NEG_BIG, F32)] * 2
            for r in range(n_slab):
                rows = slice(r * SL, (r + 1) * SL)
                blk = s[rows, :] + mask_buf[rows, :]
                s_buf[h, rows, :] = blk
                cm[r % 2] = jnp.maximum(cm[r % 2], blk)
            col_max[h] = jnp.max(jnp.maximum(cm[0], cm[1]), axis=0, keepdims=True)
        alphas = {}
        for h in heads:
            nxt_acc = score_heads(nxt_acc, nxt_keys, qiT_next_ref, wT_next_ref,
                                  range(h * IDX_PER_ATTN_HEAD, h * IDX_PER_ATTN_HEAD + IDX_IN_SOFTMAX))
            m_old = m_ref[h]
            m_new = jnp.maximum(m_old, col_max[h])
            alphas[h] = jnp.exp2(m_old - m_new)
            m_ref[h] = m_new
            mb = jnp.concatenate([m_new] * (SL // 8), axis=0)
            for r in range(n_slab):
                rows = slice(r * SL, (r + 1) * SL)
                p_buf[h, rows, :] = jnp.exp2(s_buf[h, rows, :] - mb).astype(BF16)
        ones_rows = jnp.ones((ROW_ALIGN, KC), BF16)
        for h in heads:
            nxt_acc = score_heads(nxt_acc, nxt_keys, qiT_next_ref, wT_next_ref,
                                  range(h * IDX_PER_ATTN_HEAD + IDX_IN_SOFTMAX,
                                        (h + 1) * IDX_PER_ATTN_HEAD))
            hs = slice(h * HEAD_DIM, (h + 1) * HEAD_DIM)
            v_ext = jnp.concatenate([vT_ref[hs, :], ones_rows], axis=0)
            pv = jnp.dot(v_ext, p_buf[h], preferred_element_type=F32)
            acc_ref[hs, :] = alphas[h][0:1, :] * acc_ref[hs, :] + pv[:HEAD_DIM, :]
            l_ref[h] = alphas[h] * l_ref[h] + pv[HEAD_DIM:HEAD_DIM + 1, :]

    n1, n2 = finish_scores(nxt_acc, sc2_ref.at[1 - cur], j, False)
    mom_ref[1 - cur, 0] += n1
    mom_ref[1 - cur, 1] += n2

    @pl.when(j == i)
    def _finish():
        for h in range(N_SPARSE_HEADS):
            hs = slice(h * HEAD_DIM, (h + 1) * HEAD_DIM)
            o = acc_ref[hs, :] / l_ref[h, 0:1, :]
            o_ref[:, hs] = o.T.astype(o_ref.dtype)


N_SLOPE_PIECES = 3
POS_SPLIT = 128


def _alibi_mxu_operands(S, slopes_log2):
    assert S <= POS_SPLIT * 256
    kpos = jnp.arange(S, dtype=jnp.int32)[:, None]
    col = jnp.arange(128, dtype=jnp.int32)[None, :]
    part = jnp.where(col % 2 == 0, kpos // POS_SPLIT, kpos % POS_SPLIT)
    pos = jnp.where(col < 2 * N_SLOPE_PIECES, part, 0).astype(BF16)
    rows = np.zeros((len(slopes_log2), 128), np.float32)
    for h, s in enumerate(slopes_log2):
        rest = np.float64(s)
        for p in range(N_SLOPE_PIECES):
            piece = np.float64(np.asarray(rest, np.float32).astype(jnp.bfloat16).astype(np.float32))
            rows[h, 2 * p] = POS_SPLIT * piece
            rows[h, 2 * p + 1] = piece
            rest = rest - piece
    srow = jnp.broadcast_to(jnp.asarray(rows, F32)[:, :, None],
                            (len(slopes_log2), 128, DSA_QB)).astype(BF16)
    return pos, srow


def _dsa_attention(qT, k, vT, qiT, ki, wT, slopes, *, topk):
    S = k.shape[0]
    QB, KC = DSA_QB, DSA_KC
    assert S % (2 * QB) == 0 and QB == KC
    nq = S // QB
    pos, srow = _alibi_mxu_operands(S, [s * LOG2_E for s in slopes])
    quantile = statistics.NormalDist().inv_cdf
    zq = np.array([quantile(1.0 - (topk - 0.5) / n) if n > topk else 0.0
                   for n in range(1, S + 1)], np.float32)
    zq = np.broadcast_to(zq, (8, S))
    qi_tab = np.concatenate([np.full(i + 1, i, np.int32) for i in range(nq)])
    kj_tab = np.concatenate([np.arange(i + 1, dtype=np.int32) for i in range(nq)])
    n_steps = int(qi_tab.shape[0])
    W = N_SPARSE_HEADS * HEAD_DIM
    WI = N_IDX_HEADS * IDX_HEAD_DIM
    grid_spec = pltpu.PrefetchScalarGridSpec(
        num_scalar_prefetch=2,
        grid=(n_steps,),
        in_specs=[
            pl.BlockSpec((W, QB), lambda t, qi, kj: (0, qi[t])),
            pl.BlockSpec((KC, W), lambda t, qi, kj: (kj[t], 0)),
            pl.BlockSpec((W, KC), lambda t, qi, kj: (0, kj[t])),
            pl.BlockSpec((WI, QB), lambda t, qi, kj: (0, qi[t])),
            pl.BlockSpec((WI, QB), lambda t, qi, kj: (0, jnp.minimum(qi[t] + 1, nq - 1))),
            pl.BlockSpec((S, IDX_BLOCK), lambda t, qi, kj: (0, 0)),
            pl.BlockSpec((IDX_BLOCK, QB), lambda t, qi, kj: (0, qi[t])),
            pl.BlockSpec((IDX_BLOCK, QB), lambda t, qi, kj: (0, jnp.minimum(qi[t] + 1, nq - 1))),
            pl.BlockSpec((KC, 128), lambda t, qi, kj: (kj[t], 0)),
            pl.BlockSpec(srow.shape, lambda t, qi, kj: (0, 0, 0)),
            pl.BlockSpec((8, QB), lambda t, qi, kj: (0, qi[t])),
        ],
        out_specs=pl.BlockSpec((QB, W), lambda t, qi, kj: (qi[t], 0)),
        scratch_shapes=[
            pltpu.VMEM((2, S, QB), F32),
            pltpu.VMEM((8, QB), F32),
            pltpu.VMEM((N_SPARSE_HEADS, 8, QB), F32),
            pltpu.VMEM((N_SPARSE_HEADS, 8, QB), F32),
            pltpu.VMEM((W, QB), F32),
            pltpu.VMEM((N_SPARSE_HEADS, KC, QB), F32),
            pltpu.VMEM((N_SPARSE_HEADS, KC, QB), BF16),
            pltpu.VMEM((KC, QB), F32),
            pltpu.VMEM((2, 2, 8, QB), F32),
        ],
    )
    return pl.pallas_call(
        functools.partial(_dsa_kernel, topk=topk),
        grid_spec=grid_spec,
        out_shape=jax.ShapeDtypeStruct((S, W), BF16),
        compiler_params=_cparams("arbitrary"),
        name="dsa_attention",
    )(jnp.asarray(qi_tab), jnp.asarray(kj_tab), qT, k, vT, qiT, qiT, ki, wT, wT, pos, srow,
      jnp.asarray(zq))


DIL_BLOCKS_PER_STEP = 4
def _dil_kernel(bias_ref, q_ref, kc_ref, kp_ref, vc_ref, vp_ref, o_ref, lse_ref, *, n_blocks):
    first = jnp.where(pl.program_id(1) == 0, 1, 0)
    Q = BLOCK
    for b in range(n_blocks):
        rows = slice(b * Q, (b + 1) * Q)
        prev_rows = slice((b - 1) * Q, b * Q)
        for h in range(HEADS_PER_DIL_GROUP):
            hs = slice(h * HEAD_DIM, (h + 1) * HEAD_DIM)
            q = q_ref[rows, hs]
            k_prev = kp_ref[:, hs] if b == 0 else kc_ref[prev_rows, hs]
            v_prev = vp_ref[:, hs] if b == 0 else vc_ref[prev_rows, hs]
            kk = jnp.concatenate([k_prev, kc_ref[rows, hs]], axis=0)
            vv = jnp.concatenate([v_prev, vc_ref[rows, hs]], axis=0)
            bias = bias_ref[first, h] if b == 0 else bias_ref[0, h]
            s = lax.dot_general(q, kk, (((1,), (1,)), ((), ())),
                                preferred_element_type=F32) + bias
            mx = jnp.max(s, axis=-1, keepdims=True)
            p = jnp.exp2(s - mx)
            l = jnp.sum(p, axis=-1, keepdims=True)
            o = jnp.dot(p.astype(BF16), vv, preferred_element_type=F32) / l
            o_ref[rows, hs] = o
            lse_ref[rows, hs] = jnp.broadcast_to(mx + jnp.log2(l), (Q, HEAD_DIM))


def _dilated_bias(span, dilation, slopes_log2):
    Q = BLOCK
    step = Q + np.arange(Q)[:, None] - np.arange(2 * Q)[None, :]
    band = (step >= 0) & (step <= span)
    has_prev = np.stack([np.ones_like(band), np.broadcast_to(np.arange(2 * Q)[None, :] >= Q, band.shape)])
    dist = (step * dilation).astype(np.float64)
    bias = -np.asarray(slopes_log2, np.float64)[None, :, None, None] * dist[None, None]
    keep = (band[None] & has_prev)[:, None]
    return jnp.asarray(np.where(keep, bias, NEG_BIG), F32)


def _dilated_group(qkv, window, dilation, slopes_g):
    _, d, n_sub, GW = qkv.shape
    assert d == dilation
    span = window // d
    nbs = min(DIL_BLOCKS_PER_STEP, n_sub // BLOCK)
    rows = nbs * BLOCK
    assert span <= BLOCK and n_sub % rows == 0
    bias = _dilated_bias(span, d, [s * LOG2_E for s in slopes_g])

    def cur(which):
        return pl.BlockSpec((None, None, rows, GW), lambda r, m: (which, r, m, 0))

    def prev(which):
        return pl.BlockSpec((None, None, BLOCK, GW),
                            lambda r, m: (which, r, jnp.maximum(m * nbs - 1, 0), 0))

    out_spec = pl.BlockSpec((None, rows, GW), lambda r, m: (r, m, 0))
    return pl.pallas_call(
        functools.partial(_dil_kernel, n_blocks=nbs),
        grid=(d, n_sub // rows),
        in_specs=[pl.BlockSpec(bias.shape, lambda r, m: (0, 0, 0, 0)),
                  cur(0), cur(1), prev(1), cur(2), prev(2)],
        out_specs=[out_spec, out_spec],
        out_shape=[jax.ShapeDtypeStruct((d, n_sub, GW), F32)] * 2,
        compiler_params=_cparams("parallel", "arbitrary"),
        name=f"dilated_attention_d{d}",
    )(bias, qkv, qkv, qkv, qkv, qkv)


def _layer_norm(y, g, b):
    mu = jnp.mean(y, axis=-1, keepdims=True)
    yc = y - mu
    var = jnp.mean(yc * yc, axis=-1, keepdims=True)
    return yc * lax.rsqrt(var + LN_EPS) * g + b


def _merge_kernel(oa_ref, o1_ref, o2_ref, o3_ref, l1_ref, l2_ref, l3_ref,
                  ga_ref, gb_ref, x_ref, wa_ref, wb_ref, wo_ref, g_ref, b_ref,
                  h_ref, hb_ref, nat_ref):
    tm = x_ref.shape[0]
    n_slab = DIL_OUT_WIDTH // LANES

    def natural(ref, slot):
        d = ref.shape[0]
        if d == 1:
            return [ref[0, :, c * LANES:(c + 1) * LANES] for c in range(n_slab)]
        for r in range(d):
            for c in range(n_slab):
                nat_ref[slot, c, pl.ds(r, tm // d, stride=d), :] = ref[r, :, c * LANES:(c + 1) * LANES]
        return [nat_ref[slot, c] for c in range(n_slab)]

    o1, l1 = natural(o1_ref, 0), natural(l1_ref, 1)
    o2, l2 = natural(o2_ref, 2), natural(l2_ref, 3)
    o3, l3 = natural(o3_ref, 4), natural(l3_ref, 5)
    slabs = []
    for c in range(n_slab):
        lm = jnp.maximum(jnp.maximum(l1[c], l2[c]), l3[c])
        e1, e2, e3 = jnp.exp2(l1[c] - lm), jnp.exp2(l2[c] - lm), jnp.exp2(l3[c] - lm)
        den = e1 + e2 + e3
        slabs.append((e1 / den) * o1[c] + (e2 / den) * o2[c] + (e3 / den) * o3[c])
    ob = jnp.concatenate(slabs, axis=1)
    ya = jnp.dot(oa_ref[...], wa_ref[...], preferred_element_type=F32)
    yb = jnp.dot(ob.astype(BF16), wb_ref[...], preferred_element_type=F32)
    merged = ga_ref[...] * ya + gb_ref[...] * yb
    mix = jnp.dot(merged.astype(BF16), wo_ref[...], preferred_element_type=F32)
    h = _layer_norm(DEEPNORM_ALPHA * x_ref[...] + mix, g_ref[...], b_ref[...])
    h_ref[...] = h
    hb_ref[...] = h.astype(BF16)


MERGE_TM = 256


def _merge(oa, dil, gates, x2, wa, wb, wo, ln_g, ln_b):
    S = x2.shape[0]
    tm = MERGE_TM
    (o1, l1), (o2, l2), (o3, l3) = dil
    GW = DIL_OUT_WIDTH

    def rows(w, col=0):
        return pl.BlockSpec((tm, w), lambda i: (i, col))

    def residue_major(a):
        d = a.shape[0]
        assert tm % (8 * d) == 0
        return pl.BlockSpec((d, tm // d, GW), lambda i: (0, i, 0))

    def whole(a):
        return pl.BlockSpec(a.shape, lambda i: (0,) * a.ndim, pipeline_mode=pl.Buffered(1))

    return pl.pallas_call(
        _merge_kernel,
        grid=(S // tm,),
        in_specs=[rows(SPARSE_WIDTH),
                  residue_major(o1), residue_major(o2), residue_major(o3),
                  residue_major(l1), residue_major(l2), residue_major(l3),
                  rows(D_MODEL, 0), rows(D_MODEL, 1), rows(D_MODEL),
                  whole(wa), whole(wb), whole(wo), whole(ln_g), whole(ln_b)],
        out_specs=[rows(D_MODEL), rows(D_MODEL)],
        out_shape=[jax.ShapeDtypeStruct((S, D_MODEL), F32),
                   jax.ShapeDtypeStruct((S, D_MODEL), BF16)],
        scratch_shapes=[pltpu.VMEM((6, GW // LANES, tm, LANES), F32)],
        compiler_params=_cparams("parallel"),
        name="merge_outproj_ln",
    )(oa, o1, o2, o3, l1, l2, l3, gates, gates, x2, wa, wb, wo, ln_g, ln_b)


FFN_TM = 1024
FFN_UP_TN = 512
FFN_DOWN_TN = 256


def _ffn_up_kernel(hb_ref, wg_ref, wu_ref, a_ref):
    hb = hb_ref[...]
    gate = jnp.dot(hb, wg_ref[...].astype(BF16), preferred_element_type=F32)
    up = jnp.dot(hb, wu_ref[...].astype(BF16), preferred_element_type=F32)
    a_ref[...] = (gate * jax.nn.sigmoid(gate) * up).astype(a_ref.dtype)


def _ffn_down_kernel(a_ref, h_ref, wd_ref, g_ref, b_ref, o_ref):
    j = pl.program_id(1)
    tn = wd_ref.shape[1]
    cols = pl.ds(pl.multiple_of(j * tn, tn), tn)
    o_ref[:, cols] = jnp.dot(a_ref[...], wd_ref[...].astype(BF16), preferred_element_type=F32)

    @pl.when(j == pl.num_programs(1) - 1)
    def _():
        o_ref[...] = _layer_norm(DEEPNORM_ALPHA * h_ref[...] + o_ref[...],
                                 g_ref[...], b_ref[...])


def _ffn(hb, h, wg, wu, wd, ln_g, ln_b):
    S = h.shape[0]
    F = wg.shape[1]
    tm = min(FFN_TM, S)
    assert S % tm == 0 and F % FFN_UP_TN == 0 and D_MODEL % FFN_DOWN_TN == 0
    act = pl.pallas_call(
        _ffn_up_kernel,
        grid=(S // tm, F // FFN_UP_TN),
        in_specs=[pl.BlockSpec((tm, D_MODEL), lambda i, j: (i, 0)),
                  pl.BlockSpec((D_MODEL, FFN_UP_TN), lambda i, j: (0, j)),
                  pl.BlockSpec((D_MODEL, FFN_UP_TN), lambda i, j: (0, j))],
        out_specs=pl.BlockSpec((tm, FFN_UP_TN), lambda i, j: (i, j)),
        out_shape=jax.ShapeDtypeStruct((S, F), BF16),
        compiler_params=_cparams("parallel", "arbitrary"),
        name="ffn_gate_up",
    )(hb, wg, wu)

    def per_row_tile(width):
        return pl.BlockSpec((tm, width), lambda i, j: (i, 0), pipeline_mode=pl.Buffered(1))

    return pl.pallas_call(
        _ffn_down_kernel,
        grid=(S // tm, D_MODEL // FFN_DOWN_TN),
        in_specs=[per_row_tile(F), per_row_tile(D_MODEL),
                  pl.BlockSpec((F, FFN_DOWN_TN), lambda i, j: (0, j)),
                  pl.BlockSpec((1, D_MODEL), lambda i, j: (0, 0)),
                  pl.BlockSpec((1, D_MODEL), lambda i, j: (0, 0))],
        out_specs=pl.BlockSpec((tm, D_MODEL), lambda i, j: (i, 0)),
        out_shape=jax.ShapeDtypeStruct((S, D_MODEL), F32),
        compiler_params=_cparams("parallel", "arbitrary"),
        name="ffn_down_ln",
    )(act, h, wd, ln_g, ln_b)


def _mixer_inputs(xb, w_in):
    O = IN_OFFS
    wT = jnp.transpose(w_in)
    qT = _project(xb, wT, O[0], SPARSE_WIDTH, out_dtype=BF16,
                  scale=LOG2_E * HEAD_DIM ** -0.5, transpose_out=True, name="proj_aqT")
    k = _project(xb, wT, O[1], SPARSE_WIDTH, out_dtype=BF16, name="proj_ak")
    vT = _project(xb, wT, O[2], SPARSE_WIDTH, out_dtype=BF16, transpose_out=True,
                  name="proj_avT")
    qiT = _project(xb, wT, O[3], N_IDX_HEADS * IDX_HEAD_DIM, out_dtype=BF16,
                   scale=IDX_HEAD_DIM ** -0.5, transpose_out=True, name="proj_iqT")
    ki, iwT = _project_indexer_kw(xb, wT, O[4], w_scale=N_IDX_HEADS ** -0.5)
    dil_qkv = [_project_dilated_qkv(xb, wT, O[6], g, d) for g, (_, d) in enumerate(DIL_GROUPS)]
    gates = _project(xb, wT, O[9], 2 * D_MODEL, out_dtype=F32, sigmoid=True, name="proj_gates")
    return qT, k, vT, qiT, ki, iwT, dil_qkv, gates


def _layer(x2, w_in, w_a, w_b, w_out, ln1_g, ln1_b, w_gate, w_up, w_down, ln2_g, ln2_b):
    S = x2.shape[0]
    slopes = _alibi_slopes()
    xb = x2.astype(BF16)
    qT, k, vT, qiT, ki, iwT, dil_qkv, gates = _mixer_inputs(xb, w_in)

    topk = min(TOPK_MAX, S // 4)
    oa = _dsa_attention(qT, k, vT, qiT, ki, iwT, slopes[N_DIL_HEADS:], topk=topk)

    dil = []
    for g, (win, d) in enumerate(DIL_GROUPS):
        sl = slopes[g * HEADS_PER_DIL_GROUP:(g + 1) * HEADS_PER_DIL_GROUP]
        dil.append(_dilated_group(dil_qkv[g], win, d, sl))

    h, hb = _merge(oa, dil, gates, x2, w_a.astype(BF16), w_b.astype(BF16), w_out.astype(BF16),
                   ln1_g.reshape(1, -1), ln1_b.reshape(1, -1))
    return _ffn(hb, h, w_gate, w_up, w_down, ln2_g.reshape(1, -1), ln2_b.reshape(1, -1))


def kernel(x, w_in, w_a, w_b, w_out, ln1_g, ln1_b, w_gate, w_up, w_down, ln2_g, ln2_b):
    B, S, D = x.shape
    outs = []
    for b in range(B):
        y = x[b]
        for l in range(w_in.shape[0]):
            y = _layer(y, w_in[l], w_a[l], w_b[l], w_out[l], ln1_g[l], ln1_b[l],
                       w_gate[l], w_up[l], w_down[l], ln2_g[l], ln2_b[l])
        outs.append(y)
    return jnp.stack(outs)
```

```python
import functools
import statistics

import numpy as np
import jax
import jax.numpy as jnp
from jax import lax
from jax.experimental import pallas as pl
from jax.experimental.pallas import tpu as pltpu

F32 = jnp.float32
BF16 = jnp.bfloat16

D_MODEL = 2048
HEAD_DIM = 128
N_SPARSE_HEADS = 8
SPARSE_WIDTH = N_SPARSE_HEADS * HEAD_DIM
N_IDX_HEADS = 16
IDX_HEAD_DIM = 64
TOPK_MAX = 256
DIL_GROUPS = ((128, 1), (512, 4), (2048, 16))
HEADS_PER_DIL_GROUP = 4
N_DIL_HEADS = HEADS_PER_DIL_GROUP * len(DIL_GROUPS)
DIL_WIDTH = N_DIL_HEADS * HEAD_DIM
DIL_OUT_WIDTH = HEADS_PER_DIL_GROUP * HEAD_DIM
N_ALIBI_HEADS = N_DIL_HEADS + N_SPARSE_HEADS
BLOCK = 128
D_FF = ((-(-8 * D_MODEL // 3)) + 255) // 256 * 256
DEPTH = 1
DEEPNORM_ALPHA = (2 * DEPTH) ** 0.25
LN_EPS = 1e-5
IN_SIZES = (SPARSE_WIDTH, SPARSE_WIDTH, SPARSE_WIDTH,
            N_IDX_HEADS * IDX_HEAD_DIM, IDX_HEAD_DIM, N_IDX_HEADS,
            DIL_WIDTH, DIL_WIDTH, DIL_WIDTH, D_MODEL, D_MODEL)
IN_OFFS = tuple(int(c) for c in np.cumsum((0,) + IN_SIZES))

VMEM_LIMIT_BYTES = 56 * 1024 * 1024
LOG2_E = float(np.log2(np.e))
NEG_BIG = -1e30
F32_LOWEST = float(np.finfo(np.float32).min)
F32_HIGHEST = float(np.finfo(np.float32).max)


def _alibi_slopes():
    return [2.0 ** (-8.0 * (i + 1) / N_ALIBI_HEADS) for i in range(N_ALIBI_HEADS)]


def _cparams(*sem):
    return pltpu.CompilerParams(dimension_semantics=sem,
                                vmem_limit_bytes=VMEM_LIMIT_BYTES)


LANES = 128
ROW_ALIGN = 16
PROJ_TM = 1024
PROJ_TN = 1024
IDX_BLOCK = 128


def _x_dot_wT(x_ref, w_ref):
    return lax.dot_general(x_ref[...], w_ref[...].astype(BF16), (((1,), (1,)), ((), ())),
                           preferred_element_type=F32)


def _w_rows_spec(tn, K, row_of_step):
    return pl.BlockSpec((pl.Element(tn), pl.Element(K)),
                        lambda i, j: (pl.multiple_of(row_of_step(j), ROW_ALIGN), 0))


def _proj_kernel(x_ref, w_ref, o_ref, *, scale, sigmoid, transpose_out):
    acc = _x_dot_wT(x_ref, w_ref)
    if scale != 1.0:
        acc = acc * scale
    if sigmoid:
        acc = jax.nn.sigmoid(acc)
    if transpose_out:
        acc = acc.T
    o_ref[...] = acc.astype(o_ref.dtype)


def _project(xb, wT, row0, ncols, *, out_dtype, scale=1.0, sigmoid=False,
             transpose_out=False, name):
    S, K = xb.shape
    tm, tn = min(PROJ_TM, S), PROJ_TN
    assert row0 % ROW_ALIGN == 0 and ncols % tn == 0 and S % tm == 0
    if transpose_out:
        out_shape = jax.ShapeDtypeStruct((ncols, S), out_dtype)
        out_spec = pl.BlockSpec((tn, tm), lambda i, j: (j, i))
    else:
        out_shape = jax.ShapeDtypeStruct((S, ncols), out_dtype)
        out_spec = pl.BlockSpec((tm, tn), lambda i, j: (i, j))
    return pl.pallas_call(
        functools.partial(_proj_kernel, scale=scale, sigmoid=sigmoid,
                          transpose_out=transpose_out),
        grid=(S // tm, ncols // tn),
        in_specs=[pl.BlockSpec((tm, K), lambda i, j: (i, 0)),
                  _w_rows_spec(tn, K, lambda j: row0 + j * tn)],
        out_specs=out_spec,
        out_shape=out_shape,
        compiler_params=_cparams("parallel", "arbitrary"),
        name=name,
    )(xb, wT)


def _proj_cast_kernel(x_ref, w_ref, o_ref, xb_ref):
    xb_ref[...] = x_ref[...].astype(xb_ref.dtype)
    o_ref[...] = _x_dot_wT(xb_ref, w_ref).astype(o_ref.dtype)


def _project_and_cast(x2, wT, row0, ncols, *, name):
    S, K = x2.shape
    tm = min(PROJ_TM, S)
    assert ncols == PROJ_TN and S % tm == 0
    return pl.pallas_call(
        _proj_cast_kernel,
        grid=(S // tm, 1),
        in_specs=[pl.BlockSpec((tm, K), lambda i, j: (i, 0)),
                  _w_rows_spec(ncols, K, lambda j: row0)],
        out_specs=[pl.BlockSpec((tm, ncols), lambda i, j: (i, 0)),
                   pl.BlockSpec((tm, K), lambda i, j: (i, 0))],
        out_shape=[jax.ShapeDtypeStruct((S, ncols), BF16), jax.ShapeDtypeStruct((S, K), BF16)],
        compiler_params=_cparams("parallel", "arbitrary"),
        name=name,
    )(x2, wT)


def _proj_indexer_kernel(x_ref, wq_ref, wkw_ref, qT_ref, k_ref, wT_ref, *, q_scale, w_scale):
    qT_ref[...] = (_x_dot_wT(x_ref, wq_ref) * q_scale).T.astype(qT_ref.dtype)
    kw = _x_dot_wT(x_ref, wkw_ref)
    k_ref[...] = kw.astype(k_ref.dtype)
    wT_ref[...] = (kw * w_scale).T


def _project_indexer(xb, wT, row_q, row_kw, *, q_scale, w_scale):
    S, K = xb.shape
    tm = min(PROJ_TM, S)
    WI = N_IDX_HEADS * IDX_HEAD_DIM
    return pl.pallas_call(
        functools.partial(_proj_indexer_kernel, q_scale=q_scale, w_scale=w_scale),
        grid=(S // tm, 1),
        in_specs=[pl.BlockSpec((tm, K), lambda i, j: (i, 0)),
                  _w_rows_spec(WI, K, lambda j: row_q),
                  _w_rows_spec(IDX_BLOCK, K, lambda j: row_kw)],
        out_specs=[pl.BlockSpec((WI, tm), lambda i, j: (0, i)),
                   pl.BlockSpec((tm, IDX_BLOCK), lambda i, j: (i, 0)),
                   pl.BlockSpec((IDX_BLOCK, tm), lambda i, j: (0, i))],
        out_shape=[jax.ShapeDtypeStruct((WI, S), BF16),
                   jax.ShapeDtypeStruct((S, IDX_BLOCK), BF16),
                   jax.ShapeDtypeStruct((IDX_BLOCK, S), F32)],
        compiler_params=_cparams("parallel", "arbitrary"),
        name="proj_indexer",
    )(xb, wT, wT)


def _proj_dilated_kernel(x_ref, wq_ref, wk_ref, wv_ref, o_ref, scr_ref, *, dilation):
    for which, w_ref in enumerate((wq_ref, wk_ref, wv_ref)):
        acc = _x_dot_wT(x_ref, w_ref)
        if which == 0:
            acc = acc * (LOG2_E * HEAD_DIM ** -0.5)
        tm, tn = acc.shape
        if dilation == 1:
            o_ref[which, 0] = acc.astype(o_ref.dtype)
            continue
        for c in range(tn // LANES):
            scr_ref[which, c] = acc[:, c * LANES:(c + 1) * LANES]
        for r in range(dilation):
            for c in range(tn // LANES):
                rows = scr_ref[which, c, pl.ds(r, tm // dilation, stride=dilation), :]
                o_ref[which, r, :, c * LANES:(c + 1) * LANES] = rows.astype(o_ref.dtype)


def _project_dilated_qkv(xb, wT, row0, g, dilation):
    S, K = xb.shape
    d = dilation
    tm, tn = min(PROJ_TM, S), DIL_OUT_WIDTH
    assert S % tm == 0 and tm % (d * ROW_ALIGN) == 0

    def w_spec(which):
        return _w_rows_spec(tn, K, lambda j: row0 + which * DIL_WIDTH + g * tn)

    return pl.pallas_call(
        functools.partial(_proj_dilated_kernel, dilation=d),
        grid=(S // tm, 1),
        in_specs=[pl.BlockSpec((tm, K), lambda i, j: (i, 0)), w_spec(0), w_spec(1), w_spec(2)],
        out_specs=pl.BlockSpec((3, d, tm // d, tn), lambda i, j: (0, 0, i, 0)),
        out_shape=jax.ShapeDtypeStruct((3, d, S // d, tn), BF16),
        scratch_shapes=[pltpu.VMEM((3, tn // LANES, tm, LANES), F32)],
        compiler_params=_cparams("parallel", "arbitrary"),
        name=f"proj_dil_qkv_d{d}",
    )(xb, wT, wT, wT)


DSA_QB = 256
DSA_KC = 256
SEARCH_MAX_ITERS = 512
SEARCH_BISECT_EVERY = 4
SEARCH_FIRST_STEP = 0.0625
PEEL_AFTER = 5
PEEL_NEED = 2
ATTN_SLAB = 16
ATTN_HEAD_GROUP = 8
IDX_PER_ATTN_HEAD = N_IDX_HEADS // N_SPARSE_HEADS
IDX_IN_SOFTMAX = 2
MOMENT_ROWS = 64
COUNT_SLAB = 16
COUNT_ACCUMULATORS = 2


def _dsa_kernel(qi_tab, kj_tab,
                qT_ref, k_ref, vT_ref, qiT_ref, qiT_next_ref, ki_ref, wT_ref, wT_next_ref,
                pos_ref, srow_ref, zq_ref,
                o_ref,
                sc2_ref, thr_ref, m_ref, l_ref, acc_ref, s_buf, p_buf, mask_buf, mom_ref, *, topk):
    QB, KC = DSA_QB, DSA_KC
    t = pl.program_id(0)
    i = qi_tab[t]
    j = kj_tab[t]
    cur = i % 2
    sc_ref = sc2_ref.at[cur]

    @pl.when(t == 0)
    def _first():
        mom_ref[...] = jnp.zeros(mom_ref.shape, F32)

    def score_heads(acc, kic, q_ref, w_ref, heads):
        for h in heads:
            r = jnp.dot(kic, q_ref[h * IDX_HEAD_DIM:(h + 1) * IDX_HEAD_DIM, :],
                        preferred_element_type=F32)
            w_h = w_ref[IDX_HEAD_DIM + h:IDX_HEAD_DIM + h + 1, :]
            acc = acc + w_h * jnp.maximum(r, 0.0)
        return acc

    def chunk_keys(c):
        row0 = pl.multiple_of(c * KC, KC)
        return ki_ref[pl.ds(row0, KC), 0:IDX_HEAD_DIM]

    def chunk_scores(q_ref, w_ref, dst_ref, c, diagonal):
        acc = score_heads(jnp.zeros((KC, QB), F32), chunk_keys(c), q_ref, w_ref,
                          range(N_IDX_HEADS))
        return finish_scores(acc, dst_ref, c, diagonal)

    def finish_scores(acc, dst_ref, c, diagonal):
        row0 = pl.multiple_of(c * KC, KC)
        if diagonal:
            kpos = lax.broadcasted_iota(jnp.int32, (KC, QB), 0)
            qpos = lax.broadcasted_iota(jnp.int32, (KC, QB), 1)
            causal = kpos <= qpos
            live = jnp.where(causal, acc, 0.0)
            acc = jnp.where(causal, acc, -jnp.inf)
        else:
            live = acc
        dst_ref[pl.ds(row0, KC), :] = acc
        live = live[:MOMENT_ROWS, :]
        s1 = jnp.sum(live.reshape(MOMENT_ROWS // 8, 8, QB), axis=0)
        s2 = jnp.sum((live * live).reshape(MOMENT_ROWS // 8, 8, QB), axis=0)
        return s1, s2

    def count_ge(cand):
        candb = jnp.broadcast_to(cand, (COUNT_SLAB, QB))

        def body(c, parts):
            row0 = pl.multiple_of(c * (2 * KC), 2 * KC)
            parts = list(parts)
            for r in range(2 * KC // COUNT_SLAB):
                slab = sc_ref[pl.ds(row0 + r * COUNT_SLAB, COUNT_SLAB), :]
                parts[r % len(parts)] = parts[r % len(parts)] + jnp.where(slab >= candb, 1.0, 0.0)
            return tuple(parts)
        zero = jnp.zeros((COUNT_SLAB, QB), F32)
        parts = lax.fori_loop(0, (i + 2) // 2, body, (zero,) * COUNT_ACCUMULATORS)
        return jnp.sum(sum(parts), axis=0, keepdims=True)

    def max_below(bound):
        boundb = jnp.broadcast_to(bound, (COUNT_SLAB, QB))

        def body(c, parts):
            row0 = pl.multiple_of(c * (2 * KC), 2 * KC)
            parts = list(parts)
            for r in range(2 * KC // COUNT_SLAB):
                slab = sc_ref[pl.ds(row0 + r * COUNT_SLAB, COUNT_SLAB), :]
                parts[r % len(parts)] = jnp.maximum(parts[r % len(parts)],
                                                    jnp.where(slab < boundb, slab, -jnp.inf))
            return tuple(parts)
        lowest = jnp.full((COUNT_SLAB, QB), -jnp.inf, F32)
        parts = lax.fori_loop(0, (i + 2) // 2, body, (lowest,) * COUNT_ACCUMULATORS)
        return jnp.max(functools.reduce(jnp.maximum, parts), axis=0, keepdims=True)

    @pl.when(j == 0)
    def _prepare():
        d1, d2 = chunk_scores(qiT_ref, wT_ref, sc_ref, i, True)
        s1 = jnp.sum(mom_ref[cur, 0] + d1, axis=0, keepdims=True)
        s2 = jnp.sum(mom_ref[cur, 1] + d2, axis=0, keepdims=True)
        mom_ref[1 - cur] = jnp.zeros(mom_ref.shape[1:], F32)

        @pl.when(i % 2 == 0)
        def _pad():
            sc_ref[pl.ds(pl.multiple_of((i + 1) * KC, KC), KC), :] = jnp.full((KC, QB), -jnp.inf, F32)

        kf = float(topk)
        log_k = float(np.log(topk - 0.5))
        lane = lax.broadcasted_iota(jnp.int32, (1, QB), 1)
        n_causal = (i * QB + 1 + lane).astype(F32)
        n_sample = (i * MOMENT_ROWS + jnp.minimum(lane + 1, MOMENT_ROWS)).astype(F32)
        mean = s1 / n_sample
        std = jnp.sqrt(jnp.maximum(s2 / n_sample - mean * mean, 0.0))
        zq = jnp.max(zq_ref[...], axis=0, keepdims=True)
        guess = mean + zq * std
        few = n_causal <= kf
        lo = jnp.full((1, QB), F32_LOWEST, F32)
        clo = n_causal
        hi = jnp.full((1, QB), F32_HIGHEST, F32)
        chi = jnp.zeros((1, QB), F32)
        done_f = jnp.where(few | (clo == kf), 1.0, 0.0)
        zero = jnp.zeros((1, QB), F32)
        one = jnp.ones((1, QB), F32)

        def all_done(done_f):
            return jnp.min(done_f) > 0.5

        def peel_ready(st):
            it, hi, chi, done_f = st[0], st[2], st[4], st[5]
            few_left = (kf - chi <= PEEL_NEED) & (hi < F32_HIGHEST)
            ok = jnp.min(jnp.where(few_left, 1.0, done_f)) > 0.5
            return jnp.logical_and(it >= PEEL_AFTER, ok)

        def cond(st):
            go = jnp.logical_and(st[0] < SEARCH_MAX_ITERS, jnp.logical_not(all_done(st[5])))
            return jnp.logical_and(go, jnp.logical_not(peel_ready(st)))

        def cond_no_peel(st):
            return jnp.logical_and(st[0] < SEARCH_MAX_ITERS, jnp.logical_not(all_done(st[5])))

        def step(st):
            it, lo, hi, clo, chi, done_f, n_up, n_dn, w_lo, w_hi, last = st
            done = done_f > 0.5
            mid = 0.5 * lo + 0.5 * hi
            f_lo = (jnp.log(clo) - log_k) * w_lo
            f_hi = (log_k - jnp.log(jnp.maximum(chi, 0.5))) * w_hi
            interp = lo + (hi - lo) * (f_lo / jnp.maximum(f_lo + f_hi, 1e-9))
            open_hi = hi >= F32_HIGHEST
            open_lo = lo <= F32_LOWEST
            cand = jnp.where(it % SEARCH_BISECT_EVERY == SEARCH_BISECT_EVERY - 1, mid, interp)
            cand = jnp.where(open_hi, lo + std * (SEARCH_FIRST_STEP * jnp.exp2(n_up)), cand)
            cand = jnp.where(open_lo, hi - std * (SEARCH_FIRST_STEP * jnp.exp2(n_dn)), cand)
            cand = jnp.where(open_lo & open_hi, guess, cand)
            bad = jnp.logical_not((cand > lo) & (cand < hi))
            cand = jnp.where(bad, mid, cand)
            cand = jnp.where(done, lo, cand)
            cnt = count_ge(cand)
            up = jnp.logical_not(done) & (cnt >= kf)
            dn = jnp.logical_not(done) & (cnt < kf)
            n_up = n_up + jnp.where(up & open_hi, 1.0, 0.0)
            n_dn = n_dn + jnp.where(dn & open_lo, 1.0, 0.0)
            w_hi = jnp.where(up, jnp.where(last == 1.0, 0.5 * w_hi, 1.0), jnp.where(dn, 1.0, w_hi))
            w_lo = jnp.where(dn, jnp.where(last == 2.0, 0.5 * w_lo, 1.0), jnp.where(up, 1.0, w_lo))
            last = jnp.where(up, 1.0, jnp.where(dn, 2.0, last))
            lo = jnp.where(up, cand, lo)
            clo = jnp.where(up, cnt, clo)
            hi = jnp.where(dn, cand, hi)
            chi = jnp.where(dn, cnt, chi)
            mid2 = 0.5 * lo + 0.5 * hi
            stuck = jnp.logical_not((mid2 > lo) & (mid2 < hi))
            done_f = jnp.where(done | (clo == kf) | stuck, 1.0, 0.0)
            return it + 1, lo, hi, clo, chi, done_f, n_up, n_dn, w_lo, w_hi, last

        def peel_cond(ps):
            return jnp.logical_and(ps[0] < PEEL_NEED + 2, jnp.logical_not(all_done(ps[5])))

        def peel_step(ps):
            n, lo, hi, clo, chi, done_f = ps
            done = done_f > 0.5
            y = max_below(hi)
            cand = jnp.where(done, lo, y)
            cnt = count_ge(cand)
            up = jnp.logical_not(done) & (cnt >= kf)
            dn = jnp.logical_not(done) & (cnt < kf)
            lo = jnp.where(up, cand, lo)
            clo = jnp.where(up, cnt, clo)
            hi = jnp.where(dn, cand, hi)
            chi = jnp.where(dn, cnt, chi)
            return n + 1, lo, hi, clo, chi, jnp.where(done | up, 1.0, 0.0)

        st = lax.while_loop(cond, step, (jnp.int32(0), lo, hi, clo, chi, done_f,
                                         zero, zero, one, one, zero))
        ps = lax.while_loop(peel_cond, peel_step, (jnp.int32(0),) + tuple(st[1:6]))
        st = lax.while_loop(cond_no_peel, step, (st[0],) + tuple(ps[1:6]) + tuple(st[6:]))
        lo, hi, clo, chi = st[1], st[2], st[3], st[4]
        lo = jnp.where(few, F32_LOWEST, lo)

        need = kf - chi
        tied = jnp.logical_and(clo > kf, jnp.logical_not(few))

        @pl.when(jnp.max(tied.astype(F32)) > 0.5)
        def _break_ties():
            def count_eq_le(cut):
                def body(c, part):
                    row0 = pl.multiple_of(c * KC, KC)
                    tile = sc_ref[pl.ds(row0, KC), :]
                    kpos = (row0 + lax.broadcasted_iota(jnp.int32, (KC, QB), 0)).astype(F32)
                    ind = jnp.where((tile == lo) & (kpos <= cut), 1.0, 0.0)
                    return part + jnp.sum(ind.reshape(KC // 8, 8, QB), axis=0)
                part = lax.fori_loop(0, i + 1, body, jnp.zeros((8, QB), F32))
                return jnp.sum(part, axis=0, keepdims=True)

            def bis(_, st):
                a, b = st
                mid = jnp.floor(0.5 * (a + b))
                ok = count_eq_le(mid) <= need
                return jnp.where(ok, mid, a), jnp.where(ok, b, mid)

            a0 = jnp.full((1, QB), -1.0, F32)
            b0 = ((i + 1) * KC).astype(F32) + jnp.zeros((1, QB), F32)
            n_bits = int(np.ceil(np.log2(sc_ref.shape[0] + 2))) + 1
            cut, _ = lax.fori_loop(0, n_bits, bis, (a0, b0))

            def drop(c, _):
                row0 = pl.multiple_of(c * KC, KC)
                tile = sc_ref[pl.ds(row0, KC), :]
                kpos = (row0 + lax.broadcasted_iota(jnp.int32, (KC, QB), 0)).astype(F32)
                kill = tied & (tile == lo) & (kpos > cut)
                sc_ref[pl.ds(row0, KC), :] = jnp.where(kill, -jnp.inf, tile)
                return 0
            lax.fori_loop(0, i + 1, drop, 0)

        thr_ref[...] = jnp.broadcast_to(lo, thr_ref.shape)
        m_ref[...] = jnp.full(m_ref.shape, NEG_BIG, F32)
        l_ref[...] = jnp.zeros(l_ref.shape, F32)
        acc_ref[...] = jnp.zeros(acc_ref.shape, F32)

    row0 = pl.multiple_of(j * KC, KC)
    SL = ATTN_SLAB
    n_slab = KC // SL
    thr = jnp.broadcast_to(thr_ref[0:1, :], (SL, QB))
    for r in range(n_slab):
        rows = slice(r * SL, (r + 1) * SL)
        sc = sc_ref[pl.ds(row0 + r * SL, SL), :]
        mask_buf[rows, :] = jnp.where(sc >= thr, 0.0, NEG_BIG)
    posk = pos_ref[...]
    nxt_keys = chunk_keys(j)
    nxt_acc = jnp.zeros((KC, QB), F32)
    for g0 in range(0, N_SPARSE_HEADS, ATTN_HEAD_GROUP):
        heads = range(g0, g0 + ATTN_HEAD_GROUP)
        col_max = {}
        for h in heads:
            hs = slice(h * HEAD_DIM, (h + 1) * HEAD_DIM)
            lhs = jnp.concatenate([k_ref[:, hs], posk], axis=1)
            rhs = jnp.concatenate([qT_ref[hs, :], srow_ref[h]], axis=0)
            s = jnp.dot(lhs, rhs, preferred_element_type=F32)
            cm = [jnp.full((SL, QB), NEG_BIG, F32)] * 2
            for r in range(n_slab):
                rows = slice(r * SL, (r + 1) * SL)
                blk = s[rows, :] + mask_buf[rows, :]
                s_buf[h, rows, :] = blk
                cm[r % 2] = jnp.maximum(cm[r % 2], blk)
            col_max[h] = jnp.max(jnp.maximum(cm[0], cm[1]), axis=0, keepdims=True)
        alphas = {}
        for h in heads:
            nxt_acc = score_heads(nxt_acc, nxt_keys, qiT_next_ref, wT_next_ref,
                                  range(h * IDX_PER_ATTN_HEAD, h * IDX_PER_ATTN_HEAD + IDX_IN_SOFTMAX))
            m_old = m_ref[h]
            m_new = jnp.maximum(m_old, col_max[h])
            alphas[h] = jnp.exp2(m_old - m_new)
            m_ref[h] = m_new
            mb = jnp.concatenate([m_new] * (SL // 8), axis=0)
            for r in range(n_slab):
                rows = slice(r * SL, (r + 1) * SL)
                p_buf[h, rows, :] = jnp.exp2(s_buf[h, rows, :] - mb).astype(BF16)
        ones_rows = jnp.ones((ROW_ALIGN, KC), BF16)
        for h in heads:
            nxt_acc = score_heads(nxt_acc, nxt_keys, qiT_next_ref, wT_next_ref,
                                  range(h * IDX_PER_ATTN_HEAD + IDX_IN_SOFTMAX,
                                        (h + 1) * IDX_PER_ATTN_HEAD))
            hs = slice(h * HEAD_DIM, (h + 1) * HEAD_DIM)
            v_ext = jnp.concatenate([vT_ref[hs, :], ones_rows], axis=0)
            pv = jnp.dot(v_ext, p_buf[h], preferred_element_type=F32)
            acc_ref[hs, :] = alphas[h][0:1, :] * acc_ref[hs, :] + pv[:HEAD_DIM, :]
            l_ref[h] = alphas[h] * l_ref[h] + pv[HEAD_DIM:HEAD_DIM + 1, :]

    n1, n2 = finish_scores(nxt_acc, sc2_ref.at[1 - cur], j, False)
    mom_ref[1 - cur, 0] += n1
    mom_ref[1 - cur, 1] += n2

    @pl.when(j == i)
    def _finish():
        for h in range(N_SPARSE_HEADS):
            hs = slice(h * HEAD_DIM, (h + 1) * HEAD_DIM)
            o = acc_ref[hs, :] / l_ref[h, 0:1, :]
            o_ref[:, hs] = o.T.astype(o_ref.dtype)


N_SLOPE_PIECES = 3
POS_SPLIT = 128


def _alibi_mxu_operands(S, slopes_log2):
    assert S <= POS_SPLIT * 256
    kpos = jnp.arange(S, dtype=jnp.int32)[:, None]
    col = jnp.arange(128, dtype=jnp.int32)[None, :]
    part = jnp.where(col % 2 == 0, kpos // POS_SPLIT, kpos % POS_SPLIT)
    pos = jnp.where(col < 2 * N_SLOPE_PIECES, part, 0).astype(BF16)
    rows = np.zeros((len(slopes_log2), 128), np.float32)
    for h, s in enumerate(slopes_log2):
        rest = np.float64(s)
        for p in range(N_SLOPE_PIECES):
            piece = np.float64(np.asarray(rest, np.float32).astype(jnp.bfloat16).astype(np.float32))
            rows[h, 2 * p] = POS_SPLIT * piece
            rows[h, 2 * p + 1] = piece
            rest = rest - piece
    srow = jnp.broadcast_to(jnp.asarray(rows, F32)[:, :, None],
                            (len(slopes_log2), 128, DSA_QB)).astype(BF16)
    return pos, srow


def _dsa_attention(qT, k, vT, qiT, ki, wT, slopes, *, topk):
    S = k.shape[0]
    QB, KC = DSA_QB, DSA_KC
    assert S % (2 * QB) == 0 and QB == KC
    nq = S // QB
    pos, srow = _alibi_mxu_operands(S, [s * LOG2_E for s in slopes])
    quantile = statistics.NormalDist().inv_cdf
    zq = np.array([quantile(1.0 - (topk - 0.5) / n) if n > topk else 0.0
                   for n in range(1, S + 1)], np.float32)
    zq = np.broadcast_to(zq, (8, S))
    qi_tab = np.concatenate([np.full(i + 1, i, np.int32) for i in range(nq)])
    kj_tab = np.concatenate([np.arange(i + 1, dtype=np.int32) for i in range(nq)])
    n_steps = int(qi_tab.shape[0])
    W = N_SPARSE_HEADS * HEAD_DIM
    WI = N_IDX_HEADS * IDX_HEAD_DIM
    grid_spec = pltpu.PrefetchScalarGridSpec(
        num_scalar_prefetch=2,
        grid=(n_steps,),
        in_specs=[
            pl.BlockSpec((W, QB), lambda t, qi, kj: (0, qi[t])),
            pl.BlockSpec((KC, W), lambda t, qi, kj: (kj[t], 0)),
            pl.BlockSpec((W, KC), lambda t, qi, kj: (0, kj[t])),
            pl.BlockSpec((WI, QB), lambda t, qi, kj: (0, qi[t])),
            pl.BlockSpec((WI, QB), lambda t, qi, kj: (0, jnp.minimum(qi[t] + 1, nq - 1))),
            pl.BlockSpec((S, IDX_BLOCK), lambda t, qi, kj: (0, 0)),
            pl.BlockSpec((IDX_BLOCK, QB), lambda t, qi, kj: (0, qi[t])),
            pl.BlockSpec((IDX_BLOCK, QB), lambda t, qi, kj: (0, jnp.minimum(qi[t] + 1, nq - 1))),
            pl.BlockSpec((KC, 128), lambda t, qi, kj: (kj[t], 0)),
            pl.BlockSpec(srow.shape, lambda t, qi, kj: (0, 0, 0)),
            pl.BlockSpec((8, QB), lambda t, qi, kj: (0, qi[t])),
        ],
        out_specs=pl.BlockSpec((QB, W), lambda t, qi, kj: (qi[t], 0)),
        scratch_shapes=[
            pltpu.VMEM((2, S, QB), F32),
            pltpu.VMEM((8, QB), F32),
            pltpu.VMEM((N_SPARSE_HEADS, 8, QB), F32),
            pltpu.VMEM((N_SPARSE_HEADS, 8, QB), F32),
            pltpu.VMEM((W, QB), F32),
            pltpu.VMEM((N_SPARSE_HEADS, KC, QB), F32),
            pltpu.VMEM((N_SPARSE_HEADS, KC, QB), BF16),
            pltpu.VMEM((KC, QB), F32),
            pltpu.VMEM((2, 2, 8, QB), F32),
        ],
    )
    return pl.pallas_call(
        functools.partial(_dsa_kernel, topk=topk),
        grid_spec=grid_spec,
        out_shape=jax.ShapeDtypeStruct((S, W), BF16),
        compiler_params=_cparams("arbitrary"),
        name="dsa_attention",
    )(jnp.asarray(qi_tab), jnp.asarray(kj_tab), qT, k, vT, qiT, qiT, ki, wT, wT, pos, srow,
      jnp.asarray(zq))


DIL_BLOCKS_PER_STEP = 4
def _dil_kernel(bias_ref, q_ref, kc_ref, kp_ref, vc_ref, vp_ref, o_ref, lse_ref, *, n_blocks):
    first = jnp.where(pl.program_id(1) == 0, 1, 0)
    Q = BLOCK
    for b in range(n_blocks):
        rows = slice(b * Q, (b + 1) * Q)
        prev_rows = slice((b - 1) * Q, b * Q)
        for h in range(HEADS_PER_DIL_GROUP):
            hs = slice(h * HEAD_DIM, (h + 1) * HEAD_DIM)
            q = q_ref[rows, hs]
            k_prev = kp_ref[:, hs] if b == 0 else kc_ref[prev_rows, hs]
            v_prev = vp_ref[:, hs] if b == 0 else vc_ref[prev_rows, hs]
            kk = jnp.concatenate([k_prev, kc_ref[rows, hs]], axis=0)
            vv = jnp.concatenate([v_prev, vc_ref[rows, hs]], axis=0)
            bias = bias_ref[first, h] if b == 0 else bias_ref[0, h]
            s = lax.dot_general(q, kk, (((1,), (1,)), ((), ())),
                                preferred_element_type=F32) + bias
            mx = jnp.max(s, axis=-1, keepdims=True)
            p = jnp.exp2(s - mx)
            l = jnp.sum(p, axis=-1, keepdims=True)
            o = jnp.dot(p.astype(BF16), vv, preferred_element_type=F32) / l
            o_ref[rows, hs] = o
            lse_ref[rows, hs] = jnp.broadcast_to(mx + jnp.log2(l), (Q, HEAD_DIM))


def _dilated_bias(span, dilation, slopes_log2):
    Q = BLOCK
    step = Q + np.arange(Q)[:, None] - np.arange(2 * Q)[None, :]
    band = (step >= 0) & (step <= span)
    has_prev = np.stack([np.ones_like(band), np.broadcast_to(np.arange(2 * Q)[None, :] >= Q, band.shape)])
    dist = (step * dilation).astype(np.float64)
    bias = -np.asarray(slopes_log2, np.float64)[None, :, None, None] * dist[None, None]
    keep = (band[None] & has_prev)[:, None]
    return jnp.asarray(np.where(keep, bias, NEG_BIG), F32)


def _dilated_group(qkv, window, dilation, slopes_g):
    _, d, n_sub, GW = qkv.shape
    assert d == dilation
    span = window // d
    nbs = min(DIL_BLOCKS_PER_STEP, n_sub // BLOCK)
    rows = nbs * BLOCK
    assert span <= BLOCK and n_sub % rows == 0
    bias = _dilated_bias(span, d, [s * LOG2_E for s in slopes_g])

    def cur(which):
        return pl.BlockSpec((None, None, rows, GW), lambda r, m: (which, r, m, 0))

    def prev(which):
        return pl.BlockSpec((None, None, BLOCK, GW),
                            lambda r, m: (which, r, jnp.maximum(m * nbs - 1, 0), 0))

    out_spec = pl.BlockSpec((None, rows, GW), lambda r, m: (r, m, 0))
    return pl.pallas_call(
        functools.partial(_dil_kernel, n_blocks=nbs),
        grid=(d, n_sub // rows),
        in_specs=[pl.BlockSpec(bias.shape, lambda r, m: (0, 0, 0, 0)),
                  cur(0), cur(1), prev(1), cur(2), prev(2)],
        out_specs=[out_spec, out_spec],
        out_shape=[jax.ShapeDtypeStruct((d, n_sub, GW), F32)] * 2,
        compiler_params=_cparams("parallel", "arbitrary"),
        name=f"dilated_attention_d{d}",
    )(bias, qkv, qkv, qkv, qkv, qkv)


def _layer_norm(y, g, b):
    mu = jnp.mean(y, axis=-1, keepdims=True)
    yc = y - mu
    var = jnp.mean(yc * yc, axis=-1, keepdims=True)
    return yc * lax.rsqrt(var + LN_EPS) * g + b


def _merge_kernel(oa_ref, o1_ref, o2_ref, o3_ref, l1_ref, l2_ref, l3_ref,
                  ga_ref, gb_ref, x_ref, wa_ref, wb_ref, wo_ref, g_ref, b_ref,
                  h_ref, hb_ref, nat_ref):
    tm = x_ref.shape[0]
    n_slab = DIL_OUT_WIDTH // LANES

    def natural(ref, slot):
        d = ref.shape[0]
        if d == 1:
            return [ref[0, :, c * LANES:(c + 1) * LANES] for c in range(n_slab)]
        for r in range(d):
            for c in range(n_slab):
                nat_ref[slot, c, pl.ds(r, tm // d, stride=d), :] = ref[r, :, c * LANES:(c + 1) * LANES]
        return [nat_ref[slot, c] for c in range(n_slab)]

    o1, l1 = natural(o1_ref, 0), natural(l1_ref, 1)
    o2, l2 = natural(o2_ref, 2), natural(l2_ref, 3)
    o3, l3 = natural(o3_ref, 4), natural(l3_ref, 5)
    slabs = []
    for c in range(n_slab):
        lm = jnp.maximum(jnp.maximum(l1[c], l2[c]), l3[c])
        e1, e2, e3 = jnp.exp2(l1[c] - lm), jnp.exp2(l2[c] - lm), jnp.exp2(l3[c] - lm)
        den = e1 + e2 + e3
        slabs.append((e1 / den) * o1[c] + (e2 / den) * o2[c] + (e3 / den) * o3[c])
    ob = jnp.concatenate(slabs, axis=1)
    ya = jnp.dot(oa_ref[...], wa_ref[...], preferred_element_type=F32)
    yb = jnp.dot(ob.astype(BF16), wb_ref[...], preferred_element_type=F32)
    merged = ga_ref[...] * ya + gb_ref[...] * yb
    mix = jnp.dot(merged.astype(BF16), wo_ref[...], preferred_element_type=F32)
    h = _layer_norm(DEEPNORM_ALPHA * x_ref[...] + mix, g_ref[...], b_ref[...])
    h_ref[...] = h
    hb_ref[...] = h.astype(BF16)


MERGE_TM = 256


def _merge(oa, dil, gates, x2, wa, wb, wo, ln_g, ln_b):
    S = x2.shape[0]
    tm = MERGE_TM
    (o1, l1), (o2, l2), (o3, l3) = dil
    GW = DIL_OUT_WIDTH

    def rows(w, col=0):
        return pl.BlockSpec((tm, w), lambda i: (i, col))

    def residue_major(a):
        d = a.shape[0]
        assert tm % (8 * d) == 0
        return pl.BlockSpec((d, tm // d, GW), lambda i: (0, i, 0))

    def whole(a):
        return pl.BlockSpec(a.shape, lambda i: (0,) * a.ndim, pipeline_mode=pl.Buffered(1))

    return pl.pallas_call(
        _merge_kernel,
        grid=(S // tm,),
        in_specs=[rows(SPARSE_WIDTH),
                  residue_major(o1), residue_major(o2), residue_major(o3),
                  residue_major(l1), residue_major(l2), residue_major(l3),
                  rows(D_MODEL, 0), rows(D_MODEL, 1), rows(D_MODEL),
                  whole(wa), whole(wb), whole(wo), whole(ln_g), whole(ln_b)],
        out_specs=[rows(D_MODEL), rows(D_MODEL)],
        out_shape=[jax.ShapeDtypeStruct((S, D_MODEL), F32),
                   jax.ShapeDtypeStruct((S, D_MODEL), BF16)],
        scratch_shapes=[pltpu.VMEM((6, GW // LANES, tm, LANES), F32)],
        compiler_params=_cparams("parallel"),
        name="merge_outproj_ln",
    )(oa, o1, o2, o3, l1, l2, l3, gates, gates, x2, wa, wb, wo, ln_g, ln_b)


FFN_TM = 1024
FFN_UP_TN = 512
FFN_DOWN_TN = 256


def _ffn_up_kernel(hb_ref, wg_ref, wu_ref, a_ref):
    hb = hb_ref[...]
    gate = jnp.dot(hb, wg_ref[...].astype(BF16), preferred_element_type=F32)
    up = jnp.dot(hb, wu_ref[...].astype(BF16), preferred_element_type=F32)
    a_ref[...] = (gate * jax.nn.sigmoid(gate) * up).astype(a_ref.dtype)


def _ffn_down_kernel(a_ref, h_ref, wd_ref, g_ref, b_ref, o_ref):
    j = pl.program_id(1)
    tn = wd_ref.shape[1]
    cols = pl.ds(pl.multiple_of(j * tn, tn), tn)
    o_ref[:, cols] = jnp.dot(a_ref[...], wd_ref[...].astype(BF16), preferred_element_type=F32)

    @pl.when(j == pl.num_programs(1) - 1)
    def _():
        o_ref[...] = _layer_norm(DEEPNORM_ALPHA * h_ref[...] + o_ref[...],
                                 g_ref[...], b_ref[...])


def _ffn(hb, h, wg, wu, wd, ln_g, ln_b):
    S = h.shape[0]
    F = wg.shape[1]
    tm = min(FFN_TM, S)
    assert S % tm == 0 and F % FFN_UP_TN == 0 and D_MODEL % FFN_DOWN_TN == 0
    act = pl.pallas_call(
        _ffn_up_kernel,
        grid=(S // tm, F // FFN_UP_TN),
        in_specs=[pl.BlockSpec((tm, D_MODEL), lambda i, j: (i, 0)),
                  pl.BlockSpec((D_MODEL, FFN_UP_TN), lambda i, j: (0, j)),
                  pl.BlockSpec((D_MODEL, FFN_UP_TN), lambda i, j: (0, j))],
        out_specs=pl.BlockSpec((tm, FFN_UP_TN), lambda i, j: (i, j)),
        out_shape=jax.ShapeDtypeStruct((S, F), BF16),
        compiler_params=_cparams("parallel", "arbitrary"),
        name="ffn_gate_up",
    )(hb, wg, wu)

    def per_row_tile(width):
        return pl.BlockSpec((tm, width), lambda i, j: (i, 0), pipeline_mode=pl.Buffered(1))

    return pl.pallas_call(
        _ffn_down_kernel,
        grid=(S // tm, D_MODEL // FFN_DOWN_TN),
        in_specs=[per_row_tile(F), per_row_tile(D_MODEL),
                  pl.BlockSpec((F, FFN_DOWN_TN), lambda i, j: (0, j)),
                  pl.BlockSpec((1, D_MODEL), lambda i, j: (0, 0)),
                  pl.BlockSpec((1, D_MODEL), lambda i, j: (0, 0))],
        out_specs=pl.BlockSpec((tm, D_MODEL), lambda i, j: (i, 0)),
        out_shape=jax.ShapeDtypeStruct((S, D_MODEL), F32),
        compiler_params=_cparams("parallel", "arbitrary"),
        name="ffn_down_ln",
    )(act, h, wd, ln_g, ln_b)


def _mixer_inputs(x2, w_in):
    O = IN_OFFS
    wT = jnp.transpose(w_in)
    k, xb = _project_and_cast(x2, wT, O[1], SPARSE_WIDTH, name="proj_ak")
    qT = _project(xb, wT, O[0], SPARSE_WIDTH, out_dtype=BF16,
                  scale=LOG2_E * HEAD_DIM ** -0.5, transpose_out=True, name="proj_aqT")
    vT = _project(xb, wT, O[2], SPARSE_WIDTH, out_dtype=BF16, transpose_out=True,
                  name="proj_avT")
    qiT, ki, iwT = _project_indexer(xb, wT, O[3], O[4], q_scale=IDX_HEAD_DIM ** -0.5,
                                    w_scale=N_IDX_HEADS ** -0.5)
    dil_qkv = [_project_dilated_qkv(xb, wT, O[6], g, d) for g, (_, d) in enumerate(DIL_GROUPS)]
    gates = _project(xb, wT, O[9], 2 * D_MODEL, out_dtype=F32, sigmoid=True, name="proj_gates")
    return qT, k, vT, qiT, ki, iwT, dil_qkv, gates


def _layer(x2, w_in, w_a, w_b, w_out, ln1_g, ln1_b, w_gate, w_up, w_down, ln2_g, ln2_b):
    S = x2.shape[0]
    slopes = _alibi_slopes()
    qT, k, vT, qiT, ki, iwT, dil_qkv, gates = _mixer_inputs(x2, w_in)

    topk = min(TOPK_MAX, S // 4)
    oa = _dsa_attention(qT, k, vT, qiT, ki, iwT, slopes[N_DIL_HEADS:], topk=topk)

    dil = []
    for g, (win, d) in enumerate(DIL_GROUPS):
        sl = slopes[g * HEADS_PER_DIL_GROUP:(g + 1) * HEADS_PER_DIL_GROUP]
        dil.append(_dilated_group(dil_qkv[g], win, d, sl))

    h, hb = _merge(oa, dil, gates, x2, w_a.astype(BF16), w_b.astype(BF16), w_out.astype(BF16),
                   ln1_g.reshape(1, -1), ln1_b.reshape(1, -1))
    return _ffn(hb, h, w_gate, w_up, w_down, ln2_g.reshape(1, -1), ln2_b.reshape(1, -1))


def kernel(x, w_in, w_a, w_b, w_out, ln1_g, ln1_b, w_gate, w_up, w_down, ln2_g, ln2_b):
    B, S, D = x.shape
    outs = []
    for b in range(B):
        y = x[b]
        for l in range(w_in.shape[0]):
            y = _layer(y, w_in[l], w_a[l], w_b[l], w_out[l], ln1_g[l], ln1_b[l],
                       w_gate[l], w_up[l], w_down[l], ln2_g[l], ln2_b[l])
        outs.append(y)
    return jnp.stack(outs)
```

```python
import functools
import statistics

import numpy as np
import jax
import jax.numpy as jnp
from jax import lax
from jax.experimental import pallas as pl
from jax.experimental.pallas import tpu as pltpu

F32 = jnp.float32
BF16 = jnp.bfloat16

D_MODEL = 2048
HEAD_DIM = 128
N_SPARSE_HEADS = 8
SPARSE_WIDTH = N_SPARSE_HEADS * HEAD_DIM
N_IDX_HEADS = 16
IDX_HEAD_DIM = 64
TOPK_MAX = 256
DIL_GROUPS = ((128, 1), (512, 4), (2048, 16))
HEADS_PER_DIL_GROUP = 4
N_DIL_HEADS = HEADS_PER_DIL_GROUP * len(DIL_GROUPS)
DIL_WIDTH = N_DIL_HEADS * HEAD_DIM
DIL_OUT_WIDTH = HEADS_PER_DIL_GROUP * HEAD_DIM
N_ALIBI_HEADS = N_DIL_HEADS + N_SPARSE_HEADS
BLOCK = 128
D_FF = ((-(-8 * D_MODEL // 3)) + 255) // 256 * 256
DEPTH = 1
DEEPNORM_ALPHA = (2 * DEPTH) ** 0.25
LN_EPS = 1e-5
IN_SIZES = (SPARSE_WIDTH, SPARSE_WIDTH, SPARSE_WIDTH,
            N_IDX_HEADS * IDX_HEAD_DIM, IDX_HEAD_DIM, N_IDX_HEADS,
            DIL_WIDTH, DIL_WIDTH, DIL_WIDTH, D_MODEL, D_MODEL)
IN_OFFS = tuple(int(c) for c in np.cumsum((0,) + IN_SIZES))

VMEM_LIMIT_BYTES = 56 * 1024 * 1024
LOG2_E = float(np.log2(np.e))
NEG_BIG = -1e30
F32_LOWEST = float(np.finfo(np.float32).min)
F32_HIGHEST = float(np.finfo(np.float32).max)


def _alibi_slopes():
    return [2.0 ** (-8.0 * (i + 1) / N_ALIBI_HEADS) for i in range(N_ALIBI_HEADS)]


def _cparams(*sem):
    return pltpu.CompilerParams(dimension_semantics=sem,
                                vmem_limit_bytes=VMEM_LIMIT_BYTES)


LANES = 128
ROW_ALIGN = 16
PROJ_TM = 1024
PROJ_TN = 1024
IDX_BLOCK = 128


def _x_dot_wT(x_ref, w_ref):
    return lax.dot_general(x_ref[...], w_ref[...].astype(BF16), (((1,), (1,)), ((), ())),
                           preferred_element_type=F32)


def _w_rows_spec(tn, K, row_of_step):
    return pl.BlockSpec((pl.Element(tn), pl.Element(K)),
                        lambda i, j: (pl.multiple_of(row_of_step(j), ROW_ALIGN), 0))


def _proj_kernel(x_ref, w_ref, o_ref, *, scale, sigmoid, transpose_out):
    acc = _x_dot_wT(x_ref, w_ref)
    if scale != 1.0:
        acc = acc * scale
    if sigmoid:
        acc = jax.nn.sigmoid(acc)
    if transpose_out:
        acc = acc.T
    o_ref[...] = acc.astype(o_ref.dtype)


def _project(xb, wT, row0, ncols, *, out_dtype, scale=1.0, sigmoid=False,
             transpose_out=False, name):
    S, K = xb.shape
    tm, tn = min(PROJ_TM, S), PROJ_TN
    assert row0 % ROW_ALIGN == 0 and ncols % tn == 0 and S % tm == 0
    if transpose_out:
        out_shape = jax.ShapeDtypeStruct((ncols, S), out_dtype)
        out_spec = pl.BlockSpec((tn, tm), lambda i, j: (j, i))
    else:
        out_shape = jax.ShapeDtypeStruct((S, ncols), out_dtype)
        out_spec = pl.BlockSpec((tm, tn), lambda i, j: (i, j))
    return pl.pallas_call(
        functools.partial(_proj_kernel, scale=scale, sigmoid=sigmoid,
                          transpose_out=transpose_out),
        grid=(S // tm, ncols // tn),
        in_specs=[pl.BlockSpec((tm, K), lambda i, j: (i, 0)),
                  _w_rows_spec(tn, K, lambda j: row0 + j * tn)],
        out_specs=out_spec,
        out_shape=out_shape,
        compiler_params=_cparams("parallel", "arbitrary"),
        name=name,
    )(xb, wT)


def _proj_cast_kernel(x_ref, w_ref, o_ref, xb_ref):
    xb_ref[...] = x_ref[...].astype(xb_ref.dtype)
    o_ref[...] = _x_dot_wT(xb_ref, w_ref).astype(o_ref.dtype)


def _project_and_cast(x2, wT, row0, ncols, *, name):
    S, K = x2.shape
    tm = min(PROJ_TM, S)
    assert ncols == PROJ_TN and S % tm == 0
    return pl.pallas_call(
        _proj_cast_kernel,
        grid=(S // tm, 1),
        in_specs=[pl.BlockSpec((tm, K), lambda i, j: (i, 0)),
                  _w_rows_spec(ncols, K, lambda j: row0)],
        out_specs=[pl.BlockSpec((tm, ncols), lambda i, j: (i, 0)),
                   pl.BlockSpec((tm, K), lambda i, j: (i, 0))],
        out_shape=[jax.ShapeDtypeStruct((S, ncols), BF16), jax.ShapeDtypeStruct((S, K), BF16)],
        compiler_params=_cparams("parallel", "arbitrary"),
        name=name,
    )(x2, wT)


def _proj_indexer_kernel(x_ref, wq_ref, wkw_ref, qT_ref, k_ref, wT_ref, *, q_scale, w_scale):
    qT_ref[...] = (_x_dot_wT(x_ref, wq_ref) * q_scale).T.astype(qT_ref.dtype)
    kw = _x_dot_wT(x_ref, wkw_ref)
    k_ref[...] = kw.astype(k_ref.dtype)
    wT_ref[...] = (kw * w_scale).T


def _project_indexer(xb, wT, row_q, row_kw, *, q_scale, w_scale):
    S, K = xb.shape
    tm = min(PROJ_TM, S)
    WI = N_IDX_HEADS * IDX_HEAD_DIM
    return pl.pallas_call(
        functools.partial(_proj_indexer_kernel, q_scale=q_scale, w_scale=w_scale),
        grid=(S // tm, 1),
        in_specs=[pl.BlockSpec((tm, K), lambda i, j: (i, 0)),
                  _w_rows_spec(WI, K, lambda j: row_q),
                  _w_rows_spec(IDX_BLOCK, K, lambda j: row_kw)],
        out_specs=[pl.BlockSpec((WI, tm), lambda i, j: (0, i)),
                   pl.BlockSpec((tm, IDX_BLOCK), lambda i, j: (i, 0)),
                   pl.BlockSpec((IDX_BLOCK, tm), lambda i, j: (0, i))],
        out_shape=[jax.ShapeDtypeStruct((WI, S), BF16),
                   jax.ShapeDtypeStruct((S, IDX_BLOCK), BF16),
                   jax.ShapeDtypeStruct((IDX_BLOCK, S), F32)],
        compiler_params=_cparams("parallel", "arbitrary"),
        name="proj_indexer",
    )(xb, wT, wT)


def _proj_dilated_kernel(x_ref, wq_ref, wk_ref, wv_ref, o_ref, scr_ref, *, dilation):
    for which, w_ref in enumerate((wq_ref, wk_ref, wv_ref)):
        acc = _x_dot_wT(x_ref, w_ref)
        if which == 0:
            acc = acc * (LOG2_E * HEAD_DIM ** -0.5)
        tm, tn = acc.shape
        if dilation == 1:
            o_ref[which, 0] = acc.astype(o_ref.dtype)
            continue
        for c in range(tn // LANES):
            scr_ref[which, c] = acc[:, c * LANES:(c + 1) * LANES]
        for r in range(dilation):
            for c in range(tn // LANES):
                rows = scr_ref[which, c, pl.ds(r, tm // dilation, stride=dilation), :]
                o_ref[which, r, :, c * LANES:(c + 1) * LANES] = rows.astype(o_ref.dtype)


def _project_dilated_qkv(xb, wT, row0, g, dilation):
    S, K = xb.shape
    d = dilation
    tm, tn = min(PROJ_TM, S), DIL_OUT_WIDTH
    assert S % tm == 0 and tm % (d * ROW_ALIGN) == 0

    def w_spec(which):
        return _w_rows_spec(tn, K, lambda j: row0 + which * DIL_WIDTH + g * tn)

    return pl.pallas_call(
        functools.partial(_proj_dilated_kernel, dilation=d),
        grid=(S // tm, 1),
        in_specs=[pl.BlockSpec((tm, K), lambda i, j: (i, 0)), w_spec(0), w_spec(1), w_spec(2)],
        out_specs=pl.BlockSpec((3, d, tm // d, tn), lambda i, j: (0, 0, i, 0)),
        out_shape=jax.ShapeDtypeStruct((3, d, S // d, tn), BF16),
        scratch_shapes=[pltpu.VMEM((3, tn // LANES, tm, LANES), F32)],
        compiler_params=_cparams("parallel", "arbitrary"),
        name=f"proj_dil_qkv_d{d}",
    )(xb, wT, wT, wT)


DSA_QB = 256
DSA_KC = 256
SEARCH_MAX_ITERS = 512
SEARCH_BISECT_EVERY = 8
SEARCH_FIRST_STEP = 0.0625
PEEL_AFTER = 5
PEEL_NEED = 2
ATTN_SLAB = 16
ATTN_HEAD_GROUP = 8
IDX_PER_ATTN_HEAD = N_IDX_HEADS // N_SPARSE_HEADS
IDX_IN_SOFTMAX = 2
MOMENT_ROWS = 64
COUNT_SLAB = 16
COUNT_ACCUMULATORS = 2


def _dsa_kernel(qi_tab, kj_tab,
                qT_ref, k_ref, vT_ref, qiT_ref, qiT_next_ref, ki_ref, wT_ref, wT_next_ref,
                pos_ref, srow_ref, zq_ref,
                o_ref,
                sc2_ref, thr_ref, m_ref, l_ref, acc_ref, s_buf, p_buf, mask_buf, mom_ref, *, topk):
    QB, KC = DSA_QB, DSA_KC
    t = pl.program_id(0)
    i = qi_tab[t]
    j = kj_tab[t]
    cur = i % 2
    sc_ref = sc2_ref.at[cur]

    @pl.when(t == 0)
    def _first():
        mom_ref[...] = jnp.zeros(mom_ref.shape, F32)

    def score_heads(acc, kic, q_ref, w_ref, heads):
        for h in heads:
            r = jnp.dot(kic, q_ref[h * IDX_HEAD_DIM:(h + 1) * IDX_HEAD_DIM, :],
                        preferred_element_type=F32)
            w_h = w_ref[IDX_HEAD_DIM + h:IDX_HEAD_DIM + h + 1, :]
            acc = acc + w_h * jnp.maximum(r, 0.0)
        return acc

    def chunk_keys(c):
        row0 = pl.multiple_of(c * KC, KC)
        return ki_ref[pl.ds(row0, KC), 0:IDX_HEAD_DIM]

    def chunk_scores(q_ref, w_ref, dst_ref, c, diagonal):
        acc = score_heads(jnp.zeros((KC, QB), F32), chunk_keys(c), q_ref, w_ref,
                          range(N_IDX_HEADS))
        return finish_scores(acc, dst_ref, c, diagonal)

    def finish_scores(acc, dst_ref, c, diagonal):
        row0 = pl.multiple_of(c * KC, KC)
        if diagonal:
            kpos = lax.broadcasted_iota(jnp.int32, (KC, QB), 0)
            qpos = lax.broadcasted_iota(jnp.int32, (KC, QB), 1)
            causal = kpos <= qpos
            live = jnp.where(causal, acc, 0.0)
            acc = jnp.where(causal, acc, -jnp.inf)
        else:
            live = acc
        dst_ref[pl.ds(row0, KC), :] = acc
        live = live[:MOMENT_ROWS, :]
        s1 = jnp.sum(live.reshape(MOMENT_ROWS // 8, 8, QB), axis=0)
        s2 = jnp.sum((live * live).reshape(MOMENT_ROWS // 8, 8, QB), axis=0)
        return s1, s2

    def count_ge(cand):
        candb = jnp.broadcast_to(cand, (COUNT_SLAB, QB))

        def body(c, parts):
            row0 = pl.multiple_of(c * (2 * KC), 2 * KC)
            parts = list(parts)
            for r in range(2 * KC // COUNT_SLAB):
                slab = sc_ref[pl.ds(row0 + r * COUNT_SLAB, COUNT_SLAB), :]
                parts[r % len(parts)] = parts[r % len(parts)] + jnp.where(slab >= candb, 1.0, 0.0)
            return tuple(parts)
        zero = jnp.zeros((COUNT_SLAB, QB), F32)
        parts = lax.fori_loop(0, (i + 2) // 2, body, (zero,) * COUNT_ACCUMULATORS)
        return jnp.sum(sum(parts), axis=0, keepdims=True)

    def max_below(bound):
        boundb = jnp.broadcast_to(bound, (COUNT_SLAB, QB))

        def body(c, parts):
            row0 = pl.multiple_of(c * (2 * KC), 2 * KC)
            parts = list(parts)
            for r in range(2 * KC // COUNT_SLAB):
                slab = sc_ref[pl.ds(row0 + r * COUNT_SLAB, COUNT_SLAB), :]
                parts[r % len(parts)] = jnp.maximum(parts[r % len(parts)],
                                                    jnp.where(slab < boundb, slab, -jnp.inf))
            return tuple(parts)
        lowest = jnp.full((COUNT_SLAB, QB), -jnp.inf, F32)
        parts = lax.fori_loop(0, (i + 2) // 2, body, (lowest,) * COUNT_ACCUMULATORS)
        return jnp.max(functools.reduce(jnp.maximum, parts), axis=0, keepdims=True)

    @pl.when(j == 0)
    def _prepare():
        d1, d2 = chunk_scores(qiT_ref, wT_ref, sc_ref, i, True)
        s1 = jnp.sum(mom_ref[cur, 0] + d1, axis=0, keepdims=True)
        s2 = jnp.sum(mom_ref[cur, 1] + d2, axis=0, keepdims=True)
        mom_ref[1 - cur] = jnp.zeros(mom_ref.shape[1:], F32)

        @pl.when(i % 2 == 0)
        def _pad():
            sc_ref[pl.ds(pl.multiple_of((i + 1) * KC, KC), KC), :] = jnp.full((KC, QB), -jnp.inf, F32)

        kf = float(topk)
        log_k = float(np.log(topk - 0.5))
        lane = lax.broadcasted_iota(jnp.int32, (1, QB), 1)
        n_causal = (i * QB + 1 + lane).astype(F32)
        n_sample = (i * MOMENT_ROWS + jnp.minimum(lane + 1, MOMENT_ROWS)).astype(F32)
        mean = s1 / n_sample
        std = jnp.sqrt(jnp.maximum(s2 / n_sample - mean * mean, 0.0))
        zq = jnp.max(zq_ref[...], axis=0, keepdims=True)
        guess = mean + zq * std
        few = n_causal <= kf
        lo = jnp.full((1, QB), F32_LOWEST, F32)
        clo = n_causal
        hi = jnp.full((1, QB), F32_HIGHEST, F32)
        chi = jnp.zeros((1, QB), F32)
        done_f = jnp.where(few | (clo == kf), 1.0, 0.0)
        zero = jnp.zeros((1, QB), F32)
        one = jnp.ones((1, QB), F32)

        def all_done(done_f):
            return jnp.min(done_f) > 0.5

        def peel_ready(st):
            it, hi, chi, done_f = st[0], st[2], st[4], st[5]
            few_left = (kf - chi <= PEEL_NEED) & (hi < F32_HIGHEST)
            ok = jnp.min(jnp.where(few_left, 1.0, done_f)) > 0.5
            return jnp.logical_and(it >= PEEL_AFTER, ok)

        def cond(st):
            go = jnp.logical_and(st[0] < SEARCH_MAX_ITERS, jnp.logical_not(all_done(st[5])))
            return jnp.logical_and(go, jnp.logical_not(peel_ready(st)))

        def cond_no_peel(st):
            return jnp.logical_and(st[0] < SEARCH_MAX_ITERS, jnp.logical_not(all_done(st[5])))

        def step(st):
            it, lo, hi, clo, chi, done_f, n_up, n_dn, w_lo, w_hi, last = st
            done = done_f > 0.5
            mid = 0.5 * lo + 0.5 * hi
            f_lo = (jnp.log(clo) - log_k) * w_lo
            f_hi = (log_k - jnp.log(jnp.maximum(chi, 0.5))) * w_hi
            interp = lo + (hi - lo) * (f_lo / jnp.maximum(f_lo + f_hi, 1e-9))
            open_hi = hi >= F32_HIGHEST
            open_lo = lo <= F32_LOWEST
            cand = jnp.where(it % SEARCH_BISECT_EVERY == SEARCH_BISECT_EVERY - 1, mid, interp)
            cand = jnp.where(open_hi, lo + std * (SEARCH_FIRST_STEP * jnp.exp2(n_up)), cand)
            cand = jnp.where(open_lo, hi - std * (SEARCH_FIRST_STEP * jnp.exp2(n_dn)), cand)
            cand = jnp.where(open_lo & open_hi, guess, cand)
            bad = jnp.logical_not((cand > lo) & (cand < hi))
            cand = jnp.where(bad, mid, cand)
            cand = jnp.where(done, lo, cand)
            cnt = count_ge(cand)
            up = jnp.logical_not(done) & (cnt >= kf)
            dn = jnp.logical_not(done) & (cnt < kf)
            n_up = n_up + jnp.where(up & open_hi, 1.0, 0.0)
            n_dn = n_dn + jnp.where(dn & open_lo, 1.0, 0.0)
            w_hi = jnp.where(up, jnp.where(last == 1.0, 0.5 * w_hi, 1.0), jnp.where(dn, 1.0, w_hi))
            w_lo = jnp.where(dn, jnp.where(last == 2.0, 0.5 * w_lo, 1.0), jnp.where(up, 1.0, w_lo))
            last = jnp.where(up, 1.0, jnp.where(dn, 2.0, last))
            lo = jnp.where(up, cand, lo)
            clo = jnp.where(up, cnt, clo)
            hi = jnp.where(dn, cand, hi)
            chi = jnp.where(dn, cnt, chi)
            mid2 = 0.5 * lo + 0.5 * hi
            stuck = jnp.logical_not((mid2 > lo) & (mid2 < hi))
            done_f = jnp.where(done | (clo == kf) | stuck, 1.0, 0.0)
            return it + 1, lo, hi, clo, chi, done_f, n_up, n_dn, w_lo, w_hi, last

        def peel_cond(ps):
            return jnp.logical_and(ps[0] < PEEL_NEED + 2, jnp.logical_not(all_done(ps[5])))

        def peel_step(ps):
            n, lo, hi, clo, chi, done_f = ps
            done = done_f > 0.5
            y = max_below(hi)
            cand = jnp.where(done, lo, y)
            cnt = count_ge(cand)
            up = jnp.logical_not(done) & (cnt >= kf)
            dn = jnp.logical_not(done) & (cnt < kf)
            lo = jnp.where(up, cand, lo)
            clo = jnp.where(up, cnt, clo)
            hi = jnp.where(dn, cand, hi)
            chi = jnp.where(dn, cnt, chi)
            return n + 1, lo, hi, clo, chi, jnp.where(done | up, 1.0, 0.0)

        st = lax.while_loop(cond, step, (jnp.int32(0), lo, hi, clo, chi, done_f,
                                         zero, zero, one, one, zero))
        ps = lax.while_loop(peel_cond, peel_step, (jnp.int32(0),) + tuple(st[1:6]))
        st = lax.while_loop(cond_no_peel, step, (st[0],) + tuple(ps[1:6]) + tuple(st[6:]))
        lo, hi, clo, chi = st[1], st[2], st[3], st[4]
        lo = jnp.where(few, F32_LOWEST, lo)

        need = kf - chi
        tied = jnp.logical_and(clo > kf, jnp.logical_not(few))

        @pl.when(jnp.max(tied.astype(F32)) > 0.5)
        def _break_ties():
            def count_eq_le(cut):
                def body(c, part):
                    row0 = pl.multiple_of(c * KC, KC)
                    tile = sc_ref[pl.ds(row0, KC), :]
                    kpos = (row0 + lax.broadcasted_iota(jnp.int32, (KC, QB), 0)).astype(F32)
                    ind = jnp.where((tile == lo) & (kpos <= cut), 1.0, 0.0)
                    return part + jnp.sum(ind.reshape(KC // 8, 8, QB), axis=0)
                part = lax.fori_loop(0, i + 1, body, jnp.zeros((8, QB), F32))
                return jnp.sum(part, axis=0, keepdims=True)

            def bis(_, st):
                a, b = st
                mid = jnp.floor(0.5 * (a + b))
                ok = count_eq_le(mid) <= need
                return jnp.where(ok, mid, a), jnp.where(ok, b, mid)

            a0 = jnp.full((1, QB), -1.0, F32)
            b0 = ((i + 1) * KC).astype(F32) + jnp.zeros((1, QB), F32)
            n_bits = int(np.ceil(np.log2(sc_ref.shape[0] + 2))) + 1
            cut, _ = lax.fori_loop(0, n_bits, bis, (a0, b0))

            def drop(c, _):
                row0 = pl.multiple_of(c * KC, KC)
                tile = sc_ref[pl.ds(row0, KC), :]
                kpos = (row0 + lax.broadcasted_iota(jnp.int32, (KC, QB), 0)).astype(F32)
                kill = tied & (tile == lo) & (kpos > cut)
                sc_ref[pl.ds(row0, KC), :] = jnp.where(kill, -jnp.inf, tile)
                return 0
            lax.fori_loop(0, i + 1, drop, 0)

        thr_ref[...] = jnp.broadcast_to(lo, thr_ref.shape)
        m_ref[...] = jnp.full(m_ref.shape, NEG_BIG, F32)
        l_ref[...] = jnp.zeros(l_ref.shape, F32)
        acc_ref[...] = jnp.zeros(acc_ref.shape, F32)

    row0 = pl.multiple_of(j * KC, KC)
    SL = ATTN_SLAB
    n_slab = KC // SL
    thr = jnp.broadcast_to(thr_ref[0:1, :], (SL, QB))
    for r in range(n_slab):
        rows = slice(r * SL, (r + 1) * SL)
        sc = sc_ref[pl.ds(row0 + r * SL, SL), :]
        mask_buf[rows, :] = jnp.where(sc >= thr, 0.0, NEG_BIG)
    posk = pos_ref[...]
    nxt_keys = chunk_keys(j)
    nxt_acc = jnp.zeros((KC, QB), F32)
    for g0 in range(0, N_SPARSE_HEADS, ATTN_HEAD_GROUP):
        heads = range(g0, g0 + ATTN_HEAD_GROUP)
        col_max = {}
        for h in heads:
            hs = slice(h * HEAD_DIM, (h + 1) * HEAD_DIM)
            lhs = jnp.concatenate([k_ref[:, hs], posk], axis=1)
            rhs = jnp.concatenate([qT_ref[hs, :], srow_ref[h]], axis=0)
            s = jnp.dot(lhs, rhs, preferred_element_type=F32)
            cm = [jnp.full((SL, QB), NEG_BIG, F32)] * 2
            for r in range(n_slab):
                rows = slice(r * SL, (r + 1) * SL)
                blk = s[rows, :] + mask_buf[rows, :]
                s_buf[h, rows, :] = blk
                cm[r % 2] = jnp.maximum(cm[r % 2], blk)
            col_max[h] = jnp.max(jnp.maximum(cm[0], cm[1]), axis=0, keepdims=True)
        alphas = {}
        for h in heads:
            nxt_acc = score_heads(nxt_acc, nxt_keys, qiT_next_ref, wT_next_ref,
                                  range(h * IDX_PER_ATTN_HEAD, h * IDX_PER_ATTN_HEAD + IDX_IN_SOFTMAX))
            m_old = m_ref[h]
            m_new = jnp.maximum(m_old, col_max[h])
            alphas[h] = jnp.exp2(m_old - m_new)
            m_ref[h] = m_new
            mb = jnp.concatenate([m_new] * (SL // 8), axis=0)
            for r in range(n_slab):
                rows = slice(r * SL, (r + 1) * SL)
                p_buf[h, rows, :] = jnp.exp2(s_buf[h, rows, :] - mb).astype(BF16)
        ones_rows = jnp.ones((ROW_ALIGN, KC), BF16)
        for h in heads:
            nxt_acc = score_heads(nxt_acc, nxt_keys, qiT_next_ref, wT_next_ref,
                                  range(h * IDX_PER_ATTN_HEAD + IDX_IN_SOFTMAX,
                                        (h + 1) * IDX_PER_ATTN_HEAD))
            hs = slice(h * HEAD_DIM, (h + 1) * HEAD_DIM)
            v_ext = jnp.concatenate([vT_ref[hs, :], ones_rows], axis=0)
            pv = jnp.dot(v_ext, p_buf[h], preferred_element_type=F32)
            acc_ref[hs, :] = alphas[h][0:1, :] * acc_ref[hs, :] + pv[:HEAD_DIM, :]
            l_ref[h] = alphas[h] * l_ref[h] + pv[HEAD_DIM:HEAD_DIM + 1, :]

    n1, n2 = finish_scores(nxt_acc, sc2_ref.at[1 - cur], j, False)
    mom_ref[1 - cur, 0] += n1
    mom_ref[1 - cur, 1] += n2

    @pl.when(j == i)
    def _finish():
        for h in range(N_SPARSE_HEADS):
            hs = slice(h * HEAD_DIM, (h + 1) * HEAD_DIM)
            o = acc_ref[hs, :] / l_ref[h, 0:1, :]
            o_ref[:, hs] = o.T.astype(o_ref.dtype)


N_SLOPE_PIECES = 3
POS_SPLIT = 128


def _alibi_mxu_operands(S, slopes_log2):
    assert S <= POS_SPLIT * 256
    kpos = jnp.arange(S, dtype=jnp.int32)[:, None]
    col = jnp.arange(128, dtype=jnp.int32)[None, :]
    part = jnp.where(col % 2 == 0, kpos // POS_SPLIT, kpos % POS_SPLIT)
    pos = jnp.where(col < 2 * N_SLOPE_PIECES, part, 0).astype(BF16)
    rows = np.zeros((len(slopes_log2), 128), np.float32)
    for h, s in enumerate(slopes_log2):
        rest = np.float64(s)
        for p in range(N_SLOPE_PIECES):
            piece = np.float64(np.asarray(rest, np.float32).astype(jnp.bfloat16).astype(np.float32))
            rows[h, 2 * p] = POS_SPLIT * piece
            rows[h, 2 * p + 1] = piece
            rest = rest - piece
    srow = jnp.broadcast_to(jnp.asarray(rows, F32)[:, :, None],
                            (len(slopes_log2), 128, DSA_QB)).astype(BF16)
    return pos, srow


def _dsa_attention(qT, k, vT, qiT, ki, wT, slopes, *, topk):
    S = k.shape[0]
    QB, KC = DSA_QB, DSA_KC
    assert S % (2 * QB) == 0 and QB == KC
    nq = S // QB
    pos, srow = _alibi_mxu_operands(S, [s * LOG2_E for s in slopes])
    quantile = statistics.NormalDist().inv_cdf
    zq = np.array([quantile(1.0 - (topk - 0.5) / n) if n > topk else 0.0
                   for n in range(1, S + 1)], np.float32)
    zq = np.broadcast_to(zq, (8, S))
    qi_tab = np.concatenate([np.full(i + 1, i, np.int32) for i in range(nq)])
    kj_tab = np.concatenate([np.arange(i + 1, dtype=np.int32) for i in range(nq)])
    n_steps = int(qi_tab.shape[0])
    W = N_SPARSE_HEADS * HEAD_DIM
    WI = N_IDX_HEADS * IDX_HEAD_DIM
    grid_spec = pltpu.PrefetchScalarGridSpec(
        num_scalar_prefetch=2,
        grid=(n_steps,),
        in_specs=[
            pl.BlockSpec((W, QB), lambda t, qi, kj: (0, qi[t])),
            pl.BlockSpec((KC, W), lambda t, qi, kj: (kj[t], 0)),
            pl.BlockSpec((W, KC), lambda t, qi, kj: (0, kj[t])),
            pl.BlockSpec((WI, QB), lambda t, qi, kj: (0, qi[t])),
            pl.BlockSpec((WI, QB), lambda t, qi, kj: (0, jnp.minimum(qi[t] + 1, nq - 1))),
            pl.BlockSpec((S, IDX_BLOCK), lambda t, qi, kj: (0, 0)),
            pl.BlockSpec((IDX_BLOCK, QB), lambda t, qi, kj: (0, qi[t])),
            pl.BlockSpec((IDX_BLOCK, QB), lambda t, qi, kj: (0, jnp.minimum(qi[t] + 1, nq - 1))),
            pl.BlockSpec((KC, 128), lambda t, qi, kj: (kj[t], 0)),
            pl.BlockSpec(srow.shape, lambda t, qi, kj: (0, 0, 0)),
            pl.BlockSpec((8, QB), lambda t, qi, kj: (0, qi[t])),
        ],
        out_specs=pl.BlockSpec((QB, W), lambda t, qi, kj: (qi[t], 0)),
        scratch_shapes=[
            pltpu.VMEM((2, S, QB), F32),
            pltpu.VMEM((8, QB), F32),
            pltpu.VMEM((N_SPARSE_HEADS, 8, QB), F32),
            pltpu.VMEM((N_SPARSE_HEADS, 8, QB), F32),
            pltpu.VMEM((W, QB), F32),
            pltpu.VMEM((N_SPARSE_HEADS, KC, QB), F32),
            pltpu.VMEM((N_SPARSE_HEADS, KC, QB), BF16),
            pltpu.VMEM((KC, QB), F32),
            pltpu.VMEM((2, 2, 8, QB), F32),
        ],
    )
    return pl.pallas_call(
        functools.partial(_dsa_kernel, topk=topk),
        grid_spec=grid_spec,
        out_shape=jax.ShapeDtypeStruct((S, W), BF16),
        compiler_params=_cparams("arbitrary"),
        name="dsa_attention",
    )(jnp.asarray(qi_tab), jnp.asarray(kj_tab), qT, k, vT, qiT, qiT, ki, wT, wT, pos, srow,
      jnp.asarray(zq))


DIL_BLOCKS_PER_STEP = 16
def _dil_kernel(bias_ref, q_ref, kc_ref, kp_ref, vc_ref, vp_ref, o_ref, lse_ref, *, n_blocks):
    first = jnp.where(pl.program_id(1) == 0, 1, 0)
    Q = BLOCK
    for b in range(n_blocks):
        rows = slice(b * Q, (b + 1) * Q)
        prev_rows = slice((b - 1) * Q, b * Q)
        for h in range(HEADS_PER_DIL_GROUP):
            hs = slice(h * HEAD_DIM, (h + 1) * HEAD_DIM)
            q = q_ref[rows, hs]
            k_prev = kp_ref[:, hs] if b == 0 else kc_ref[prev_rows, hs]
            v_prev = vp_ref[:, hs] if b == 0 else vc_ref[prev_rows, hs]
            kk = jnp.concatenate([k_prev, kc_ref[rows, hs]], axis=0)
            vv = jnp.concatenate([v_prev, vc_ref[rows, hs]], axis=0)
            bias = bias_ref[first, h] if b == 0 else bias_ref[0, h]
            s = lax.dot_general(q, kk, (((1,), (1,)), ((), ())),
                                preferred_element_type=F32) + bias
            mx = jnp.max(s, axis=-1, keepdims=True)
            p = jnp.exp2(s - mx)
            l = jnp.sum(p, axis=-1, keepdims=True)
            o = jnp.dot(p.astype(BF16), vv, preferred_element_type=F32) / l
            o_ref[rows, hs] = o
            lse_ref[rows, hs] = jnp.broadcast_to(mx + jnp.log2(l), (Q, HEAD_DIM))


def _dilated_bias(span, dilation, slopes_log2):
    Q = BLOCK
    step = Q + np.arange(Q)[:, None] - np.arange(2 * Q)[None, :]
    band = (step >= 0) & (step <= span)
    has_prev = np.stack([np.ones_like(band), np.broadcast_to(np.arange(2 * Q)[None, :] >= Q, band.shape)])
    dist = (step * dilation).astype(np.float64)
    bias = -np.asarray(slopes_log2, np.float64)[None, :, None, None] * dist[None, None]
    keep = (band[None] & has_prev)[:, None]
    return jnp.asarray(np.where(keep, bias, NEG_BIG), F32)


def _dilated_group(qkv, window, dilation, slopes_g):
    _, d, n_sub, GW = qkv.shape
    assert d == dilation
    span = window // d
    nbs = min(DIL_BLOCKS_PER_STEP, n_sub // BLOCK)
    rows = nbs * BLOCK
    assert span <= BLOCK and n_sub % rows == 0
    bias = _dilated_bias(span, d, [s * LOG2_E for s in slopes_g])

    def cur(which):
        return pl.BlockSpec((None, None, rows, GW), lambda r, m: (which, r, m, 0))

    def prev(which):
        return pl.BlockSpec((None, None, BLOCK, GW),
                            lambda r, m: (which, r, jnp.maximum(m * nbs - 1, 0), 0))

    out_spec = pl.BlockSpec((None, rows, GW), lambda r, m: (r, m, 0))
    return pl.pallas_call(
        functools.partial(_dil_kernel, n_blocks=nbs),
        grid=(d, n_sub // rows),
        in_specs=[pl.BlockSpec(bias.shape, lambda r, m: (0, 0, 0, 0)),
                  cur(0), cur(1), prev(1), cur(2), prev(2)],
        out_specs=[out_spec, out_spec],
        out_shape=[jax.ShapeDtypeStruct((d, n_sub, GW), F32)] * 2,
        compiler_params=_cparams("parallel", "arbitrary"),
        name=f"dilated_attention_d{d}",
    )(bias, qkv, qkv, qkv, qkv, qkv)


def _layer_norm(y, g, b):
    mu = jnp.mean(y, axis=-1, keepdims=True)
    yc = y - mu
    var = jnp.mean(yc * yc, axis=-1, keepdims=True)
    return yc * lax.rsqrt(var + LN_EPS) * g + b


def _merge_kernel(oa_ref, o1_ref, o2_ref, o3_ref, l1_ref, l2_ref, l3_ref,
                  ga_ref, gb_ref, x_ref, wa_ref, wb_ref, wo_ref, g_ref, b_ref,
                  h_ref, hb_ref, nat_ref):
    tm = x_ref.shape[0]
    n_slab = DIL_OUT_WIDTH // LANES

    def natural(ref, slot):
        d = ref.shape[0]
        if d == 1:
            return [ref[0, :, c * LANES:(c + 1) * LANES] for c in range(n_slab)]
        for r in range(d):
            for c in range(n_slab):
                nat_ref[slot, c, pl.ds(r, tm // d, stride=d), :] = ref[r, :, c * LANES:(c + 1) * LANES]
        return [nat_ref[slot, c] for c in range(n_slab)]

    o1, l1 = natural(o1_ref, 0), natural(l1_ref, 1)
    o2, l2 = natural(o2_ref, 2), natural(l2_ref, 3)
    o3, l3 = natural(o3_ref, 4), natural(l3_ref, 5)
    slabs = []
    for c in range(n_slab):
        lm = jnp.maximum(jnp.maximum(l1[c], l2[c]), l3[c])
        e1, e2, e3 = jnp.exp2(l1[c] - lm), jnp.exp2(l2[c] - lm), jnp.exp2(l3[c] - lm)
        den = e1 + e2 + e3
        slabs.append((e1 / den) * o1[c] + (e2 / den) * o2[c] + (e3 / den) * o3[c])
    ob = jnp.concatenate(slabs, axis=1)
    ya = jnp.dot(oa_ref[...], wa_ref[...], preferred_element_type=F32)
    yb = jnp.dot(ob.astype(BF16), wb_ref[...], preferred_element_type=F32)
    merged = ga_ref[...] * ya + gb_ref[...] * yb
    mix = jnp.dot(merged.astype(BF16), wo_ref[...], preferred_element_type=F32)
    h = _layer_norm(DEEPNORM_ALPHA * x_ref[...] + mix, g_ref[...], b_ref[...])
    h_ref[...] = h
    hb_ref[...] = h.astype(BF16)


MERGE_TM = 256


def _merge(oa, dil, gates, x2, wa, wb, wo, ln_g, ln_b):
    S = x2.shape[0]
    tm = MERGE_TM
    (o1, l1), (o2, l2), (o3, l3) = dil
    GW = DIL_OUT_WIDTH

    def rows(w, col=0):
        return pl.BlockSpec((tm, w), lambda i: (i, col))

    def residue_major(a):
        d = a.shape[0]
        assert tm % (8 * d) == 0
        return pl.BlockSpec((d, tm // d, GW), lambda i: (0, i, 0))

    def whole(a):
        return pl.BlockSpec(a.shape, lambda i: (0,) * a.ndim, pipeline_mode=pl.Buffered(1))

    return pl.pallas_call(
        _merge_kernel,
        grid=(S // tm,),
        in_specs=[rows(SPARSE_WIDTH),
                  residue_major(o1), residue_major(o2), residue_major(o3),
                  residue_major(l1), residue_major(l2), residue_major(l3),
                  rows(D_MODEL, 0), rows(D_MODEL, 1), rows(D_MODEL),
                  whole(wa), whole(wb), whole(wo), whole(ln_g), whole(ln_b)],
        out_specs=[rows(D_MODEL), rows(D_MODEL)],
        out_shape=[jax.ShapeDtypeStruct((S, D_MODEL), F32),
                   jax.ShapeDtypeStruct((S, D_MODEL), BF16)],
        scratch_shapes=[pltpu.VMEM((6, GW // LANES, tm, LANES), F32)],
        compiler_params=_cparams("parallel"),
        name="merge_outproj_ln",
    )(oa, o1, o2, o3, l1, l2, l3, gates, gates, x2, wa, wb, wo, ln_g, ln_b)


FFN_TM = 1024
FFN_UP_TN = 512
FFN_DOWN_TN = 256


def _ffn_up_kernel(hb_ref, wg_ref, wu_ref, a_ref):
    hb = hb_ref[...]
    gate = jnp.dot(hb, wg_ref[...].astype(BF16), preferred_element_type=F32)
    up = jnp.dot(hb, wu_ref[...].astype(BF16), preferred_element_type=F32)
    a_ref[...] = (gate * jax.nn.sigmoid(gate) * up).astype(a_ref.dtype)


def _ffn_down_kernel(a_ref, h_ref, wd_ref, g_ref, b_ref, o_ref):
    j = pl.program_id(1)
    tn = wd_ref.shape[1]
    cols = pl.ds(pl.multiple_of(j * tn, tn), tn)
    o_ref[:, cols] = jnp.dot(a_ref[...], wd_ref[...].astype(BF16), preferred_element_type=F32)

    @pl.when(j == pl.num_programs(1) - 1)
    def _():
        o_ref[...] = _layer_norm(DEEPNORM_ALPHA * h_ref[...] + o_ref[...],
                                 g_ref[...], b_ref[...])


def _ffn(hb, h, wg, wu, wd, ln_g, ln_b):
    S = h.shape[0]
    F = wg.shape[1]
    tm = min(FFN_TM, S)
    assert S % tm == 0 and F % FFN_UP_TN == 0 and D_MODEL % FFN_DOWN_TN == 0
    act = pl.pallas_call(
        _ffn_up_kernel,
        grid=(S // tm, F // FFN_UP_TN),
        in_specs=[pl.BlockSpec((tm, D_MODEL), lambda i, j: (i, 0)),
                  pl.BlockSpec((D_MODEL, FFN_UP_TN), lambda i, j: (0, j)),
                  pl.BlockSpec((D_MODEL, FFN_UP_TN), lambda i, j: (0, j))],
        out_specs=pl.BlockSpec((tm, FFN_UP_TN), lambda i, j: (i, j)),
        out_shape=jax.ShapeDtypeStruct((S, F), BF16),
        compiler_params=_cparams("parallel", "arbitrary"),
        name="ffn_gate_up",
    )(hb, wg, wu)

    def per_row_tile(width):
        return pl.BlockSpec((tm, width), lambda i, j: (i, 0), pipeline_mode=pl.Buffered(1))

    return pl.pallas_call(
        _ffn_down_kernel,
        grid=(S // tm, D_MODEL // FFN_DOWN_TN),
        in_specs=[per_row_tile(F), per_row_tile(D_MODEL),
                  pl.BlockSpec((F, FFN_DOWN_TN), lambda i, j: (0, j)),
                  pl.BlockSpec((1, D_MODEL), lambda i, j: (0, 0)),
                  pl.BlockSpec((1, D_MODEL), lambda i, j: (0, 0))],
        out_specs=pl.BlockSpec((tm, D_MODEL), lambda i, j: (i, 0)),
        out_shape=jax.ShapeDtypeStruct((S, D_MODEL), F32),
        compiler_params=_cparams("parallel", "arbitrary"),
        name="ffn_down_ln",
    )(act, h, wd, ln_g, ln_b)


def _mixer_inputs(x2, w_in):
    O = IN_OFFS
    wT = jnp.transpose(w_in)
    k, xb = _project_and_cast(x2, wT, O[1], SPARSE_WIDTH, name="proj_ak")
    qT = _project(xb, wT, O[0], SPARSE_WIDTH, out_dtype=BF16,
                  scale=LOG2_E * HEAD_DIM ** -0.5, transpose_out=True, name="proj_aqT")
    vT = _project(xb, wT, O[2], SPARSE_WIDTH, out_dtype=BF16, transpose_out=True,
                  name="proj_avT")
    qiT, ki, iwT = _project_indexer(xb, wT, O[3], O[4], q_scale=IDX_HEAD_DIM ** -0.5,
                                    w_scale=N_IDX_HEADS ** -0.5)
    dil_qkv = [_project_dilated_qkv(xb, wT, O[6], g, d) for g, (_, d) in enumerate(DIL_GROUPS)]
    gates = _project(xb, wT, O[9], 2 * D_MODEL, out_dtype=F32, sigmoid=True, name="proj_gates")
    return qT, k, vT, qiT, ki, iwT, dil_qkv, gates


def _layer(x2, w_in, w_a, w_b, w_out, ln1_g, ln1_b, w_gate, w_up, w_down, ln2_g, ln2_b):
    S = x2.shape[0]
    slopes = _alibi_slopes()
    qT, k, vT, qiT, ki, iwT, dil_qkv, gates = _mixer_inputs(x2, w_in)

    topk = min(TOPK_MAX, S // 4)
    oa = _dsa_attention(qT, k, vT, qiT, ki, iwT, slopes[N_DIL_HEADS:], topk=topk)

    dil = []
    for g, (win, d) in enumerate(DIL_GROUPS):
        sl = slopes[g * HEADS_PER_DIL_GROUP:(g + 1) * HEADS_PER_DIL_GROUP]
        dil.append(_dilated_group(dil_qkv[g], win, d, sl))

    h, hb = _merge(oa, dil, gates, x2, w_a.astype(BF16), w_b.astype(BF16), w_out.astype(BF16),
                   ln1_g.reshape(1, -1), ln1_b.reshape(1, -1))
    return _ffn(hb, h, w_gate, w_up, w_down, ln2_g.reshape(1, -1), ln2_b.reshape(1, -1))


def kernel(x, w_in, w_a, w_b, w_out, ln1_g, ln1_b, w_gate, w_up, w_down, ln2_g, ln2_b):
    B, S, D = x.shape
    outs = []
    for b in range(B):
        y = x[b]
        for l in range(w_in.shape[0]):
            y = _layer(y, w_in[l], w_a[l], w_b[l], w_out[l], ln1_g[l], ln1_b[l],
                       w_gate[l], w_up[l], w_down[l], ln2_g[l], ln2_b[l])
        outs.append(y)
    return jnp.stack(outs)
```

```python
import functools
import statistics

import numpy as np
import jax
import jax.numpy as jnp
from jax import lax
from jax.experimental import pallas as pl
from jax.experimental.pallas import tpu as pltpu

F32 = jnp.float32
BF16 = jnp.bfloat16

D_MODEL = 2048
HEAD_DIM = 128
N_SPARSE_HEADS = 8
SPARSE_WIDTH = N_SPARSE_HEADS * HEAD_DIM
N_IDX_HEADS = 16
IDX_HEAD_DIM = 64
TOPK_MAX = 256
DIL_GROUPS = ((128, 1), (512, 4), (2048, 16))
HEADS_PER_DIL_GROUP = 4
N_DIL_HEADS = HEADS_PER_DIL_GROUP * len(DIL_GROUPS)
DIL_WIDTH = N_DIL_HEADS * HEAD_DIM
DIL_OUT_WIDTH = HEADS_PER_DIL_GROUP * HEAD_DIM
N_ALIBI_HEADS = N_DIL_HEADS + N_SPARSE_HEADS
BLOCK = 128
D_FF = ((-(-8 * D_MODEL // 3)) + 255) // 256 * 256
DEPTH = 1
DEEPNORM_ALPHA = (2 * DEPTH) ** 0.25
LN_EPS = 1e-5
IN_SIZES = (SPARSE_WIDTH, SPARSE_WIDTH, SPARSE_WIDTH,
            N_IDX_HEADS * IDX_HEAD_DIM, IDX_HEAD_DIM, N_IDX_HEADS,
            DIL_WIDTH, DIL_WIDTH, DIL_WIDTH, D_MODEL, D_MODEL)
IN_OFFS = tuple(int(c) for c in np.cumsum((0,) + IN_SIZES))

VMEM_LIMIT_BYTES = 56 * 1024 * 1024
LOG2_E = float(np.log2(np.e))
NEG_BIG = -1e30
F32_LOWEST = float(np.finfo(np.float32).min)
F32_HIGHEST = float(np.finfo(np.float32).max)


def _alibi_slopes():
    return [2.0 ** (-8.0 * (i + 1) / N_ALIBI_HEADS) for i in range(N_ALIBI_HEADS)]


def _cparams(*sem):
    return pltpu.CompilerParams(dimension_semantics=sem,
                                vmem_limit_bytes=VMEM_LIMIT_BYTES)


LANES = 128
ROW_ALIGN = 16
PROJ_TM = 1024
PROJ_TN = 1024
IDX_BLOCK = 128


def _x_dot_wT(x_ref, w_ref):
    return lax.dot_general(x_ref[...], w_ref[...].astype(BF16), (((1,), (1,)), ((), ())),
                           preferred_element_type=F32)


def _w_rows_spec(tn, K, row_of_step):
    return pl.BlockSpec((pl.Element(tn), pl.Element(K)),
                        lambda i, j: (pl.multiple_of(row_of_step(j), ROW_ALIGN), 0))


def _proj_kernel(x_ref, w_ref, o_ref, *, scale, sigmoid, transpose_out):
    acc = _x_dot_wT(x_ref, w_ref)
    if scale != 1.0:
        acc = acc * scale
    if sigmoid:
        acc = jax.nn.sigmoid(acc)
    if transpose_out:
        acc = acc.T
    o_ref[...] = acc.astype(o_ref.dtype)


def _project(xb, wT, row0, ncols, *, out_dtype, scale=1.0, sigmoid=False,
             transpose_out=False, name):
    S, K = xb.shape
    tm, tn = min(PROJ_TM, S), PROJ_TN
    assert row0 % ROW_ALIGN == 0 and ncols % tn == 0 and S % tm == 0
    if transpose_out:
        out_shape = jax.ShapeDtypeStruct((ncols, S), out_dtype)
        out_spec = pl.BlockSpec((tn, tm), lambda i, j: (j, i))
    else:
        out_shape = jax.ShapeDtypeStruct((S, ncols), out_dtype)
        out_spec = pl.BlockSpec((tm, tn), lambda i, j: (i, j))
    return pl.pallas_call(
        functools.partial(_proj_kernel, scale=scale, sigmoid=sigmoid,
                          transpose_out=transpose_out),
        grid=(S // tm, ncols // tn),
        in_specs=[pl.BlockSpec((tm, K), lambda i, j: (i, 0)),
                  _w_rows_spec(tn, K, lambda j: row0 + j * tn)],
        out_specs=out_spec,
        out_shape=out_shape,
        compiler_params=_cparams("parallel", "arbitrary"),
        name=name,
    )(xb, wT)


def _proj_cast_kernel(x_ref, w_ref, o_ref, xb_ref):
    xb_ref[...] = x_ref[...].astype(xb_ref.dtype)
    o_ref[...] = _x_dot_wT(xb_ref, w_ref).astype(o_ref.dtype)


def _project_and_cast(x2, wT, row0, ncols, *, name):
    S, K = x2.shape
    tm = min(PROJ_TM, S)
    assert ncols == PROJ_TN and S % tm == 0
    return pl.pallas_call(
        _proj_cast_kernel,
        grid=(S // tm, 1),
        in_specs=[pl.BlockSpec((tm, K), lambda i, j: (i, 0)),
                  _w_rows_spec(ncols, K, lambda j: row0)],
        out_specs=[pl.BlockSpec((tm, ncols), lambda i, j: (i, 0)),
                   pl.BlockSpec((tm, K), lambda i, j: (i, 0))],
        out_shape=[jax.ShapeDtypeStruct((S, ncols), BF16), jax.ShapeDtypeStruct((S, K), BF16)],
        compiler_params=_cparams("parallel", "arbitrary"),
        name=name,
    )(x2, wT)


def _proj_indexer_kernel(x_ref, wq_ref, wkw_ref, qT_ref, k_ref, wT_ref, *, q_scale, w_scale):
    qT_ref[...] = (_x_dot_wT(x_ref, wq_ref) * q_scale).T.astype(qT_ref.dtype)
    kw = _x_dot_wT(x_ref, wkw_ref)
    k_ref[...] = kw.astype(k_ref.dtype)
    wT_ref[...] = (kw * w_scale).T


def _project_indexer(xb, wT, row_q, row_kw, *, q_scale, w_scale):
    S, K = xb.shape
    tm = min(PROJ_TM, S)
    WI = N_IDX_HEADS * IDX_HEAD_DIM
    return pl.pallas_call(
        functools.partial(_proj_indexer_kernel, q_scale=q_scale, w_scale=w_scale),
        grid=(S // tm, 1),
        in_specs=[pl.BlockSpec((tm, K), lambda i, j: (i, 0)),
                  _w_rows_spec(WI, K, lambda j: row_q),
                  _w_rows_spec(IDX_BLOCK, K, lambda j: row_kw)],
        out_specs=[pl.BlockSpec((WI, tm), lambda i, j: (0, i)),
                   pl.BlockSpec((tm, IDX_BLOCK), lambda i, j: (i, 0)),
                   pl.BlockSpec((IDX_BLOCK, tm), lambda i, j: (0, i))],
        out_shape=[jax.ShapeDtypeStruct((WI, S), BF16),
                   jax.ShapeDtypeStruct((S, IDX_BLOCK), BF16),
                   jax.ShapeDtypeStruct((IDX_BLOCK, S), F32)],
        compiler_params=_cparams("parallel", "arbitrary"),
        name="proj_indexer",
    )(xb, wT, wT)


def _proj_dilated_kernel(x_ref, wq_ref, wk_ref, wv_ref, o_ref, scr_ref, *, dilation):
    for which, w_ref in enumerate((wq_ref, wk_ref, wv_ref)):
        acc = _x_dot_wT(x_ref, w_ref)
        if which == 0:
            acc = acc * (LOG2_E * HEAD_DIM ** -0.5)
        tm, tn = acc.shape
        if dilation == 1:
            o_ref[which, 0] = acc.astype(o_ref.dtype)
            continue
        for c in range(tn // LANES):
            scr_ref[which, c] = acc[:, c * LANES:(c + 1) * LANES]
        for r in range(dilation):
            for c in range(tn // LANES):
                rows = scr_ref[which, c, pl.ds(r, tm // dilation, stride=dilation), :]
                o_ref[which, r, :, c * LANES:(c + 1) * LANES] = rows.astype(o_ref.dtype)


def _project_dilated_qkv(xb, wT, row0, g, dilation):
    S, K = xb.shape
    d = dilation
    tm, tn = min(PROJ_TM, S), DIL_OUT_WIDTH
    assert S % tm == 0 and tm % (d * ROW_ALIGN) == 0

    def w_spec(which):
        return _w_rows_spec(tn, K, lambda j: row0 + which * DIL_WIDTH + g * tn)

    return pl.pallas_call(
        functools.partial(_proj_dilated_kernel, dilation=d),
        grid=(S // tm, 1),
        in_specs=[pl.BlockSpec((tm, K), lambda i, j: (i, 0)), w_spec(0), w_spec(1), w_spec(2)],
        out_specs=pl.BlockSpec((3, d, tm // d, tn), lambda i, j: (0, 0, i, 0)),
        out_shape=jax.ShapeDtypeStruct((3, d, S // d, tn), BF16),
        scratch_shapes=[pltpu.VMEM((3, tn // LANES, tm, LANES), F32)],
        compiler_params=_cparams("parallel", "arbitrary"),
        name=f"proj_dil_qkv_d{d}",
    )(xb, wT, wT, wT)


DSA_QB = 256
DSA_KC = 256
SEARCH_MAX_ITERS = 512
SEARCH_BISECT_EVERY = 8
SEARCH_FIRST_STEP = 0.0625
PEEL_AFTER = 5
PEEL_NEED = 2
ATTN_SLAB = 16
ATTN_HEAD_GROUP = 8
IDX_PER_ATTN_HEAD = N_IDX_HEADS // N_SPARSE_HEADS
IDX_IN_SOFTMAX = 2
MOMENT_ROWS = 64
COUNT_SLAB = 16
COUNT_ACCUMULATORS = 2


def _dsa_kernel(qi_tab, kj_tab,
                qT_ref, k_ref, vT_ref, qiT_ref, qiT_next_ref, ki_ref, wT_ref, wT_next_ref,
                pos_ref, srow_ref, zq_ref,
                o_ref,
                sc2_ref, thr_ref, m_ref, l_ref, acc_ref, s_buf, p_buf, mask_buf, mom_ref, *, topk):
    QB, KC = DSA_QB, DSA_KC
    t = pl.program_id(0)
    i = qi_tab[t]
    j = kj_tab[t]
    cur = i % 2
    sc_ref = sc2_ref.at[cur]

    @pl.when(t == 0)
    def _first():
        mom_ref[...] = jnp.zeros(mom_ref.shape, F32)

    def score_heads(acc, kic, q_ref, w_ref, heads):
        for h in heads:
            r = jnp.dot(kic, q_ref[h * IDX_HEAD_DIM:(h + 1) * IDX_HEAD_DIM, :],
                        preferred_element_type=F32)
            w_h = w_ref[IDX_HEAD_DIM + h:IDX_HEAD_DIM + h + 1, :]
            acc = acc + w_h * jnp.maximum(r, 0.0)
        return acc

    def chunk_keys(c):
        row0 = pl.multiple_of(c * KC, KC)
        return ki_ref[pl.ds(row0, KC), 0:IDX_HEAD_DIM]

    def chunk_scores(q_ref, w_ref, dst_ref, c, diagonal):
        acc = score_heads(jnp.zeros((KC, QB), F32), chunk_keys(c), q_ref, w_ref,
                          range(N_IDX_HEADS))
        return finish_scores(acc, dst_ref, c, diagonal)

    def finish_scores(acc, dst_ref, c, diagonal):
        row0 = pl.multiple_of(c * KC, KC)
        if diagonal:
            kpos = lax.broadcasted_iota(jnp.int32, (KC, QB), 0)
            qpos = lax.broadcasted_iota(jnp.int32, (KC, QB), 1)
            causal = kpos <= qpos
            live = jnp.where(causal, acc, 0.0)
            acc = jnp.where(causal, acc, -jnp.inf)
        else:
            live = acc
        dst_ref[pl.ds(row0, KC), :] = acc
        live = live[:MOMENT_ROWS, :]
        s1 = jnp.sum(live.reshape(MOMENT_ROWS // 8, 8, QB), axis=0)
        s2 = jnp.sum((live * live).reshape(MOMENT_ROWS // 8, 8, QB), axis=0)
        return s1, s2

    def count_ge(cand):
        candb = jnp.broadcast_to(cand, (COUNT_SLAB, QB))

        def body(c, parts):
            row0 = pl.multiple_of(c * (2 * KC), 2 * KC)
            parts = list(parts)
            for r in range(2 * KC // COUNT_SLAB):
                slab = sc_ref[pl.ds(row0 + r * COUNT_SLAB, COUNT_SLAB), :]
                parts[r % len(parts)] = parts[r % len(parts)] + jnp.where(slab >= candb, 1.0, 0.0)
            return tuple(parts)
        zero = jnp.zeros((COUNT_SLAB, QB), F32)
        parts = lax.fori_loop(0, (i + 2) // 2, body, (zero,) * COUNT_ACCUMULATORS)
        return jnp.sum(sum(parts), axis=0, keepdims=True)

    def max_below(bound):
        boundb = jnp.broadcast_to(bound, (COUNT_SLAB, QB))

        def body(c, parts):
            row0 = pl.multiple_of(c * (2 * KC), 2 * KC)
            parts = list(parts)
            for r in range(2 * KC // COUNT_SLAB):
                slab = sc_ref[pl.ds(row0 + r * COUNT_SLAB, COUNT_SLAB), :]
                parts[r % len(parts)] = jnp.maximum(parts[r % len(parts)],
                                                    jnp.where(slab < boundb, slab, -jnp.inf))
            return tuple(parts)
        lowest = jnp.full((COUNT_SLAB, QB), -jnp.inf, F32)
        parts = lax.fori_loop(0, (i + 2) // 2, body, (lowest,) * COUNT_ACCUMULATORS)
        return jnp.max(functools.reduce(jnp.maximum, parts), axis=0, keepdims=True)

    @pl.when(j == 0)
    def _prepare():
        d1, d2 = chunk_scores(qiT_ref, wT_ref, sc_ref, i, True)
        s1 = jnp.sum(mom_ref[cur, 0] + d1, axis=0, keepdims=True)
        s2 = jnp.sum(mom_ref[cur, 1] + d2, axis=0, keepdims=True)
        mom_ref[1 - cur] = jnp.zeros(mom_ref.shape[1:], F32)

        @pl.when(i % 2 == 0)
        def _pad():
            sc_ref[pl.ds(pl.multiple_of((i + 1) * KC, KC), KC), :] = jnp.full((KC, QB), -jnp.inf, F32)

        kf = float(topk)
        log_k = float(np.log(topk - 0.5))
        lane = lax.broadcasted_iota(jnp.int32, (1, QB), 1)
        n_causal = (i * QB + 1 + lane).astype(F32)
        n_sample = (i * MOMENT_ROWS + jnp.minimum(lane + 1, MOMENT_ROWS)).astype(F32)
        mean = s1 / n_sample
        std = jnp.sqrt(jnp.maximum(s2 / n_sample - mean * mean, 0.0))
        zq = jnp.max(zq_ref[...], axis=0, keepdims=True)
        guess = mean + zq * std
        few = n_causal <= kf
        lo = jnp.full((1, QB), F32_LOWEST, F32)
        clo = n_causal
        hi = jnp.full((1, QB), F32_HIGHEST, F32)
        chi = jnp.zeros((1, QB), F32)
        done_f = jnp.where(few | (clo == kf), 1.0, 0.0)
        zero = jnp.zeros((1, QB), F32)
        one = jnp.ones((1, QB), F32)

        def all_done(done_f):
            return jnp.min(done_f) > 0.5

        def peel_ready(st):
            it, hi, chi, done_f = st[0], st[2], st[4], st[5]
            few_left = (kf - chi <= PEEL_NEED) & (hi < F32_HIGHEST)
            ok = jnp.min(jnp.where(few_left, 1.0, done_f)) > 0.5
            return jnp.logical_and(it >= PEEL_AFTER, ok)

        def cond(st):
            go = jnp.logical_and(st[0] < SEARCH_MAX_ITERS, jnp.logical_not(all_done(st[5])))
            return jnp.logical_and(go, jnp.logical_not(peel_ready(st)))

        def cond_no_peel(st):
            return jnp.logical_and(st[0] < SEARCH_MAX_ITERS, jnp.logical_not(all_done(st[5])))

        def step(st):
            it, lo, hi, clo, chi, done_f, n_up, n_dn, w_lo, w_hi, last = st
            done = done_f > 0.5
            mid = 0.5 * lo + 0.5 * hi
            f_lo = (jnp.log(clo) - log_k) * w_lo
            f_hi = (log_k - jnp.log(jnp.maximum(chi, 0.5))) * w_hi
            interp = lo + (hi - lo) * (f_lo / jnp.maximum(f_lo + f_hi, 1e-9))
            open_hi = hi >= F32_HIGHEST
            open_lo = lo <= F32_LOWEST
            cand = jnp.where(it % SEARCH_BISECT_EVERY == SEARCH_BISECT_EVERY - 1, mid, interp)
            cand = jnp.where(open_hi, lo + std * (SEARCH_FIRST_STEP * jnp.exp2(n_up)), cand)
            cand = jnp.where(open_lo, hi - std * (SEARCH_FIRST_STEP * jnp.exp2(n_dn)), cand)
            cand = jnp.where(open_lo & open_hi, guess, cand)
            bad = jnp.logical_not((cand > lo) & (cand < hi))
            cand = jnp.where(bad, mid, cand)
            cand = jnp.where(done, lo, cand)
            cnt = count_ge(cand)
            up = jnp.logical_not(done) & (cnt >= kf)
            dn = jnp.logical_not(done) & (cnt < kf)
            n_up = n_up + jnp.where(up & open_hi, 1.0, 0.0)
            n_dn = n_dn + jnp.where(dn & open_lo, 1.0, 0.0)
            w_hi = jnp.where(up, jnp.where(last == 1.0, 0.5 * w_hi, 1.0), jnp.where(dn, 1.0, w_hi))
            w_lo = jnp.where(dn, jnp.where(last == 2.0, 0.5 * w_lo, 1.0), jnp.where(up, 1.0, w_lo))
            last = jnp.where(up, 1.0, jnp.where(dn, 2.0, last))
            lo = jnp.where(up, cand, lo)
            clo = jnp.where(up, cnt, clo)
            hi = jnp.where(dn, cand, hi)
            chi = jnp.where(dn, cnt, chi)
            mid2 = 0.5 * lo + 0.5 * hi
            stuck = jnp.logical_not((mid2 > lo) & (mid2 < hi))
            done_f = jnp.where(done | (clo == kf) | stuck, 1.0, 0.0)
            return it + 1, lo, hi, clo, chi, done_f, n_up, n_dn, w_lo, w_hi, last

        def peel_cond(ps):
            return jnp.logical_and(ps[0] < PEEL_NEED + 2, jnp.logical_not(all_done(ps[5])))

        def peel_step(ps):
            n, lo, hi, clo, chi, done_f = ps
            done = done_f > 0.5
            y = max_below(hi)
            cand = jnp.where(done, lo, y)
            cnt = count_ge(cand)
            up = jnp.logical_not(done) & (cnt >= kf)
            dn = jnp.logical_not(done) & (cnt < kf)
            lo = jnp.where(up, cand, lo)
            clo = jnp.where(up, cnt, clo)
            hi = jnp.where(dn, cand, hi)
            chi = jnp.where(dn, cnt, chi)
            return n + 1, lo, hi, clo, chi, jnp.where(done | up, 1.0, 0.0)

        st = (jnp.int32(0), lo, hi, clo, chi, done_f, zero, zero, one, one, zero)
        st = lax.fori_loop(0, PEEL_AFTER, lambda _, s: step(s), st)
        st = lax.while_loop(cond, step, st)
        ps = lax.while_loop(peel_cond, peel_step, (jnp.int32(0),) + tuple(st[1:6]))
        st = lax.while_loop(cond_no_peel, step, (st[0],) + tuple(ps[1:6]) + tuple(st[6:]))
        lo, hi, clo, chi = st[1], st[2], st[3], st[4]
        lo = jnp.where(few, F32_LOWEST, lo)

        need = kf - chi
        tied = jnp.logical_and(clo > kf, jnp.logical_not(few))

        @pl.when(jnp.max(tied.astype(F32)) > 0.5)
        def _break_ties():
            def count_eq_le(cut):
                def body(c, part):
                    row0 = pl.multiple_of(c * KC, KC)
                    tile = sc_ref[pl.ds(row0, KC), :]
                    kpos = (row0 + lax.broadcasted_iota(jnp.int32, (KC, QB), 0)).astype(F32)
                    ind = jnp.where((tile == lo) & (kpos <= cut), 1.0, 0.0)
                    return part + jnp.sum(ind.reshape(KC // 8, 8, QB), axis=0)
                part = lax.fori_loop(0, i + 1, body, jnp.zeros((8, QB), F32))
                return jnp.sum(part, axis=0, keepdims=True)

            def bis(_, st):
                a, b = st
                mid = jnp.floor(0.5 * (a + b))
                ok = count_eq_le(mid) <= need
                return jnp.where(ok, mid, a), jnp.where(ok, b, mid)

            a0 = jnp.full((1, QB), -1.0, F32)
            b0 = ((i + 1) * KC).astype(F32) + jnp.zeros((1, QB), F32)
            n_bits = int(np.ceil(np.log2(sc_ref.shape[0] + 2))) + 1
            cut, _ = lax.fori_loop(0, n_bits, bis, (a0, b0))

            def drop(c, _):
                row0 = pl.multiple_of(c * KC, KC)
                tile = sc_ref[pl.ds(row0, KC), :]
                kpos = (row0 + lax.broadcasted_iota(jnp.int32, (KC, QB), 0)).astype(F32)
                kill = tied & (tile == lo) & (kpos > cut)
                sc_ref[pl.ds(row0, KC), :] = jnp.where(kill, -jnp.inf, tile)
                return 0
            lax.fori_loop(0, i + 1, drop, 0)

        thr_ref[...] = jnp.broadcast_to(lo, thr_ref.shape)
        m_ref[...] = jnp.full(m_ref.shape, NEG_BIG, F32)
        l_ref[...] = jnp.zeros(l_ref.shape, F32)
        acc_ref[...] = jnp.zeros(acc_ref.shape, F32)

    row0 = pl.multiple_of(j * KC, KC)
    SL = ATTN_SLAB
    n_slab = KC // SL
    thr = jnp.broadcast_to(thr_ref[0:1, :], (SL, QB))
    for r in range(n_slab):
        rows = slice(r * SL, (r + 1) * SL)
        sc = sc_ref[pl.ds(row0 + r * SL, SL), :]
        mask_buf[rows, :] = jnp.where(sc >= thr, 0.0, NEG_BIG)
    posk = pos_ref[...]
    nxt_keys = chunk_keys(j)
    nxt_acc = jnp.zeros((KC, QB), F32)
    for g0 in range(0, N_SPARSE_HEADS, ATTN_HEAD_GROUP):
        heads = range(g0, g0 + ATTN_HEAD_GROUP)
        col_max = {}
        for h in heads:
            hs = slice(h * HEAD_DIM, (h + 1) * HEAD_DIM)
            lhs = jnp.concatenate([k_ref[:, hs], posk], axis=1)
            rhs = jnp.concatenate([qT_ref[hs, :], srow_ref[h]], axis=0)
            s = jnp.dot(lhs, rhs, preferred_element_type=F32)
            cm = [jnp.full((SL, QB), NEG_BIG, F32)] * 2
            for r in range(n_slab):
                rows = slice(r * SL, (r + 1) * SL)
                blk = s[rows, :] + mask_buf[rows, :]
                s_buf[h, rows, :] = blk
                cm[r % 2] = jnp.maximum(cm[r % 2], blk)
            col_max[h] = jnp.max(jnp.maximum(cm[0], cm[1]), axis=0, keepdims=True)
        alphas = {}
        for h in heads:
            nxt_acc = score_heads(nxt_acc, nxt_keys, qiT_next_ref, wT_next_ref,
                                  range(h * IDX_PER_ATTN_HEAD, h * IDX_PER_ATTN_HEAD + IDX_IN_SOFTMAX))
            m_old = m_ref[h]
            m_new = jnp.maximum(m_old, col_max[h])
            alphas[h] = jnp.exp2(m_old - m_new)
            m_ref[h] = m_new
            mb = jnp.concatenate([m_new] * (SL // 8), axis=0)
            for r in range(n_slab):
                rows = slice(r * SL, (r + 1) * SL)
                p_buf[h, rows, :] = jnp.exp2(s_buf[h, rows, :] - mb).astype(BF16)
        ones_rows = jnp.ones((ROW_ALIGN, KC), BF16)
        for h in heads:
            nxt_acc = score_heads(nxt_acc, nxt_keys, qiT_next_ref, wT_next_ref,
                                  range(h * IDX_PER_ATTN_HEAD + IDX_IN_SOFTMAX,
                                        (h + 1) * IDX_PER_ATTN_HEAD))
            hs = slice(h * HEAD_DIM, (h + 1) * HEAD_DIM)
            v_ext = jnp.concatenate([vT_ref[hs, :], ones_rows], axis=0)
            pv = jnp.dot(v_ext, p_buf[h], preferred_element_type=F32)
            acc_ref[hs, :] = alphas[h][0:1, :] * acc_ref[hs, :] + pv[:HEAD_DIM, :]
            l_ref[h] = alphas[h] * l_ref[h] + pv[HEAD_DIM:HEAD_DIM + 1, :]

    n1, n2 = finish_scores(nxt_acc, sc2_ref.at[1 - cur], j, False)
    mom_ref[1 - cur, 0] += n1
    mom_ref[1 - cur, 1] += n2

    @pl.when(j == i)
    def _finish():
        for h in range(N_SPARSE_HEADS):
            hs = slice(h * HEAD_DIM, (h + 1) * HEAD_DIM)
            o = acc_ref[hs, :] / l_ref[h, 0:1, :]
            o_ref[:, hs] = o.T.astype(o_ref.dtype)


N_SLOPE_PIECES = 3
POS_SPLIT = 128


def _alibi_mxu_operands(S, slopes_log2):
    assert S <= POS_SPLIT * 256
    kpos = jnp.arange(S, dtype=jnp.int32)[:, None]
    col = jnp.arange(128, dtype=jnp.int32)[None, :]
    part = jnp.where(col % 2 == 0, kpos // POS_SPLIT, kpos % POS_SPLIT)
    pos = jnp.where(col < 2 * N_SLOPE_PIECES, part, 0).astype(BF16)
    rows = np.zeros((len(slopes_log2), 128), np.float32)
    for h, s in enumerate(slopes_log2):
        rest = np.float64(s)
        for p in range(N_SLOPE_PIECES):
            piece = np.float64(np.asarray(rest, np.float32).astype(jnp.bfloat16).astype(np.float32))
            rows[h, 2 * p] = POS_SPLIT * piece
            rows[h, 2 * p + 1] = piece
            rest = rest - piece
    srow = jnp.broadcast_to(jnp.asarray(rows, F32)[:, :, None],
                            (len(slopes_log2), 128, DSA_QB)).astype(BF16)
    return pos, srow


def _dsa_attention(qT, k, vT, qiT, ki, wT, slopes, *, topk):
    S = k.shape[0]
    QB, KC = DSA_QB, DSA_KC
    assert S % (2 * QB) == 0 and QB == KC
    nq = S // QB
    pos, srow = _alibi_mxu_operands(S, [s * LOG2_E for s in slopes])
    quantile = statistics.NormalDist().inv_cdf
    zq = np.array([quantile(1.0 - (topk - 0.5) / n) if n > topk else 0.0
                   for n in range(1, S + 1)], np.float32)
    zq = np.broadcast_to(zq, (8, S))
    qi_tab = np.concatenate([np.full(i + 1, i, np.int32) for i in range(nq)])
    kj_tab = np.concatenate([np.arange(i + 1, dtype=np.int32) for i in range(nq)])
    n_steps = int(qi_tab.shape[0])
    W = N_SPARSE_HEADS * HEAD_DIM
    WI = N_IDX_HEADS * IDX_HEAD_DIM
    grid_spec = pltpu.PrefetchScalarGridSpec(
        num_scalar_prefetch=2,
        grid=(n_steps,),
        in_specs=[
            pl.BlockSpec((W, QB), lambda t, qi, kj: (0, qi[t])),
            pl.BlockSpec((KC, W), lambda t, qi, kj: (kj[t], 0)),
            pl.BlockSpec((W, KC), lambda t, qi, kj: (0, kj[t])),
            pl.BlockSpec((WI, QB), lambda t, qi, kj: (0, qi[t])),
            pl.BlockSpec((WI, QB), lambda t, qi, kj: (0, jnp.minimum(qi[t] + 1, nq - 1))),
            pl.BlockSpec((S, IDX_BLOCK), lambda t, qi, kj: (0, 0)),
            pl.BlockSpec((IDX_BLOCK, QB), lambda t, qi, kj: (0, qi[t])),
            pl.BlockSpec((IDX_BLOCK, QB), lambda t, qi, kj: (0, jnp.minimum(qi[t] + 1, nq - 1))),
            pl.BlockSpec((KC, 128), lambda t, qi, kj: (kj[t], 0)),
            pl.BlockSpec(srow.shape, lambda t, qi, kj: (0, 0, 0)),
            pl.BlockSpec((8, QB), lambda t, qi, kj: (0, qi[t])),
        ],
        out_specs=pl.BlockSpec((QB, W), lambda t, qi, kj: (qi[t], 0)),
        scratch_shapes=[
            pltpu.VMEM((2, S, QB), F32),
            pltpu.VMEM((8, QB), F32),
            pltpu.VMEM((N_SPARSE_HEADS, 8, QB), F32),
            pltpu.VMEM((N_SPARSE_HEADS, 8, QB), F32),
            pltpu.VMEM((W, QB), F32),
            pltpu.VMEM((N_SPARSE_HEADS, KC, QB), F32),
            pltpu.VMEM((N_SPARSE_HEADS, KC, QB), BF16),
            pltpu.VMEM((KC, QB), F32),
            pltpu.VMEM((2, 2, 8, QB), F32),
        ],
    )
    return pl.pallas_call(
        functools.partial(_dsa_kernel, topk=topk),
        grid_spec=grid_spec,
        out_shape=jax.ShapeDtypeStruct((S, W), BF16),
        compiler_params=_cparams("arbitrary"),
        name="dsa_attention",
    )(jnp.asarray(qi_tab), jnp.asarray(kj_tab), qT, k, vT, qiT, qiT, ki, wT, wT, pos, srow,
      jnp.asarray(zq))


DIL_BLOCKS_PER_STEP = 16
def _dil_kernel(bias_ref, q_ref, kc_ref, kp_ref, vc_ref, vp_ref, o_ref, lse_ref, *, n_blocks):
    first = jnp.where(pl.program_id(1) == 0, 1, 0)
    Q = BLOCK
    for b in range(n_blocks):
        rows = slice(b * Q, (b + 1) * Q)
        prev_rows = slice((b - 1) * Q, b * Q)
        for h in range(HEADS_PER_DIL_GROUP):
            hs = slice(h * HEAD_DIM, (h + 1) * HEAD_DIM)
            q = q_ref[rows, hs]
            k_prev = kp_ref[:, hs] if b == 0 else kc_ref[prev_rows, hs]
            v_prev = vp_ref[:, hs] if b == 0 else vc_ref[prev_rows, hs]
            kk = jnp.concatenate([k_prev, kc_ref[rows, hs]], axis=0)
            vv = jnp.concatenate([v_prev, vc_ref[rows, hs]], axis=0)
            bias = bias_ref[first, h] if b == 0 else bias_ref[0, h]
            s = lax.dot_general(q, kk, (((1,), (1,)), ((), ())),
                                preferred_element_type=F32) + bias
            mx = jnp.max(s, axis=-1, keepdims=True)
            p = jnp.exp2(s - mx)
            l = jnp.sum(p, axis=-1, keepdims=True)
            o = jnp.dot(p.astype(BF16), vv, preferred_element_type=F32) / l
            o_ref[rows, hs] = o
            lse_ref[rows, hs] = jnp.broadcast_to(mx + jnp.log2(l), (Q, HEAD_DIM))


def _dilated_bias(span, dilation, slopes_log2):
    Q = BLOCK
    step = Q + np.arange(Q)[:, None] - np.arange(2 * Q)[None, :]
    band = (step >= 0) & (step <= span)
    has_prev = np.stack([np.ones_like(band), np.broadcast_to(np.arange(2 * Q)[None, :] >= Q, band.shape)])
    dist = (step * dilation).astype(np.float64)
    bias = -np.asarray(slopes_log2, np.float64)[None, :, None, None] * dist[None, None]
    keep = (band[None] & has_prev)[:, None]
    return jnp.asarray(np.where(keep, bias, NEG_BIG), F32)


def _dilated_group(qkv, window, dilation, slopes_g):
    _, d, n_sub, GW = qkv.shape
    assert d == dilation
    span = window // d
    nbs = min(DIL_BLOCKS_PER_STEP, n_sub // BLOCK)
    rows = nbs * BLOCK
    assert span <= BLOCK and n_sub % rows == 0
    bias = _dilated_bias(span, d, [s * LOG2_E for s in slopes_g])

    def cur(which):
        return pl.BlockSpec((None, None, rows, GW), lambda r, m: (which, r, m, 0))

    def prev(which):
        return pl.BlockSpec((None, None, BLOCK, GW),
                            lambda r, m: (which, r, jnp.maximum(m * nbs - 1, 0), 0))

    out_spec = pl.BlockSpec((None, rows, GW), lambda r, m: (r, m, 0))
    return pl.pallas_call(
        functools.partial(_dil_kernel, n_blocks=nbs),
        grid=(d, n_sub // rows),
        in_specs=[pl.BlockSpec(bias.shape, lambda r, m: (0, 0, 0, 0)),
                  cur(0), cur(1), prev(1), cur(2), prev(2)],
        out_specs=[out_spec, out_spec],
        out_shape=[jax.ShapeDtypeStruct((d, n_sub, GW), F32)] * 2,
        compiler_params=_cparams("parallel", "arbitrary"),
        name=f"dilated_attention_d{d}",
    )(bias, qkv, qkv, qkv, qkv, qkv)


def _layer_norm(y, g, b):
    mu = jnp.mean(y, axis=-1, keepdims=True)
    yc = y - mu
    var = jnp.mean(yc * yc, axis=-1, keepdims=True)
    return yc * lax.rsqrt(var + LN_EPS) * g + b


def _merge_kernel(oa_ref, o1_ref, o2_ref, o3_ref, l1_ref, l2_ref, l3_ref,
                  ga_ref, gb_ref, x_ref, wa_ref, wb_ref, wo_ref, g_ref, b_ref,
                  h_ref, hb_ref, nat_ref):
    tm = x_ref.shape[0]
    n_slab = DIL_OUT_WIDTH // LANES

    def natural(ref, slot):
        d = ref.shape[0]
        if d == 1:
            return [ref[0, :, c * LANES:(c + 1) * LANES] for c in range(n_slab)]
        for r in range(d):
            for c in range(n_slab):
                nat_ref[slot, c, pl.ds(r, tm // d, stride=d), :] = ref[r, :, c * LANES:(c + 1) * LANES]
        return [nat_ref[slot, c] for c in range(n_slab)]

    o1, l1 = natural(o1_ref, 0), natural(l1_ref, 1)
    o2, l2 = natural(o2_ref, 2), natural(l2_ref, 3)
    o3, l3 = natural(o3_ref, 4), natural(l3_ref, 5)
    slabs = []
    for c in range(n_slab):
        lm = jnp.maximum(jnp.maximum(l1[c], l2[c]), l3[c])
        e1, e2, e3 = jnp.exp2(l1[c] - lm), jnp.exp2(l2[c] - lm), jnp.exp2(l3[c] - lm)
        den = e1 + e2 + e3
        slabs.append((e1 / den) * o1[c] + (e2 / den) * o2[c] + (e3 / den) * o3[c])
    ob = jnp.concatenate(slabs, axis=1)
    ya = jnp.dot(oa_ref[...], wa_ref[...], preferred_element_type=F32)
    yb = jnp.dot(ob.astype(BF16), wb_ref[...], preferred_element_type=F32)
    merged = ga_ref[...] * ya + gb_ref[...] * yb
    mix = jnp.dot(merged.astype(BF16), wo_ref[...], preferred_element_type=F32)
    h = _layer_norm(DEEPNORM_ALPHA * x_ref[...] + mix, g_ref[...], b_ref[...])
    h_ref[...] = h
    hb_ref[...] = h.astype(BF16)


MERGE_TM = 256


def _merge(oa, dil, gates, x2, wa, wb, wo, ln_g, ln_b):
    S = x2.shape[0]
    tm = MERGE_TM
    (o1, l1), (o2, l2), (o3, l3) = dil
    GW = DIL_OUT_WIDTH

    def rows(w, col=0):
        return pl.BlockSpec((tm, w), lambda i: (i, col))

    def residue_major(a):
        d = a.shape[0]
        assert tm % (8 * d) == 0
        return pl.BlockSpec((d, tm // d, GW), lambda i: (0, i, 0))

    def whole(a):
        return pl.BlockSpec(a.shape, lambda i: (0,) * a.ndim, pipeline_mode=pl.Buffered(1))

    return pl.pallas_call(
        _merge_kernel,
        grid=(S // tm,),
        in_specs=[rows(SPARSE_WIDTH),
                  residue_major(o1), residue_major(o2), residue_major(o3),
                  residue_major(l1), residue_major(l2), residue_major(l3),
                  rows(D_MODEL, 0), rows(D_MODEL, 1), rows(D_MODEL),
                  whole(wa), whole(wb), whole(wo), whole(ln_g), whole(ln_b)],
        out_specs=[rows(D_MODEL), rows(D_MODEL)],
        out_shape=[jax.ShapeDtypeStruct((S, D_MODEL), F32),
                   jax.ShapeDtypeStruct((S, D_MODEL), BF16)],
        scratch_shapes=[pltpu.VMEM((6, GW // LANES, tm, LANES), F32)],
        compiler_params=_cparams("parallel"),
        name="merge_outproj_ln",
    )(oa, o1, o2, o3, l1, l2, l3, gates, gates, x2, wa, wb, wo, ln_g, ln_b)


FFN_TM = 1024
FFN_UP_TN = 512
FFN_DOWN_TN = 256


def _ffn_up_kernel(hb_ref, wg_ref, wu_ref, a_ref):
    hb = hb_ref[...]
    gate = jnp.dot(hb, wg_ref[...].astype(BF16), preferred_element_type=F32)
    up = jnp.dot(hb, wu_ref[...].astype(BF16), preferred_element_type=F32)
    a_ref[...] = (gate * jax.nn.sigmoid(gate) * up).astype(a_ref.dtype)


def _ffn_down_kernel(a_ref, h_ref, wd_ref, g_ref, b_ref, o_ref):
    j = pl.program_id(1)
    tn = wd_ref.shape[1]
    cols = pl.ds(pl.multiple_of(j * tn, tn), tn)
    o_ref[:, cols] = jnp.dot(a_ref[...], wd_ref[...].astype(BF16), preferred_element_type=F32)

    @pl.when(j == pl.num_programs(1) - 1)
    def _():
        o_ref[...] = _layer_norm(DEEPNORM_ALPHA * h_ref[...] + o_ref[...],
                                 g_ref[...], b_ref[...])


def _ffn(hb, h, wg, wu, wd, ln_g, ln_b):
    S = h.shape[0]
    F = wg.shape[1]
    tm = min(FFN_TM, S)
    assert S % tm == 0 and F % FFN_UP_TN == 0 and D_MODEL % FFN_DOWN_TN == 0
    act = pl.pallas_call(
        _ffn_up_kernel,
        grid=(S // tm, F // FFN_UP_TN),
        in_specs=[pl.BlockSpec((tm, D_MODEL), lambda i, j: (i, 0)),
                  pl.BlockSpec((D_MODEL, FFN_UP_TN), lambda i, j: (0, j)),
                  pl.BlockSpec((D_MODEL, FFN_UP_TN), lambda i, j: (0, j))],
        out_specs=pl.BlockSpec((tm, FFN_UP_TN), lambda i, j: (i, j)),
        out_shape=jax.ShapeDtypeStruct((S, F), BF16),
        compiler_params=_cparams("parallel", "arbitrary"),
        name="ffn_gate_up",
    )(hb, wg, wu)

    def per_row_tile(width):
        return pl.BlockSpec((tm, width), lambda i, j: (i, 0), pipeline_mode=pl.Buffered(1))

    return pl.pallas_call(
        _ffn_down_kernel,
        grid=(S // tm, D_MODEL // FFN_DOWN_TN),
        in_specs=[per_row_tile(F), per_row_tile(D_MODEL),
                  pl.BlockSpec((F, FFN_DOWN_TN), lambda i, j: (0, j)),
                  pl.BlockSpec((1, D_MODEL), lambda i, j: (0, 0)),
                  pl.BlockSpec((1, D_MODEL), lambda i, j: (0, 0))],
        out_specs=pl.BlockSpec((tm, D_MODEL), lambda i, j: (i, 0)),
        out_shape=jax.ShapeDtypeStruct((S, D_MODEL), F32),
        compiler_params=_cparams("parallel", "arbitrary"),
        name="ffn_down_ln",
    )(act, h, wd, ln_g, ln_b)


def _mixer_inputs(x2, w_in):
    O = IN_OFFS
    wT = jnp.transpose(w_in)
    k, xb = _project_and_cast(x2, wT, O[1], SPARSE_WIDTH, name="proj_ak")
    qT = _project(xb, wT, O[0], SPARSE_WIDTH, out_dtype=BF16,
                  scale=LOG2_E * HEAD_DIM ** -0.5, transpose_out=True, name="proj_aqT")
    vT = _project(xb, wT, O[2], SPARSE_WIDTH, out_dtype=BF16, transpose_out=True,
                  name="proj_avT")
    qiT, ki, iwT = _project_indexer(xb, wT, O[3], O[4], q_scale=IDX_HEAD_DIM ** -0.5,
                                    w_scale=N_IDX_HEADS ** -0.5)
    dil_qkv = [_project_dilated_qkv(xb, wT, O[6], g, d) for g, (_, d) in enumerate(DIL_GROUPS)]
    gates = _project(xb, wT, O[9], 2 * D_MODEL, out_dtype=F32, sigmoid=True, name="proj_gates")
    return qT, k, vT, qiT, ki, iwT, dil_qkv, gates


def _layer(x2, w_in, w_a, w_b, w_out, ln1_g, ln1_b, w_gate, w_up, w_down, ln2_g, ln2_b):
    S = x2.shape[0]
    slopes = _alibi_slopes()
    qT, k, vT, qiT, ki, iwT, dil_qkv, gates = _mixer_inputs(x2, w_in)

    topk = min(TOPK_MAX, S // 4)
    oa = _dsa_attention(qT, k, vT, qiT, ki, iwT, slopes[N_DIL_HEADS:], topk=topk)

    dil = []
    for g, (win, d) in enumerate(DIL_GROUPS):
        sl = slopes[g * HEADS_PER_DIL_GROUP:(g + 1) * HEADS_PER_DIL_GROUP]
        dil.append(_dilated_group(dil_qkv[g], win, d, sl))

    h, hb = _merge(oa, dil, gates, x2, w_a.astype(BF16), w_b.astype(BF16), w_out.astype(BF16),
                   ln1_g.reshape(1, -1), ln1_b.reshape(1, -1))
    return _ffn(hb, h, w_gate, w_up, w_down, ln2_g.reshape(1, -1), ln2_b.reshape(1, -1))


def kernel(x, w_in, w_a, w_b, w_out, ln1_g, ln1_b, w_gate, w_up, w_down, ln2_g, ln2_b):
    B, S, D = x.shape
    outs = []
    for b in range(B):
        y = x[b]
        for l in range(w_in.shape[0]):
            y = _layer(y, w_in[l], w_a[l], w_b[l], w_out[l], ln1_g[l], ln1_b[l],
                       w_gate[l], w_up[l], w_down[l], ln2_g[l], ln2_b[l])
        outs.append(y)
    return jnp.stack(outs)
```

```python
import functools
import statistics

import numpy as np
import jax
import jax.numpy as jnp
from jax import lax
from jax.experimental import pallas as pl
from jax.experimental.pallas import tpu as pltpu

F32 = jnp.float32
BF16 = jnp.bfloat16

D_MODEL = 2048
HEAD_DIM = 128
N_SPARSE_HEADS = 8
SPARSE_WIDTH = N_SPARSE_HEADS * HEAD_DIM
N_IDX_HEADS = 16
IDX_HEAD_DIM = 64
TOPK_MAX = 256
DIL_GROUPS = ((128, 1), (512, 4), (2048, 16))
HEADS_PER_DIL_GROUP = 4
N_DIL_HEADS = HEADS_PER_DIL_GROUP * len(DIL_GROUPS)
DIL_WIDTH = N_DIL_HEADS * HEAD_DIM
DIL_OUT_WIDTH = HEADS_PER_DIL_GROUP * HEAD_DIM
N_ALIBI_HEADS = N_DIL_HEADS + N_SPARSE_HEADS
BLOCK = 128
D_FF = ((-(-8 * D_MODEL // 3)) + 255) // 256 * 256
DEPTH = 1
DEEPNORM_ALPHA = (2 * DEPTH) ** 0.25
LN_EPS = 1e-5
IN_SIZES = (SPARSE_WIDTH, SPARSE_WIDTH, SPARSE_WIDTH,
            N_IDX_HEADS * IDX_HEAD_DIM, IDX_HEAD_DIM, N_IDX_HEADS,
            DIL_WIDTH, DIL_WIDTH, DIL_WIDTH, D_MODEL, D_MODEL)
IN_OFFS = tuple(int(c) for c in np.cumsum((0,) + IN_SIZES))

VMEM_LIMIT_BYTES = 56 * 1024 * 1024
LOG2_E = float(np.log2(np.e))
NEG_BIG = -1e30
F32_LOWEST = float(np.finfo(np.float32).min)
F32_HIGHEST = float(np.finfo(np.float32).max)


def _alibi_slopes():
    return [2.0 ** (-8.0 * (i + 1) / N_ALIBI_HEADS) for i in range(N_ALIBI_HEADS)]


def _cparams(*sem):
    return pltpu.CompilerParams(dimension_semantics=sem,
                                vmem_limit_bytes=VMEM_LIMIT_BYTES)


LANES = 128
ROW_ALIGN = 16
PROJ_TM = 1024
PROJ_TN = 1024
IDX_BLOCK = 128


def _x_dot_wT(x_ref, w_ref):
    return lax.dot_general(x_ref[...], w_ref[...].astype(BF16), (((1,), (1,)), ((), ())),
                           preferred_element_type=F32)


def _w_rows_spec(tn, K, row_of_step):
    return pl.BlockSpec((pl.Element(tn), pl.Element(K)),
                        lambda i, j: (pl.multiple_of(row_of_step(j), ROW_ALIGN), 0))


def _proj_kernel(x_ref, w_ref, o_ref, *, scale, sigmoid, transpose_out):
    acc = _x_dot_wT(x_ref, w_ref)
    if scale != 1.0:
        acc = acc * scale
    if sigmoid:
        acc = jax.nn.sigmoid(acc)
    if transpose_out:
        acc = acc.T
    o_ref[...] = acc.astype(o_ref.dtype)


def _project(xb, wT, row0, ncols, *, out_dtype, scale=1.0, sigmoid=False,
             transpose_out=False, name):
    S, K = xb.shape
    tm, tn = min(PROJ_TM, S), PROJ_TN
    assert row0 % ROW_ALIGN == 0 and ncols % tn == 0 and S % tm == 0
    if transpose_out:
        out_shape = jax.ShapeDtypeStruct((ncols, S), out_dtype)
        out_spec = pl.BlockSpec((tn, tm), lambda i, j: (j, i))
    else:
        out_shape = jax.ShapeDtypeStruct((S, ncols), out_dtype)
        out_spec = pl.BlockSpec((tm, tn), lambda i, j: (i, j))
    return pl.pallas_call(
        functools.partial(_proj_kernel, scale=scale, sigmoid=sigmoid,
                          transpose_out=transpose_out),
        grid=(S // tm, ncols // tn),
        in_specs=[pl.BlockSpec((tm, K), lambda i, j: (i, 0)),
                  _w_rows_spec(tn, K, lambda j: row0 + j * tn)],
        out_specs=out_spec,
        out_shape=out_shape,
        compiler_params=_cparams("parallel", "arbitrary"),
        name=name,
    )(xb, wT)


def _proj_cast_kernel(x_ref, w_ref, o_ref, xb_ref):
    xb_ref[...] = x_ref[...].astype(xb_ref.dtype)
    o_ref[...] = _x_dot_wT(xb_ref, w_ref).astype(o_ref.dtype)


def _project_and_cast(x2, wT, row0, ncols, *, name):
    S, K = x2.shape
    tm = min(PROJ_TM, S)
    assert ncols == PROJ_TN and S % tm == 0
    return pl.pallas_call(
        _proj_cast_kernel,
        grid=(S // tm, 1),
        in_specs=[pl.BlockSpec((tm, K), lambda i, j: (i, 0)),
                  _w_rows_spec(ncols, K, lambda j: row0)],
        out_specs=[pl.BlockSpec((tm, ncols), lambda i, j: (i, 0)),
                   pl.BlockSpec((tm, K), lambda i, j: (i, 0))],
        out_shape=[jax.ShapeDtypeStruct((S, ncols), BF16), jax.ShapeDtypeStruct((S, K), BF16)],
        compiler_params=_cparams("parallel", "arbitrary"),
        name=name,
    )(x2, wT)


def _proj_indexer_kernel(x_ref, wq_ref, wkw_ref, qT_ref, k_ref, wT_ref, *, q_scale, w_scale):
    qT_ref[...] = (_x_dot_wT(x_ref, wq_ref) * q_scale).T.astype(qT_ref.dtype)
    kw = _x_dot_wT(x_ref, wkw_ref)
    k_ref[...] = kw.astype(k_ref.dtype)
    wT_ref[...] = (kw * w_scale).T


def _project_indexer(xb, wT, row_q, row_kw, *, q_scale, w_scale):
    S, K = xb.shape
    tm = min(PROJ_TM, S)
    WI = N_IDX_HEADS * IDX_HEAD_DIM
    return pl.pallas_call(
        functools.partial(_proj_indexer_kernel, q_scale=q_scale, w_scale=w_scale),
        grid=(S // tm, 1),
        in_specs=[pl.BlockSpec((tm, K), lambda i, j: (i, 0)),
                  _w_rows_spec(WI, K, lambda j: row_q),
                  _w_rows_spec(IDX_BLOCK, K, lambda j: row_kw)],
        out_specs=[pl.BlockSpec((WI, tm), lambda i, j: (0, i)),
                   pl.BlockSpec((tm, IDX_BLOCK), lambda i, j: (i, 0)),
                   pl.BlockSpec((IDX_BLOCK, tm), lambda i, j: (0, i))],
        out_shape=[jax.ShapeDtypeStruct((WI, S), BF16),
                   jax.ShapeDtypeStruct((S, IDX_BLOCK), BF16),
                   jax.ShapeDtypeStruct((IDX_BLOCK, S), F32)],
        compiler_params=_cparams("parallel", "arbitrary"),
        name="proj_indexer",
    )(xb, wT, wT)


def _proj_dilated_kernel(x_ref, wq_ref, wk_ref, wv_ref, o_ref, scr_ref, *, dilation):
    for which, w_ref in enumerate((wq_ref, wk_ref, wv_ref)):
        acc = _x_dot_wT(x_ref, w_ref)
        if which == 0:
            acc = acc * (LOG2_E * HEAD_DIM ** -0.5)
        tm, tn = acc.shape
        if dilation == 1:
            o_ref[which, 0] = acc.astype(o_ref.dtype)
            continue
        for c in range(tn // LANES):
            scr_ref[which, c] = acc[:, c * LANES:(c + 1) * LANES]
        for r in range(dilation):
            for c in range(tn // LANES):
                rows = scr_ref[which, c, pl.ds(r, tm // dilation, stride=dilation), :]
                o_ref[which, r, :, c * LANES:(c + 1) * LANES] = rows.astype(o_ref.dtype)


def _project_dilated_qkv(xb, wT, row0, g, dilation):
    S, K = xb.shape
    d = dilation
    tm, tn = min(PROJ_TM, S), DIL_OUT_WIDTH
    assert S % tm == 0 and tm % (d * ROW_ALIGN) == 0

    def w_spec(which):
        return _w_rows_spec(tn, K, lambda j: row0 + which * DIL_WIDTH + g * tn)

    return pl.pallas_call(
        functools.partial(_proj_dilated_kernel, dilation=d),
        grid=(S // tm, 1),
        in_specs=[pl.BlockSpec((tm, K), lambda i, j: (i, 0)), w_spec(0), w_spec(1), w_spec(2)],
        out_specs=pl.BlockSpec((3, d, tm // d, tn), lambda i, j: (0, 0, i, 0)),
        out_shape=jax.ShapeDtypeStruct((3, d, S // d, tn), BF16),
        scratch_shapes=[pltpu.VMEM((3, tn // LANES, tm, LANES), F32)],
        compiler_params=_cparams("parallel", "arbitrary"),
        name=f"proj_dil_qkv_d{d}",
    )(xb, wT, wT, wT)


DSA_QB = 256
DSA_KC = 256
SEARCH_MAX_ITERS = 512
SEARCH_BISECT_EVERY = 8
SEARCH_FIRST_STEP = 0.0625
SEARCH_UNTESTED_TRIPS = 8
PEEL_AFTER = 5
PEEL_NEED = 2
ATTN_SLAB = 16
ATTN_HEAD_GROUP = 8
IDX_PER_ATTN_HEAD = N_IDX_HEADS // N_SPARSE_HEADS
IDX_IN_SOFTMAX = 2
MOMENT_ROWS = 64
COUNT_SLAB = 16
COUNT_ACCUMULATORS = 2


def _dsa_kernel(qi_tab, kj_tab,
                qT_ref, k_ref, vT_ref, qiT_ref, qiT_next_ref, ki_ref, wT_ref, wT_next_ref,
                pos_ref, srow_ref, zq_ref,
                o_ref,
                sc2_ref, thr_ref, m_ref, l_ref, acc_ref, s_buf, p_buf, mask_buf, mom_ref, *, topk):
    QB, KC = DSA_QB, DSA_KC
    t = pl.program_id(0)
    i = qi_tab[t]
    j = kj_tab[t]
    cur = i % 2
    sc_ref = sc2_ref.at[cur]

    @pl.when(t == 0)
    def _first():
        mom_ref[...] = jnp.zeros(mom_ref.shape, F32)

    def score_heads(acc, kic, q_ref, w_ref, heads):
        for h in heads:
            r = jnp.dot(kic, q_ref[h * IDX_HEAD_DIM:(h + 1) * IDX_HEAD_DIM, :],
                        preferred_element_type=F32)
            w_h = w_ref[IDX_HEAD_DIM + h:IDX_HEAD_DIM + h + 1, :]
            acc = acc + w_h * jnp.maximum(r, 0.0)
        return acc

    def chunk_keys(c):
        row0 = pl.multiple_of(c * KC, KC)
        return ki_ref[pl.ds(row0, KC), 0:IDX_HEAD_DIM]

    def chunk_scores(q_ref, w_ref, dst_ref, c, diagonal):
        acc = score_heads(jnp.zeros((KC, QB), F32), chunk_keys(c), q_ref, w_ref,
                          range(N_IDX_HEADS))
        return finish_scores(acc, dst_ref, c, diagonal)

    def finish_scores(acc, dst_ref, c, diagonal):
        row0 = pl.multiple_of(c * KC, KC)
        if diagonal:
            kpos = lax.broadcasted_iota(jnp.int32, (KC, QB), 0)
            qpos = lax.broadcasted_iota(jnp.int32, (KC, QB), 1)
            causal = kpos <= qpos
            live = jnp.where(causal, acc, 0.0)
            acc = jnp.where(causal, acc, -jnp.inf)
        else:
            live = acc
        dst_ref[pl.ds(row0, KC), :] = acc
        live = live[:MOMENT_ROWS, :]
        s1 = jnp.sum(live.reshape(MOMENT_ROWS // 8, 8, QB), axis=0)
        s2 = jnp.sum((live * live).reshape(MOMENT_ROWS // 8, 8, QB), axis=0)
        return s1, s2

    def count_ge(cand):
        candb = jnp.broadcast_to(cand, (COUNT_SLAB, QB))

        def body(c, parts):
            row0 = pl.multiple_of(c * (2 * KC), 2 * KC)
            parts = list(parts)
            for r in range(2 * KC // COUNT_SLAB):
                slab = sc_ref[pl.ds(row0 + r * COUNT_SLAB, COUNT_SLAB), :]
                parts[r % len(parts)] = parts[r % len(parts)] + jnp.where(slab >= candb, 1.0, 0.0)
            return tuple(parts)
        zero = jnp.zeros((COUNT_SLAB, QB), F32)
        parts = lax.fori_loop(0, (i + 2) // 2, body, (zero,) * COUNT_ACCUMULATORS)
        return jnp.sum(sum(parts), axis=0, keepdims=True)

    def max_below(bound):
        boundb = jnp.broadcast_to(bound, (COUNT_SLAB, QB))

        def body(c, parts):
            row0 = pl.multiple_of(c * (2 * KC), 2 * KC)
            parts = list(parts)
            for r in range(2 * KC // COUNT_SLAB):
                slab = sc_ref[pl.ds(row0 + r * COUNT_SLAB, COUNT_SLAB), :]
                parts[r % len(parts)] = jnp.maximum(parts[r % len(parts)],
                                                    jnp.where(slab < boundb, slab, -jnp.inf))
            return tuple(parts)
        lowest = jnp.full((COUNT_SLAB, QB), -jnp.inf, F32)
        parts = lax.fori_loop(0, (i + 2) // 2, body, (lowest,) * COUNT_ACCUMULATORS)
        return jnp.max(functools.reduce(jnp.maximum, parts), axis=0, keepdims=True)

    @pl.when(j == 0)
    def _prepare():
        d1, d2 = chunk_scores(qiT_ref, wT_ref, sc_ref, i, True)
        s1 = jnp.sum(mom_ref[cur, 0] + d1, axis=0, keepdims=True)
        s2 = jnp.sum(mom_ref[cur, 1] + d2, axis=0, keepdims=True)
        mom_ref[1 - cur] = jnp.zeros(mom_ref.shape[1:], F32)

        @pl.when(i % 2 == 0)
        def _pad():
            sc_ref[pl.ds(pl.multiple_of((i + 1) * KC, KC), KC), :] = jnp.full((KC, QB), -jnp.inf, F32)

        kf = float(topk)
        log_k = float(np.log(topk - 0.5))
        lane = lax.broadcasted_iota(jnp.int32, (1, QB), 1)
        n_causal = (i * QB + 1 + lane).astype(F32)
        n_sample = (i * MOMENT_ROWS + jnp.minimum(lane + 1, MOMENT_ROWS)).astype(F32)
        mean = s1 / n_sample
        std = jnp.sqrt(jnp.maximum(s2 / n_sample - mean * mean, 0.0))
        zq = jnp.max(zq_ref[...], axis=0, keepdims=True)
        guess = mean + zq * std
        few = n_causal <= kf
        lo = jnp.full((1, QB), F32_LOWEST, F32)
        clo = n_causal
        hi = jnp.full((1, QB), F32_HIGHEST, F32)
        chi = jnp.zeros((1, QB), F32)
        done_f = jnp.where(few | (clo == kf), 1.0, 0.0)
        zero = jnp.zeros((1, QB), F32)
        one = jnp.ones((1, QB), F32)

        def all_done(done_f):
            return jnp.min(done_f) > 0.5

        def peel_ready(st):
            it, hi, chi, done_f = st[0], st[2], st[4], st[5]
            few_left = (kf - chi <= PEEL_NEED) & (hi < F32_HIGHEST)
            ok = jnp.min(jnp.where(few_left, 1.0, done_f)) > 0.5
            return jnp.logical_and(it >= PEEL_AFTER, ok)

        def cond(st):
            return jnp.logical_and(st[0] < SEARCH_MAX_ITERS, jnp.logical_not(peel_ready(st)))

        def cond_no_peel(st):
            return jnp.logical_and(st[0] < SEARCH_MAX_ITERS, jnp.logical_not(all_done(st[5])))

        def step(st):
            it, lo, hi, clo, chi, done_f, n_up, n_dn, w_lo, w_hi, last = st
            done = done_f > 0.5
            mid = 0.5 * lo + 0.5 * hi
            f_lo = (jnp.log(clo) - log_k) * w_lo
            f_hi = (log_k - jnp.log(jnp.maximum(chi, 0.5))) * w_hi
            interp = lo + (hi - lo) * (f_lo / jnp.maximum(f_lo + f_hi, 1e-9))
            open_hi = hi >= F32_HIGHEST
            open_lo = lo <= F32_LOWEST
            cand = jnp.where(it % SEARCH_BISECT_EVERY == SEARCH_BISECT_EVERY - 1, mid, interp)
            cand = jnp.where(open_hi, lo + std * (SEARCH_FIRST_STEP * jnp.exp2(n_up)), cand)
            cand = jnp.where(open_lo, hi - std * (SEARCH_FIRST_STEP * jnp.exp2(n_dn)), cand)
            cand = jnp.where(open_lo & open_hi, guess, cand)
            bad = jnp.logical_not((cand > lo) & (cand < hi))
            cand = jnp.where(bad, mid, cand)
            cand = jnp.where(done, lo, cand)
            cnt = count_ge(cand)
            up = jnp.logical_not(done) & (cnt >= kf)
            dn = jnp.logical_not(done) & (cnt < kf)
            n_up = n_up + jnp.where(up & open_hi, 1.0, 0.0)
            n_dn = n_dn + jnp.where(dn & open_lo, 1.0, 0.0)
            w_hi = jnp.where(up, jnp.where(last == 1.0, 0.5 * w_hi, 1.0), jnp.where(dn, 1.0, w_hi))
            w_lo = jnp.where(dn, jnp.where(last == 2.0, 0.5 * w_lo, 1.0), jnp.where(up, 1.0, w_lo))
            last = jnp.where(up, 1.0, jnp.where(dn, 2.0, last))
            lo = jnp.where(up, cand, lo)
            clo = jnp.where(up, cnt, clo)
            hi = jnp.where(dn, cand, hi)
            chi = jnp.where(dn, cnt, chi)
            mid2 = 0.5 * lo + 0.5 * hi
            stuck = jnp.logical_not((mid2 > lo) & (mid2 < hi))
            done_f = jnp.where(done | (clo == kf) | stuck, 1.0, 0.0)
            return it + 1, lo, hi, clo, chi, done_f, n_up, n_dn, w_lo, w_hi, last

        def peel_cond(ps):
            return jnp.logical_and(ps[0] < PEEL_NEED + 2, jnp.logical_not(all_done(ps[5])))

        def peel_step(ps):
            n, lo, hi, clo, chi, done_f = ps
            done = done_f > 0.5
            y = max_below(hi)
            cand = jnp.where(done, lo, y)
            cnt = count_ge(cand)
            up = jnp.logical_not(done) & (cnt >= kf)
            dn = jnp.logical_not(done) & (cnt < kf)
            lo = jnp.where(up, cand, lo)
            clo = jnp.where(up, cnt, clo)
            hi = jnp.where(dn, cand, hi)
            chi = jnp.where(dn, cnt, chi)
            return n + 1, lo, hi, clo, chi, jnp.where(done | up, 1.0, 0.0)

        st = (jnp.int32(0), lo, hi, clo, chi, done_f, zero, zero, one, one, zero)
        st = lax.fori_loop(0, SEARCH_UNTESTED_TRIPS, lambda _, s: step(s), st)
        st = lax.while_loop(cond, step, st)
        ps = lax.while_loop(peel_cond, peel_step, (jnp.int32(0),) + tuple(st[1:6]))
        st = lax.while_loop(cond_no_peel, step, (st[0],) + tuple(ps[1:6]) + tuple(st[6:]))
        lo, hi, clo, chi = st[1], st[2], st[3], st[4]
        lo = jnp.where(few, F32_LOWEST, lo)

        need = kf - chi
        tied = jnp.logical_and(clo > kf, jnp.logical_not(few))

        @pl.when(jnp.max(tied.astype(F32)) > 0.5)
        def _break_ties():
            def count_eq_le(cut):
                def body(c, part):
                    row0 = pl.multiple_of(c * KC, KC)
                    tile = sc_ref[pl.ds(row0, KC), :]
                    kpos = (row0 + lax.broadcasted_iota(jnp.int32, (KC, QB), 0)).astype(F32)
                    ind = jnp.where((tile == lo) & (kpos <= cut), 1.0, 0.0)
                    return part + jnp.sum(ind.reshape(KC // 8, 8, QB), axis=0)
                part = lax.fori_loop(0, i + 1, body, jnp.zeros((8, QB), F32))
                return jnp.sum(part, axis=0, keepdims=True)

            def bis(_, st):
                a, b = st
                mid = jnp.floor(0.5 * (a + b))
                ok = count_eq_le(mid) <= need
                return jnp.where(ok, mid, a), jnp.where(ok, b, mid)

            a0 = jnp.full((1, QB), -1.0, F32)
            b0 = ((i + 1) * KC).astype(F32) + jnp.zeros((1, QB), F32)
            n_bits = int(np.ceil(np.log2(sc_ref.shape[0] + 2))) + 1
            cut, _ = lax.fori_loop(0, n_bits, bis, (a0, b0))

            def drop(c, _):
                row0 = pl.multiple_of(c * KC, KC)
                tile = sc_ref[pl.ds(row0, KC), :]
                kpos = (row0 + lax.broadcasted_iota(jnp.int32, (KC, QB), 0)).astype(F32)
                kill = tied & (tile == lo) & (kpos > cut)
                sc_ref[pl.ds(row0, KC), :] = jnp.where(kill, -jnp.inf, tile)
                return 0
            lax.fori_loop(0, i + 1, drop, 0)

        thr_ref[...] = jnp.broadcast_to(lo, thr_ref.shape)
        m_ref[...] = jnp.full(m_ref.shape, NEG_BIG, F32)
        l_ref[...] = jnp.zeros(l_ref.shape, F32)
        acc_ref[...] = jnp.zeros(acc_ref.shape, F32)

    row0 = pl.multiple_of(j * KC, KC)
    SL = ATTN_SLAB
    n_slab = KC // SL
    thr = jnp.broadcast_to(thr_ref[0:1, :], (SL, QB))
    for r in range(n_slab):
        rows = slice(r * SL, (r + 1) * SL)
        sc = sc_ref[pl.ds(row0 + r * SL, SL), :]
        mask_buf[rows, :] = jnp.where(sc >= thr, 0.0, NEG_BIG)
    posk = pos_ref[...]
    nxt_keys = chunk_keys(j)
    nxt_acc = jnp.zeros((KC, QB), F32)
    for g0 in range(0, N_SPARSE_HEADS, ATTN_HEAD_GROUP):
        heads = range(g0, g0 + ATTN_HEAD_GROUP)
        col_max = {}
        for h in heads:
            hs = slice(h * HEAD_DIM, (h + 1) * HEAD_DIM)
            lhs = jnp.concatenate([k_ref[:, hs], posk], axis=1)
            rhs = jnp.concatenate([qT_ref[hs, :], srow_ref[h]], axis=0)
            s = jnp.dot(lhs, rhs, preferred_element_type=F32)
            cm = [jnp.full((SL, QB), NEG_BIG, F32)] * 2
            for r in range(n_slab):
                rows = slice(r * SL, (r + 1) * SL)
                blk = s[rows, :] + mask_buf[rows, :]
                s_buf[h, rows, :] = blk
                cm[r % 2] = jnp.maximum(cm[r % 2], blk)
            col_max[h] = jnp.max(jnp.maximum(cm[0], cm[1]), axis=0, keepdims=True)
        alphas = {}
        for h in heads:
            nxt_acc = score_heads(nxt_acc, nxt_keys, qiT_next_ref, wT_next_ref,
                                  range(h * IDX_PER_ATTN_HEAD, h * IDX_PER_ATTN_HEAD + IDX_IN_SOFTMAX))
            m_old = m_ref[h]
            m_new = jnp.maximum(m_old, col_max[h])
            alphas[h] = jnp.exp2(m_old - m_new)
            m_ref[h] = m_new
            mb = jnp.concatenate([m_new] * (SL // 8), axis=0)
            for r in range(n_slab):
                rows = slice(r * SL, (r + 1) * SL)
                p_buf[h, rows, :] = jnp.exp2(s_buf[h, rows, :] - mb).astype(BF16)
        ones_rows = jnp.ones((ROW_ALIGN, KC), BF16)
        for h in heads:
            nxt_acc = score_heads(nxt_acc, nxt_keys, qiT_next_ref, wT_next_ref,
                                  range(h * IDX_PER_ATTN_HEAD + IDX_IN_SOFTMAX,
                                        (h + 1) * IDX_PER_ATTN_HEAD))
            hs = slice(h * HEAD_DIM, (h + 1) * HEAD_DIM)
            v_ext = jnp.concatenate([vT_ref[hs, :], ones_rows], axis=0)
            pv = jnp.dot(v_ext, p_buf[h], preferred_element_type=F32)
            acc_ref[hs, :] = alphas[h][0:1, :] * acc_ref[hs, :] + pv[:HEAD_DIM, :]
            l_ref[h] = alphas[h] * l_ref[h] + pv[HEAD_DIM:HEAD_DIM + 1, :]

    n1, n2 = finish_scores(nxt_acc, sc2_ref.at[1 - cur], j, False)
    mom_ref[1 - cur, 0] += n1
    mom_ref[1 - cur, 1] += n2

    @pl.when(j == i)
    def _finish():
        for h in range(N_SPARSE_HEADS):
            hs = slice(h * HEAD_DIM, (h + 1) * HEAD_DIM)
            o = acc_ref[hs, :] / l_ref[h, 0:1, :]
            o_ref[:, hs] = o.T.astype(o_ref.dtype)


N_SLOPE_PIECES = 3
POS_SPLIT = 128


def _alibi_mxu_operands(S, slopes_log2):
    assert S <= POS_SPLIT * 256
    kpos = jnp.arange(S, dtype=jnp.int32)[:, None]
    col = jnp.arange(128, dtype=jnp.int32)[None, :]
    part = jnp.where(col % 2 == 0, kpos // POS_SPLIT, kpos % POS_SPLIT)
    pos = jnp.where(col < 2 * N_SLOPE_PIECES, part, 0).astype(BF16)
    rows = np.zeros((len(slopes_log2), 128), np.float32)
    for h, s in enumerate(slopes_log2):
        rest = np.float64(s)
        for p in range(N_SLOPE_PIECES):
            piece = np.float64(np.asarray(rest, np.float32).astype(jnp.bfloat16).astype(np.float32))
            rows[h, 2 * p] = POS_SPLIT * piece
            rows[h, 2 * p + 1] = piece
            rest = rest - piece
    srow = jnp.broadcast_to(jnp.asarray(rows, F32)[:, :, None],
                            (len(slopes_log2), 128, DSA_QB)).astype(BF16)
    return pos, srow


def _dsa_attention(qT, k, vT, qiT, ki, wT, slopes, *, topk):
    S = k.shape[0]
    QB, KC = DSA_QB, DSA_KC
    assert S % (2 * QB) == 0 and QB == KC
    nq = S // QB
    pos, srow = _alibi_mxu_operands(S, [s * LOG2_E for s in slopes])
    quantile = statistics.NormalDist().inv_cdf
    zq = np.array([quantile(1.0 - (topk - 0.5) / n) if n > topk else 0.0
                   for n in range(1, S + 1)], np.float32)
    zq = np.broadcast_to(zq, (8, S))
    qi_tab = np.concatenate([np.full(i + 1, i, np.int32) for i in range(nq)])
    kj_tab = np.concatenate([np.arange(i + 1, dtype=np.int32) for i in range(nq)])
    n_steps = int(qi_tab.shape[0])
    W = N_SPARSE_HEADS * HEAD_DIM
    WI = N_IDX_HEADS * IDX_HEAD_DIM
    grid_spec = pltpu.PrefetchScalarGridSpec(
        num_scalar_prefetch=2,
        grid=(n_steps,),
        in_specs=[
            pl.BlockSpec((W, QB), lambda t, qi, kj: (0, qi[t])),
            pl.BlockSpec((KC, W), lambda t, qi, kj: (kj[t], 0)),
            pl.BlockSpec((W, KC), lambda t, qi, kj: (0, kj[t])),
            pl.BlockSpec((WI, QB), lambda t, qi, kj: (0, qi[t])),
            pl.BlockSpec((WI, QB), lambda t, qi, kj: (0, jnp.minimum(qi[t] + 1, nq - 1))),
            pl.BlockSpec((S, IDX_BLOCK), lambda t, qi, kj: (0, 0)),
            pl.BlockSpec((IDX_BLOCK, QB), lambda t, qi, kj: (0, qi[t])),
            pl.BlockSpec((IDX_BLOCK, QB), lambda t, qi, kj: (0, jnp.minimum(qi[t] + 1, nq - 1))),
            pl.BlockSpec((KC, LANES), lambda t, qi, kj: (kj[t], 0)),
            pl.BlockSpec(srow.shape, lambda t, qi, kj: (0, 0, 0)),
            pl.BlockSpec((8, QB), lambda t, qi, kj: (0, qi[t])),
        ],
        out_specs=pl.BlockSpec((QB, W), lambda t, qi, kj: (qi[t], 0)),
        scratch_shapes=[
            pltpu.VMEM((2, S, QB), F32),
            pltpu.VMEM((8, QB), F32),
            pltpu.VMEM((N_SPARSE_HEADS, 8, QB), F32),
            pltpu.VMEM((N_SPARSE_HEADS, 8, QB), F32),
            pltpu.VMEM((W, QB), F32),
            pltpu.VMEM((N_SPARSE_HEADS, KC, QB), F32),
            pltpu.VMEM((N_SPARSE_HEADS, KC, QB), BF16),
            pltpu.VMEM((KC, QB), F32),
            pltpu.VMEM((2, 2, 8, QB), F32),
        ],
    )
    return pl.pallas_call(
        functools.partial(_dsa_kernel, topk=topk),
        grid_spec=grid_spec,
        out_shape=jax.ShapeDtypeStruct((S, W), BF16),
        compiler_params=_cparams("arbitrary"),
        name="dsa_attention",
    )(jnp.asarray(qi_tab), jnp.asarray(kj_tab), qT, k, vT, qiT, qiT, ki, wT, wT, pos, srow,
      jnp.asarray(zq))


DIL_BLOCKS_PER_STEP = 16
def _dil_kernel(bias_ref, q_ref, kc_ref, kp_ref, vc_ref, vp_ref, o_ref, lse_ref, *, n_blocks):
    first = jnp.where(pl.program_id(1) == 0, 1, 0)
    Q = BLOCK
    for b in range(n_blocks):
        rows = slice(b * Q, (b + 1) * Q)
        prev_rows = slice((b - 1) * Q, b * Q)
        for h in range(HEADS_PER_DIL_GROUP):
            hs = slice(h * HEAD_DIM, (h + 1) * HEAD_DIM)
            q = q_ref[rows, hs]
            k_prev = kp_ref[:, hs] if b == 0 else kc_ref[prev_rows, hs]
            v_prev = vp_ref[:, hs] if b == 0 else vc_ref[prev_rows, hs]
            kk = jnp.concatenate([k_prev, kc_ref[rows, hs]], axis=0)
            vv = jnp.concatenate([v_prev, vc_ref[rows, hs]], axis=0)
            bias = bias_ref[first, h] if b == 0 else bias_ref[0, h]
            s = lax.dot_general(q, kk, (((1,), (1,)), ((), ())),
                                preferred_element_type=F32) + bias
            mx = jnp.max(s, axis=-1, keepdims=True)
            p = jnp.exp2(s - mx)
            l = jnp.sum(p, axis=-1, keepdims=True)
            o = jnp.dot(p.astype(BF16), vv, preferred_element_type=F32) / l
            o_ref[rows, hs] = o
            lse_ref[rows, hs] = jnp.broadcast_to(mx + jnp.log2(l), (Q, HEAD_DIM))


def _dilated_bias(span, dilation, slopes_log2):
    Q = BLOCK
    step = Q + np.arange(Q)[:, None] - np.arange(2 * Q)[None, :]
    band = (step >= 0) & (step <= span)
    has_prev = np.stack([np.ones_like(band), np.broadcast_to(np.arange(2 * Q)[None, :] >= Q, band.shape)])
    dist = (step * dilation).astype(np.float64)
    bias = -np.asarray(slopes_log2, np.float64)[None, :, None, None] * dist[None, None]
    keep = (band[None] & has_prev)[:, None]
    return jnp.asarray(np.where(keep, bias, NEG_BIG), F32)


def _dilated_group(qkv, window, dilation, slopes_g):
    _, d, n_sub, GW = qkv.shape
    assert d == dilation
    span = window // d
    nbs = min(DIL_BLOCKS_PER_STEP, n_sub // BLOCK)
    rows = nbs * BLOCK
    assert span <= BLOCK and n_sub % rows == 0
    bias = _dilated_bias(span, d, [s * LOG2_E for s in slopes_g])

    def cur(which):
        return pl.BlockSpec((None, None, rows, GW), lambda r, m: (which, r, m, 0))

    def prev(which):
        return pl.BlockSpec((None, None, BLOCK, GW),
                            lambda r, m: (which, r, jnp.maximum(m * nbs - 1, 0), 0))

    out_spec = pl.BlockSpec((None, rows, GW), lambda r, m: (r, m, 0))
    return pl.pallas_call(
        functools.partial(_dil_kernel, n_blocks=nbs),
        grid=(d, n_sub // rows),
        in_specs=[pl.BlockSpec(bias.shape, lambda r, m: (0, 0, 0, 0)),
                  cur(0), cur(1), prev(1), cur(2), prev(2)],
        out_specs=[out_spec, out_spec],
        out_shape=[jax.ShapeDtypeStruct((d, n_sub, GW), F32)] * 2,
        compiler_params=_cparams("parallel", "arbitrary"),
        name=f"dilated_attention_d{d}",
    )(bias, qkv, qkv, qkv, qkv, qkv)


def _layer_norm(y, g, b):
    mu = jnp.mean(y, axis=-1, keepdims=True)
    yc = y - mu
    var = jnp.mean(yc * yc, axis=-1, keepdims=True)
    return yc * lax.rsqrt(var + LN_EPS) * g + b


def _merge_kernel(oa_ref, o1_ref, o2_ref, o3_ref, l1_ref, l2_ref, l3_ref,
                  ga_ref, gb_ref, x_ref, wa_ref, wb_ref, wo_ref, g_ref, b_ref,
                  h_ref, hb_ref, nat_ref):
    tm = x_ref.shape[0]
    n_slab = DIL_OUT_WIDTH // LANES

    def natural(ref, slot):
        d = ref.shape[0]
        if d == 1:
            return [ref[0, :, c * LANES:(c + 1) * LANES] for c in range(n_slab)]
        for r in range(d):
            for c in range(n_slab):
                nat_ref[slot, c, pl.ds(r, tm // d, stride=d), :] = ref[r, :, c * LANES:(c + 1) * LANES]
        return [nat_ref[slot, c] for c in range(n_slab)]

    o1, l1 = natural(o1_ref, 0), natural(l1_ref, 1)
    o2, l2 = natural(o2_ref, 2), natural(l2_ref, 3)
    o3, l3 = natural(o3_ref, 4), natural(l3_ref, 5)
    slabs = []
    for c in range(n_slab):
        lm = jnp.maximum(jnp.maximum(l1[c], l2[c]), l3[c])
        e1, e2, e3 = jnp.exp2(l1[c] - lm), jnp.exp2(l2[c] - lm), jnp.exp2(l3[c] - lm)
        den = e1 + e2 + e3
        slabs.append((e1 / den) * o1[c] + (e2 / den) * o2[c] + (e3 / den) * o3[c])
    ob = jnp.concatenate(slabs, axis=1)
    ya = jnp.dot(oa_ref[...], wa_ref[...], preferred_element_type=F32)
    yb = jnp.dot(ob.astype(BF16), wb_ref[...], preferred_element_type=F32)
    merged = ga_ref[...] * ya + gb_ref[...] * yb
    mix = jnp.dot(merged.astype(BF16), wo_ref[...], preferred_element_type=F32)
    h = _layer_norm(DEEPNORM_ALPHA * x_ref[...] + mix, g_ref[...], b_ref[...])
    h_ref[...] = h
    hb_ref[...] = h.astype(BF16)


MERGE_TM = 256


def _merge(oa, dil, gates, x2, wa, wb, wo, ln_g, ln_b):
    S = x2.shape[0]
    tm = MERGE_TM
    (o1, l1), (o2, l2), (o3, l3) = dil
    GW = DIL_OUT_WIDTH

    def rows(w, col=0):
        return pl.BlockSpec((tm, w), lambda i: (i, col))

    def residue_major(a):
        d = a.shape[0]
        assert tm % (8 * d) == 0
        return pl.BlockSpec((d, tm // d, GW), lambda i: (0, i, 0))

    def whole(a):
        return pl.BlockSpec(a.shape, lambda i: (0,) * a.ndim, pipeline_mode=pl.Buffered(1))

    return pl.pallas_call(
        _merge_kernel,
        grid=(S // tm,),
        in_specs=[rows(SPARSE_WIDTH),
                  residue_major(o1), residue_major(o2), residue_major(o3),
                  residue_major(l1), residue_major(l2), residue_major(l3),
                  rows(D_MODEL, 0), rows(D_MODEL, 1), rows(D_MODEL),
                  whole(wa), whole(wb), whole(wo), whole(ln_g), whole(ln_b)],
        out_specs=[rows(D_MODEL), rows(D_MODEL)],
        out_shape=[jax.ShapeDtypeStruct((S, D_MODEL), F32),
                   jax.ShapeDtypeStruct((S, D_MODEL), BF16)],
        scratch_shapes=[pltpu.VMEM((6, GW // LANES, tm, LANES), F32)],
        compiler_params=_cparams("parallel"),
        name="merge_outproj_ln",
    )(oa, o1, o2, o3, l1, l2, l3, gates, gates, x2, wa, wb, wo, ln_g, ln_b)


FFN_TM = 1024
FFN_UP_TN = 512
FFN_DOWN_TN = 256


def _ffn_up_kernel(hb_ref, wg_ref, wu_ref, a_ref):
    hb = hb_ref[...]
    gate = jnp.dot(hb, wg_ref[...].astype(BF16), preferred_element_type=F32)
    up = jnp.dot(hb, wu_ref[...].astype(BF16), preferred_element_type=F32)
    a_ref[...] = (gate * jax.nn.sigmoid(gate) * up).astype(a_ref.dtype)


def _ffn_down_kernel(a_ref, h_ref, wd_ref, g_ref, b_ref, o_ref):
    j = pl.program_id(1)
    tn = wd_ref.shape[1]
    cols = pl.ds(pl.multiple_of(j * tn, tn), tn)
    o_ref[:, cols] = jnp.dot(a_ref[...], wd_ref[...].astype(BF16), preferred_element_type=F32)

    @pl.when(j == pl.num_programs(1) - 1)
    def _():
        o_ref[...] = _layer_norm(DEEPNORM_ALPHA * h_ref[...] + o_ref[...],
                                 g_ref[...], b_ref[...])


def _ffn(hb, h, wg, wu, wd, ln_g, ln_b):
    S = h.shape[0]
    F = wg.shape[1]
    tm = min(FFN_TM, S)
    assert S % tm == 0 and F % FFN_UP_TN == 0 and D_MODEL % FFN_DOWN_TN == 0
    act = pl.pallas_call(
        _ffn_up_kernel,
        grid=(S // tm, F // FFN_UP_TN),
        in_specs=[pl.BlockSpec((tm, D_MODEL), lambda i, j: (i, 0)),
                  pl.BlockSpec((D_MODEL, FFN_UP_TN), lambda i, j: (0, j)),
                  pl.BlockSpec((D_MODEL, FFN_UP_TN), lambda i, j: (0, j))],
        out_specs=pl.BlockSpec((tm, FFN_UP_TN), lambda i, j: (i, j)),
        out_shape=jax.ShapeDtypeStruct((S, F), BF16),
        compiler_params=_cparams("parallel", "arbitrary"),
        name="ffn_gate_up",
    )(hb, wg, wu)

    def per_row_tile(width):
        return pl.BlockSpec((tm, width), lambda i, j: (i, 0), pipeline_mode=pl.Buffered(1))

    return pl.pallas_call(
        _ffn_down_kernel,
        grid=(S // tm, D_MODEL // FFN_DOWN_TN),
        in_specs=[per_row_tile(F), per_row_tile(D_MODEL),
                  pl.BlockSpec((F, FFN_DOWN_TN), lambda i, j: (0, j)),
                  pl.BlockSpec((1, D_MODEL), lambda i, j: (0, 0)),
                  pl.BlockSpec((1, D_MODEL), lambda i, j: (0, 0))],
        out_specs=pl.BlockSpec((tm, D_MODEL), lambda i, j: (i, 0)),
        out_shape=jax.ShapeDtypeStruct((S, D_MODEL), F32),
        compiler_params=_cparams("parallel", "arbitrary"),
        name="ffn_down_ln",
    )(act, h, wd, ln_g, ln_b)


def _mixer_inputs(x2, w_in):
    O = IN_OFFS
    wT = jnp.transpose(w_in)
    k, xb = _project_and_cast(x2, wT, O[1], SPARSE_WIDTH, name="proj_ak")
    qT = _project(xb, wT, O[0], SPARSE_WIDTH, out_dtype=BF16,
                  scale=LOG2_E * HEAD_DIM ** -0.5, transpose_out=True, name="proj_aqT")
    vT = _project(xb, wT, O[2], SPARSE_WIDTH, out_dtype=BF16, transpose_out=True,
                  name="proj_avT")
    qiT, ki, iwT = _project_indexer(xb, wT, O[3], O[4], q_scale=IDX_HEAD_DIM ** -0.5,
                                    w_scale=N_IDX_HEADS ** -0.5)
    dil_qkv = [_project_dilated_qkv(xb, wT, O[6], g, d) for g, (_, d) in enumerate(DIL_GROUPS)]
    gates = _project(xb, wT, O[9], 2 * D_MODEL, out_dtype=F32, sigmoid=True, name="proj_gates")
    return qT, k, vT, qiT, ki, iwT, dil_qkv, gates


def _layer(x2, w_in, w_a, w_b, w_out, ln1_g, ln1_b, w_gate, w_up, w_down, ln2_g, ln2_b):
    S = x2.shape[0]
    slopes = _alibi_slopes()
    qT, k, vT, qiT, ki, iwT, dil_qkv, gates = _mixer_inputs(x2, w_in)

    topk = min(TOPK_MAX, S // 4)
    oa = _dsa_attention(qT, k, vT, qiT, ki, iwT, slopes[N_DIL_HEADS:], topk=topk)

    dil = []
    for g, (win, d) in enumerate(DIL_GROUPS):
        sl = slopes[g * HEADS_PER_DIL_GROUP:(g + 1) * HEADS_PER_DIL_GROUP]
        dil.append(_dilated_group(dil_qkv[g], win, d, sl))

    h, hb = _merge(oa, dil, gates, x2, w_a.astype(BF16), w_b.astype(BF16), w_out.astype(BF16),
                   ln1_g.reshape(1, -1), ln1_b.reshape(1, -1))
    return _ffn(hb, h, w_gate, w_up, w_down, ln2_g.reshape(1, -1), ln2_b.reshape(1, -1))


def kernel(x, w_in, w_a, w_b, w_out, ln1_g, ln1_b, w_gate, w_up, w_down, ln2_g, ln2_b):
    B, S, D = x.shape
    outs = []
    for b in range(B):
        y = x[b]
        for l in range(w_in.shape[0]):
            y = _layer(y, w_in[l], w_a[l], w_b[l], w_out[l], ln1_g[l], ln1_b[l],
                       w_gate[l], w_up[l], w_down[l], ln2_g[l], ln2_b[l])
        outs.append(y)
    return jnp.stack(outs)
```

```python
import functools
import statistics

import numpy as np
import jax
import jax.numpy as jnp
from jax import lax
from jax.experimental import pallas as pl
from jax.experimental.pallas import tpu as pltpu

F32 = jnp.float32
BF16 = jnp.bfloat16

D_MODEL = 2048
HEAD_DIM = 128
N_SPARSE_HEADS = 8
SPARSE_WIDTH = N_SPARSE_HEADS * HEAD_DIM
N_IDX_HEADS = 16
IDX_HEAD_DIM = 64
TOPK_MAX = 256
DIL_GROUPS = ((128, 1), (512, 4), (2048, 16))
HEADS_PER_DIL_GROUP = 4
N_DIL_HEADS = HEADS_PER_DIL_GROUP * len(DIL_GROUPS)
DIL_WIDTH = N_DIL_HEADS * HEAD_DIM
DIL_OUT_WIDTH = HEADS_PER_DIL_GROUP * HEAD_DIM
N_ALIBI_HEADS = N_DIL_HEADS + N_SPARSE_HEADS
BLOCK = 128
D_FF = ((-(-8 * D_MODEL // 3)) + 255) // 256 * 256
DEPTH = 1
DEEPNORM_ALPHA = (2 * DEPTH) ** 0.25
LN_EPS = 1e-5
IN_SIZES = (SPARSE_WIDTH, SPARSE_WIDTH, SPARSE_WIDTH,
            N_IDX_HEADS * IDX_HEAD_DIM, IDX_HEAD_DIM, N_IDX_HEADS,
            DIL_WIDTH, DIL_WIDTH, DIL_WIDTH, D_MODEL, D_MODEL)
IN_OFFS = tuple(int(c) for c in np.cumsum((0,) + IN_SIZES))

VMEM_LIMIT_BYTES = 56 * 1024 * 1024
LOG2_E = float(np.log2(np.e))
NEG_BIG = -1e30
F32_LOWEST = float(np.finfo(np.float32).min)
F32_HIGHEST = float(np.finfo(np.float32).max)


def _alibi_slopes():
    return [2.0 ** (-8.0 * (i + 1) / N_ALIBI_HEADS) for i in range(N_ALIBI_HEADS)]


def _cparams(*sem):
    return pltpu.CompilerParams(dimension_semantics=sem,
                                vmem_limit_bytes=VMEM_LIMIT_BYTES)


LANES = 128
ROW_ALIGN = 16
PROJ_TM = 1024
PROJ_TN = 1024
IDX_BLOCK = 128


def _x_dot_wT(x_ref, w_ref):
    return lax.dot_general(x_ref[...], w_ref[...].astype(BF16), (((1,), (1,)), ((), ())),
                           preferred_element_type=F32)


def _w_rows_spec(tn, K, row_of_step):
    return pl.BlockSpec((pl.Element(tn), pl.Element(K)),
                        lambda i, j: (pl.multiple_of(row_of_step(j), ROW_ALIGN), 0))


def _proj_kernel(x_ref, w_ref, o_ref, *, scale, sigmoid, transpose_out):
    acc = _x_dot_wT(x_ref, w_ref)
    if scale != 1.0:
        acc = acc * scale
    if sigmoid:
        acc = jax.nn.sigmoid(acc)
    if transpose_out:
        acc = acc.T
    o_ref[...] = acc.astype(o_ref.dtype)


def _project(xb, wT, row0, ncols, *, out_dtype, scale=1.0, sigmoid=False,
             transpose_out=False, name):
    S, K = xb.shape
    tm, tn = min(PROJ_TM, S), PROJ_TN
    assert row0 % ROW_ALIGN == 0 and ncols % tn == 0 and S % tm == 0
    if transpose_out:
        out_shape = jax.ShapeDtypeStruct((ncols, S), out_dtype)
        out_spec = pl.BlockSpec((tn, tm), lambda i, j: (j, i))
    else:
        out_shape = jax.ShapeDtypeStruct((S, ncols), out_dtype)
        out_spec = pl.BlockSpec((tm, tn), lambda i, j: (i, j))
    return pl.pallas_call(
        functools.partial(_proj_kernel, scale=scale, sigmoid=sigmoid,
                          transpose_out=transpose_out),
        grid=(S // tm, ncols // tn),
        in_specs=[pl.BlockSpec((tm, K), lambda i, j: (i, 0)),
                  _w_rows_spec(tn, K, lambda j: row0 + j * tn)],
        out_specs=out_spec,
        out_shape=out_shape,
        compiler_params=_cparams("parallel", "arbitrary"),
        name=name,
    )(xb, wT)


def _proj_cast_kernel(x_ref, w_ref, o_ref, xb_ref):
    xb_ref[...] = x_ref[...].astype(xb_ref.dtype)
    o_ref[...] = _x_dot_wT(xb_ref, w_ref).astype(o_ref.dtype)


def _project_and_cast(x2, wT, row0, ncols, *, name):
    S, K = x2.shape
    tm = min(PROJ_TM, S)
    assert ncols == PROJ_TN and S % tm == 0
    return pl.pallas_call(
        _proj_cast_kernel,
        grid=(S // tm, 1),
        in_specs=[pl.BlockSpec((tm, K), lambda i, j: (i, 0)),
                  _w_rows_spec(ncols, K, lambda j: row0)],
        out_specs=[pl.BlockSpec((tm, ncols), lambda i, j: (i, 0)),
                   pl.BlockSpec((tm, K), lambda i, j: (i, 0))],
        out_shape=[jax.ShapeDtypeStruct((S, ncols), BF16), jax.ShapeDtypeStruct((S, K), BF16)],
        compiler_params=_cparams("parallel", "arbitrary"),
        name=name,
    )(x2, wT)


def _proj_indexer_kernel(x_ref, wq_ref, wkw_ref, qT_ref, k_ref, wT_ref, *, q_scale, w_scale):
    qT_ref[...] = (_x_dot_wT(x_ref, wq_ref) * q_scale).T.astype(qT_ref.dtype)
    kw = _x_dot_wT(x_ref, wkw_ref)
    k_ref[...] = kw.astype(k_ref.dtype)
    wT_ref[...] = (kw * w_scale).T


def _project_indexer(xb, wT, row_q, row_kw, *, q_scale, w_scale):
    S, K = xb.shape
    tm = min(PROJ_TM, S)
    WI = N_IDX_HEADS * IDX_HEAD_DIM
    return pl.pallas_call(
        functools.partial(_proj_indexer_kernel, q_scale=q_scale, w_scale=w_scale),
        grid=(S // tm, 1),
        in_specs=[pl.BlockSpec((tm, K), lambda i, j: (i, 0)),
                  _w_rows_spec(WI, K, lambda j: row_q),
                  _w_rows_spec(IDX_BLOCK, K, lambda j: row_kw)],
        out_specs=[pl.BlockSpec((WI, tm), lambda i, j: (0, i)),
                   pl.BlockSpec((tm, IDX_BLOCK), lambda i, j: (i, 0)),
                   pl.BlockSpec((IDX_BLOCK, tm), lambda i, j: (0, i))],
        out_shape=[jax.ShapeDtypeStruct((WI, S), BF16),
                   jax.ShapeDtypeStruct((S, IDX_BLOCK), BF16),
                   jax.ShapeDtypeStruct((IDX_BLOCK, S), F32)],
        compiler_params=_cparams("parallel", "arbitrary"),
        name="proj_indexer",
    )(xb, wT, wT)


def _proj_dilated_kernel(x_ref, wq_ref, wk_ref, wv_ref, o_ref, scr_ref, *, dilation):
    for which, w_ref in enumerate((wq_ref, wk_ref, wv_ref)):
        acc = _x_dot_wT(x_ref, w_ref)
        if which == 0:
            acc = acc * (LOG2_E * HEAD_DIM ** -0.5)
        tm, tn = acc.shape
        if dilation == 1:
            o_ref[which, 0] = acc.astype(o_ref.dtype)
            continue
        for c in range(tn // LANES):
            scr_ref[which, c] = acc[:, c * LANES:(c + 1) * LANES]
        for r in range(dilation):
            for c in range(tn // LANES):
                rows = scr_ref[which, c, pl.ds(r, tm // dilation, stride=dilation), :]
                o_ref[which, r, :, c * LANES:(c + 1) * LANES] = rows.astype(o_ref.dtype)


def _project_dilated_qkv(xb, wT, row0, g, dilation):
    S, K = xb.shape
    d = dilation
    tm, tn = min(PROJ_TM, S), DIL_OUT_WIDTH
    assert S % tm == 0 and tm % (d * ROW_ALIGN) == 0

    def w_spec(which):
        return _w_rows_spec(tn, K, lambda j: row0 + which * DIL_WIDTH + g * tn)

    return pl.pallas_call(
        functools.partial(_proj_dilated_kernel, dilation=d),
        grid=(S // tm, 1),
        in_specs=[pl.BlockSpec((tm, K), lambda i, j: (i, 0)), w_spec(0), w_spec(1), w_spec(2)],
        out_specs=pl.BlockSpec((3, d, tm // d, tn), lambda i, j: (0, 0, i, 0)),
        out_shape=jax.ShapeDtypeStruct((3, d, S // d, tn), BF16),
        scratch_shapes=[pltpu.VMEM((3, tn // LANES, tm, LANES), F32)],
        compiler_params=_cparams("parallel", "arbitrary"),
        name=f"proj_dil_qkv_d{d}",
    )(xb, wT, wT, wT)


DSA_QB = 256
DSA_KC = 256
SEARCH_MAX_ITERS = 512
SEARCH_BISECT_EVERY = 8
SEARCH_FIRST_STEP = 0.0625
SEARCH_UNTESTED_TRIPS = 8
PEEL_AFTER = 5
PEEL_NEED = 2
ATTN_SLAB = 16
ATTN_HEAD_GROUP = 8
IDX_PER_ATTN_HEAD = N_IDX_HEADS // N_SPARSE_HEADS
IDX_IN_SOFTMAX = 2
MOMENT_ROWS = 64
COUNT_SLAB = 16
COUNT_ACCUMULATORS = 2


def _dsa_kernel(qi_tab, kj_tab,
                qT_ref, k_ref, vT_ref, qiT_ref, qiT_next_ref, ki_ref, wT_ref, wT_next_ref,
                pos_ref, srow_ref, zq_ref,
                o_ref,
                sc2_ref, thr_ref, m_ref, l_ref, acc_ref, s_buf, p_buf, mask_buf, mom_ref, *, topk):
    QB, KC = DSA_QB, DSA_KC
    t = pl.program_id(0)
    i = qi_tab[t]
    j = kj_tab[t]
    cur = i % 2
    sc_ref = sc2_ref.at[cur]

    @pl.when(t == 0)
    def _first():
        mom_ref[...] = jnp.zeros(mom_ref.shape, F32)

    def score_heads(acc, kic, q_ref, w_ref, heads):
        for h in heads:
            r = jnp.dot(kic, q_ref[h * IDX_HEAD_DIM:(h + 1) * IDX_HEAD_DIM, :],
                        preferred_element_type=F32)
            w_h = w_ref[IDX_HEAD_DIM + h:IDX_HEAD_DIM + h + 1, :]
            acc = acc + w_h * jnp.maximum(r, 0.0)
        return acc

    def chunk_keys(c):
        row0 = pl.multiple_of(c * KC, KC)
        return ki_ref[pl.ds(row0, KC), 0:IDX_HEAD_DIM]

    def chunk_scores(q_ref, w_ref, dst_ref, c, diagonal):
        acc = score_heads(jnp.zeros((KC, QB), F32), chunk_keys(c), q_ref, w_ref,
                          range(N_IDX_HEADS))
        return finish_scores(acc, dst_ref, c, diagonal)

    def finish_scores(acc, dst_ref, c, diagonal):
        row0 = pl.multiple_of(c * KC, KC)
        if diagonal:
            kpos = lax.broadcasted_iota(jnp.int32, (KC, QB), 0)
            qpos = lax.broadcasted_iota(jnp.int32, (KC, QB), 1)
            causal = kpos <= qpos
            live = jnp.where(causal, acc, 0.0)
            acc = jnp.where(causal, acc, -jnp.inf)
        else:
            live = acc
        dst_ref[pl.ds(row0, KC), :] = acc
        live = live[:MOMENT_ROWS, :]
        s1 = jnp.sum(live.reshape(MOMENT_ROWS // 8, 8, QB), axis=0)
        s2 = jnp.sum((live * live).reshape(MOMENT_ROWS // 8, 8, QB), axis=0)
        return s1, s2

    def count_ge(cand):
        candb = jnp.broadcast_to(cand, (COUNT_SLAB, QB))

        def body(c, parts):
            row0 = pl.multiple_of(c * (2 * KC), 2 * KC)
            parts = list(parts)
            for r in range(2 * KC // COUNT_SLAB):
                slab = sc_ref[pl.ds(row0 + r * COUNT_SLAB, COUNT_SLAB), :]
                parts[r % len(parts)] = parts[r % len(parts)] + jnp.where(slab >= candb, 1.0, 0.0)
            return tuple(parts)
        zero = jnp.zeros((COUNT_SLAB, QB), F32)
        parts = lax.fori_loop(0, (i + 2) // 2, body, (zero,) * COUNT_ACCUMULATORS)
        return jnp.sum(sum(parts), axis=0, keepdims=True)

    def max_below(bound):
        boundb = jnp.broadcast_to(bound, (COUNT_SLAB, QB))

        def body(c, parts):
            row0 = pl.multiple_of(c * (2 * KC), 2 * KC)
            parts = list(parts)
            for r in range(2 * KC // COUNT_SLAB):
                slab = sc_ref[pl.ds(row0 + r * COUNT_SLAB, COUNT_SLAB), :]
                parts[r % len(parts)] = jnp.maximum(parts[r % len(parts)],
                                                    jnp.where(slab < boundb, slab, -jnp.inf))
            return tuple(parts)
        lowest = jnp.full((COUNT_SLAB, QB), -jnp.inf, F32)
        parts = lax.fori_loop(0, (i + 2) // 2, body, (lowest,) * COUNT_ACCUMULATORS)
        return jnp.max(functools.reduce(jnp.maximum, parts), axis=0, keepdims=True)

    @pl.when(j == 0)
    def _prepare():
        d1, d2 = chunk_scores(qiT_ref, wT_ref, sc_ref, i, True)
        s1 = jnp.sum(mom_ref[cur, 0] + d1, axis=0, keepdims=True)
        s2 = jnp.sum(mom_ref[cur, 1] + d2, axis=0, keepdims=True)
        mom_ref[1 - cur] = jnp.zeros(mom_ref.shape[1:], F32)

        @pl.when(i % 2 == 0)
        def _pad():
            sc_ref[pl.ds(pl.multiple_of((i + 1) * KC, KC), KC), :] = jnp.full((KC, QB), -jnp.inf, F32)

        kf = float(topk)
        log_k = float(np.log(topk - 0.5))
        lane = lax.broadcasted_iota(jnp.int32, (1, QB), 1)
        n_causal = (i * QB + 1 + lane).astype(F32)
        n_sample = (i * MOMENT_ROWS + jnp.minimum(lane + 1, MOMENT_ROWS)).astype(F32)
        mean = s1 / n_sample
        std = jnp.sqrt(jnp.maximum(s2 / n_sample - mean * mean, 0.0))
        zq = jnp.max(zq_ref[...], axis=0, keepdims=True)
        guess = mean + zq * std
        few = n_causal <= kf
        lo = jnp.full((1, QB), F32_LOWEST, F32)
        clo = n_causal
        hi = jnp.full((1, QB), F32_HIGHEST, F32)
        chi = jnp.zeros((1, QB), F32)
        done_f = jnp.where(few | (clo == kf), 1.0, 0.0)
        zero = jnp.zeros((1, QB), F32)
        one = jnp.ones((1, QB), F32)

        def all_done(done_f):
            return jnp.min(done_f) > 0.5

        def peel_ready(st):
            it, hi, chi, done_f = st[0], st[2], st[4], st[5]
            few_left = (kf - chi <= PEEL_NEED) & (hi < F32_HIGHEST)
            ok = jnp.min(jnp.where(few_left, 1.0, done_f)) > 0.5
            return jnp.logical_and(it >= PEEL_AFTER, ok)

        def cond(st):
            return jnp.logical_and(st[0] < SEARCH_MAX_ITERS, jnp.logical_not(peel_ready(st)))

        def cond_no_peel(st):
            return jnp.logical_and(st[0] < SEARCH_MAX_ITERS, jnp.logical_not(all_done(st[5])))

        def step(st):
            it, lo, hi, clo, chi, done_f, n_up, n_dn, w_lo, w_hi, last = st
            done = done_f > 0.5
            mid = 0.5 * lo + 0.5 * hi
            f_lo = (jnp.log(clo) - log_k) * w_lo
            f_hi = (log_k - jnp.log(jnp.maximum(chi, 0.5))) * w_hi
            interp = lo + (hi - lo) * (f_lo / jnp.maximum(f_lo + f_hi, 1e-9))
            open_hi = hi >= F32_HIGHEST
            open_lo = lo <= F32_LOWEST
            cand = jnp.where(it % SEARCH_BISECT_EVERY == SEARCH_BISECT_EVERY - 1, mid, interp)
            cand = jnp.where(open_hi, lo + std * (SEARCH_FIRST_STEP * jnp.exp2(n_up)), cand)
            cand = jnp.where(open_lo, hi - std * (SEARCH_FIRST_STEP * jnp.exp2(n_dn)), cand)
            cand = jnp.where(open_lo & open_hi, guess, cand)
            bad = jnp.logical_not((cand > lo) & (cand < hi))
            cand = jnp.where(bad, mid, cand)
            cand = jnp.where(done, lo, cand)
            cnt = count_ge(cand)
            up = jnp.logical_not(done) & (cnt >= kf)
            dn = jnp.logical_not(done) & (cnt < kf)
            n_up = n_up + jnp.where(up & open_hi, 1.0, 0.0)
            n_dn = n_dn + jnp.where(dn & open_lo, 1.0, 0.0)
            w_hi = jnp.where(up, jnp.where(last == 1.0, 0.5 * w_hi, 1.0), jnp.where(dn, 1.0, w_hi))
            w_lo = jnp.where(dn, jnp.where(last == 2.0, 0.5 * w_lo, 1.0), jnp.where(up, 1.0, w_lo))
            last = jnp.where(up, 1.0, jnp.where(dn, 2.0, last))
            lo = jnp.where(up, cand, lo)
            clo = jnp.where(up, cnt, clo)
            hi = jnp.where(dn, cand, hi)
            chi = jnp.where(dn, cnt, chi)
            mid2 = 0.5 * lo + 0.5 * hi
            stuck = jnp.logical_not((mid2 > lo) & (mid2 < hi))
            done_f = jnp.where(done | (clo == kf) | stuck, 1.0, 0.0)
            return it + 1, lo, hi, clo, chi, done_f, n_up, n_dn, w_lo, w_hi, last

        def peel_cond(ps):
            return jnp.logical_and(ps[0] < PEEL_NEED + 2, jnp.logical_not(all_done(ps[5])))

        def peel_step(ps):
            n, lo, hi, clo, chi, done_f = ps
            done = done_f > 0.5
            y = max_below(hi)
            cand = jnp.where(done, lo, y)
            cnt = count_ge(cand)
            up = jnp.logical_not(done) & (cnt >= kf)
            dn = jnp.logical_not(done) & (cnt < kf)
            lo = jnp.where(up, cand, lo)
            clo = jnp.where(up, cnt, clo)
            hi = jnp.where(dn, cand, hi)
            chi = jnp.where(dn, cnt, chi)
            return n + 1, lo, hi, clo, chi, jnp.where(done | up, 1.0, 0.0)

        st = (jnp.int32(0), lo, hi, clo, chi, done_f, zero, zero, one, one, zero)
        st = lax.fori_loop(0, SEARCH_UNTESTED_TRIPS, lambda _, s: step(s), st)
        st = lax.while_loop(cond, step, st)
        ps = lax.while_loop(peel_cond, peel_step, (jnp.int32(0),) + tuple(st[1:6]))
        st = lax.while_loop(cond_no_peel, step, (st[0],) + tuple(ps[1:6]) + tuple(st[6:]))
        lo, hi, clo, chi = st[1], st[2], st[3], st[4]
        lo = jnp.where(few, F32_LOWEST, lo)

        need = kf - chi
        tied = jnp.logical_and(clo > kf, jnp.logical_not(few))

        @pl.when(jnp.max(tied.astype(F32)) > 0.5)
        def _break_ties():
            def count_eq_le(cut):
                def body(c, part):
                    row0 = pl.multiple_of(c * KC, KC)
                    tile = sc_ref[pl.ds(row0, KC), :]
                    kpos = (row0 + lax.broadcasted_iota(jnp.int32, (KC, QB), 0)).astype(F32)
                    ind = jnp.where((tile == lo) & (kpos <= cut), 1.0, 0.0)
                    return part + jnp.sum(ind.reshape(KC // 8, 8, QB), axis=0)
                part = lax.fori_loop(0, i + 1, body, jnp.zeros((8, QB), F32))
                return jnp.sum(part, axis=0, keepdims=True)

            def bis(_, st):
                a, b = st
                mid = jnp.floor(0.5 * (a + b))
                ok = count_eq_le(mid) <= need
                return jnp.where(ok, mid, a), jnp.where(ok, b, mid)

            a0 = jnp.full((1, QB), -1.0, F32)
            b0 = ((i + 1) * KC).astype(F32) + jnp.zeros((1, QB), F32)
            n_bits = int(np.ceil(np.log2(sc_ref.shape[0] + 2))) + 1
            cut, _ = lax.fori_loop(0, n_bits, bis, (a0, b0))

            def drop(c, _):
                row0 = pl.multiple_of(c * KC, KC)
                tile = sc_ref[pl.ds(row0, KC), :]
                kpos = (row0 + lax.broadcasted_iota(jnp.int32, (KC, QB), 0)).astype(F32)
                kill = tied & (tile == lo) & (kpos > cut)
                sc_ref[pl.ds(row0, KC), :] = jnp.where(kill, -jnp.inf, tile)
                return 0
            lax.fori_loop(0, i + 1, drop, 0)

        thr_ref[...] = jnp.broadcast_to(lo, thr_ref.shape)
        m_ref[...] = jnp.full(m_ref.shape, NEG_BIG, F32)
        l_ref[...] = jnp.zeros(l_ref.shape, F32)
        acc_ref[...] = jnp.zeros(acc_ref.shape, F32)

    row0 = pl.multiple_of(j * KC, KC)
    SL = ATTN_SLAB
    n_slab = KC // SL
    thr = jnp.broadcast_to(thr_ref[0:1, :], (SL, QB))
    for r in range(n_slab):
        rows = slice(r * SL, (r + 1) * SL)
        sc = sc_ref[pl.ds(row0 + r * SL, SL), :]
        mask_buf[rows, :] = jnp.where(sc >= thr, 0.0, NEG_BIG)
    posk = pos_ref[...]
    nxt_keys = chunk_keys(j)
    nxt_acc = jnp.zeros((KC, QB), F32)
    for g0 in range(0, N_SPARSE_HEADS, ATTN_HEAD_GROUP):
        heads = range(g0, g0 + ATTN_HEAD_GROUP)
        col_max = {}
        for h in heads:
            hs = slice(h * HEAD_DIM, (h + 1) * HEAD_DIM)
            lhs = jnp.concatenate([k_ref[:, hs], posk], axis=1)
            rhs = jnp.concatenate([qT_ref[hs, :], srow_ref[h]], axis=0)
            s = jnp.dot(lhs, rhs, preferred_element_type=F32)
            cm = [jnp.full((SL, QB), NEG_BIG, F32)] * 2
            for r in range(n_slab):
                rows = slice(r * SL, (r + 1) * SL)
                blk = s[rows, :] + mask_buf[rows, :]
                s_buf[h, rows, :] = blk
                cm[r % 2] = jnp.maximum(cm[r % 2], blk)
            col_max[h] = jnp.max(jnp.maximum(cm[0], cm[1]), axis=0, keepdims=True)
        alphas = {}
        for h in heads:
            nxt_acc = score_heads(nxt_acc, nxt_keys, qiT_next_ref, wT_next_ref,
                                  range(h * IDX_PER_ATTN_HEAD, h * IDX_PER_ATTN_HEAD + IDX_IN_SOFTMAX))
            m_old = m_ref[h]
            m_new = jnp.maximum(m_old, col_max[h])
            alphas[h] = jnp.exp2(m_old - m_new)
            m_ref[h] = m_new
            mb = jnp.concatenate([m_new] * (SL // 8), axis=0)
            for r in range(n_slab):
                rows = slice(r * SL, (r + 1) * SL)
                p_buf[h, rows, :] = jnp.exp2(s_buf[h, rows, :] - mb).astype(BF16)
        ones_rows = jnp.ones((ROW_ALIGN, KC), BF16)
        for h in heads:
            nxt_acc = score_heads(nxt_acc, nxt_keys, qiT_next_ref, wT_next_ref,
                                  range(h * IDX_PER_ATTN_HEAD + IDX_IN_SOFTMAX,
                                        (h + 1) * IDX_PER_ATTN_HEAD))
            hs = slice(h * HEAD_DIM, (h + 1) * HEAD_DIM)
            v_ext = jnp.concatenate([vT_ref[hs, :], ones_rows], axis=0)
            pv = jnp.dot(v_ext, p_buf[h], preferred_element_type=F32)
            acc_ref[hs, :] = alphas[h][0:1, :] * acc_ref[hs, :] + pv[:HEAD_DIM, :]
            l_ref[h] = alphas[h] * l_ref[h] + pv[HEAD_DIM:HEAD_DIM + 1, :]

    n1, n2 = finish_scores(nxt_acc, sc2_ref.at[1 - cur], j, False)
    mom_ref[1 - cur, 0] += n1
    mom_ref[1 - cur, 1] += n2

    @pl.when(j == i)
    def _finish():
        for h in range(N_SPARSE_HEADS):
            hs = slice(h * HEAD_DIM, (h + 1) * HEAD_DIM)
            o = acc_ref[hs, :] / l_ref[h, 0:1, :]
            o_ref[:, hs] = o.T.astype(o_ref.dtype)


N_SLOPE_PIECES = 3
POS_SPLIT = 128


def _alibi_mxu_operands(S, slopes_log2):
    assert S <= POS_SPLIT * 256
    kpos = jnp.arange(S, dtype=jnp.int32)[:, None]
    col = jnp.arange(128, dtype=jnp.int32)[None, :]
    part = jnp.where(col % 2 == 0, kpos // POS_SPLIT, kpos % POS_SPLIT)
    pos = jnp.where(col < 2 * N_SLOPE_PIECES, part, 0).astype(BF16)
    rows = np.zeros((len(slopes_log2), 128), np.float32)
    for h, s in enumerate(slopes_log2):
        rest = np.float64(s)
        for p in range(N_SLOPE_PIECES):
            piece = np.float64(np.asarray(rest, np.float32).astype(jnp.bfloat16).astype(np.float32))
            rows[h, 2 * p] = POS_SPLIT * piece
            rows[h, 2 * p + 1] = piece
            rest = rest - piece
    srow = jnp.broadcast_to(jnp.asarray(rows, F32)[:, :, None],
                            (len(slopes_log2), 128, DSA_QB)).astype(BF16)
    return pos, srow


def _dsa_attention(qT, k, vT, qiT, ki, wT, slopes, *, topk):
    S = k.shape[0]
    QB, KC = DSA_QB, DSA_KC
    assert S % (2 * QB) == 0 and QB == KC
    nq = S // QB
    pos, srow = _alibi_mxu_operands(S, [s * LOG2_E for s in slopes])
    quantile = statistics.NormalDist().inv_cdf
    zq = np.array([quantile(1.0 - (topk - 0.5) / n) if n > topk else 0.0
                   for n in range(1, S + 1)], np.float32)
    zq = np.broadcast_to(zq, (8, S))
    qi_tab = np.concatenate([np.full(i + 1, i, np.int32) for i in range(nq)])
    kj_tab = np.concatenate([np.arange(i + 1, dtype=np.int32) for i in range(nq)])
    n_steps = int(qi_tab.shape[0])
    W = N_SPARSE_HEADS * HEAD_DIM
    WI = N_IDX_HEADS * IDX_HEAD_DIM
    grid_spec = pltpu.PrefetchScalarGridSpec(
        num_scalar_prefetch=2,
        grid=(n_steps,),
        in_specs=[
            pl.BlockSpec((W, QB), lambda t, qi, kj: (0, qi[t])),
            pl.BlockSpec((KC, W), lambda t, qi, kj: (kj[t], 0)),
            pl.BlockSpec((W, KC), lambda t, qi, kj: (0, kj[t])),
            pl.BlockSpec((WI, QB), lambda t, qi, kj: (0, qi[t])),
            pl.BlockSpec((WI, QB), lambda t, qi, kj: (0, jnp.minimum(qi[t] + 1, nq - 1))),
            pl.BlockSpec((S, IDX_BLOCK), lambda t, qi, kj: (0, 0)),
            pl.BlockSpec((IDX_BLOCK, QB), lambda t, qi, kj: (0, qi[t])),
            pl.BlockSpec((IDX_BLOCK, QB), lambda t, qi, kj: (0, jnp.minimum(qi[t] + 1, nq - 1))),
            pl.BlockSpec((KC, LANES), lambda t, qi, kj: (kj[t], 0)),
            pl.BlockSpec(srow.shape, lambda t, qi, kj: (0, 0, 0)),
            pl.BlockSpec((8, QB), lambda t, qi, kj: (0, qi[t])),
        ],
        out_specs=pl.BlockSpec((QB, W), lambda t, qi, kj: (qi[t], 0)),
        scratch_shapes=[
            pltpu.VMEM((2, S, QB), F32),
            pltpu.VMEM((8, QB), F32),
            pltpu.VMEM((N_SPARSE_HEADS, 8, QB), F32),
            pltpu.VMEM((N_SPARSE_HEADS, 8, QB), F32),
            pltpu.VMEM((W, QB), F32),
            pltpu.VMEM((N_SPARSE_HEADS, KC, QB), F32),
            pltpu.VMEM((N_SPARSE_HEADS, KC, QB), BF16),
            pltpu.VMEM((KC, QB), F32),
            pltpu.VMEM((2, 2, 8, QB), F32),
        ],
    )
    return pl.pallas_call(
        functools.partial(_dsa_kernel, topk=topk),
        grid_spec=grid_spec,
        out_shape=jax.ShapeDtypeStruct((S, W), BF16),
        compiler_params=_cparams("arbitrary"),
        name="dsa_attention",
    )(jnp.asarray(qi_tab), jnp.asarray(kj_tab), qT, k, vT, qiT, qiT, ki, wT, wT, pos, srow,
      jnp.asarray(zq))


DIL_BLOCKS_PER_STEP = 16
def _dil_kernel(bias_ref, q_ref, kc_ref, kp_ref, vc_ref, vp_ref, o_ref, lse_ref, *, n_blocks):
    first = jnp.where(pl.program_id(1) == 0, 1, 0)
    Q = BLOCK
    for b in range(n_blocks):
        rows = slice(b * Q, (b + 1) * Q)
        prev_rows = slice((b - 1) * Q, b * Q)
        for h in range(HEADS_PER_DIL_GROUP):
            hs = slice(h * HEAD_DIM, (h + 1) * HEAD_DIM)
            q = q_ref[rows, hs]
            k_prev = kp_ref[:, hs] if b == 0 else kc_ref[prev_rows, hs]
            v_prev = vp_ref[:, hs] if b == 0 else vc_ref[prev_rows, hs]
            kk = jnp.concatenate([k_prev, kc_ref[rows, hs]], axis=0)
            vv = jnp.concatenate([v_prev, vc_ref[rows, hs]], axis=0)
            bias = bias_ref[first, h] if b == 0 else bias_ref[0, h]
            s = lax.dot_general(q, kk, (((1,), (1,)), ((), ())),
                                preferred_element_type=F32) + bias
            mx = jnp.max(s, axis=-1, keepdims=True)
            p = jnp.exp2(s - mx)
            l = jnp.sum(p, axis=-1, keepdims=True)
            o = jnp.dot(p.astype(BF16), vv, preferred_element_type=F32) / l
            o_ref[rows, hs] = o
            lse_ref[rows, hs] = jnp.broadcast_to(mx + jnp.log2(l), (Q, HEAD_DIM))


def _dilated_bias(span, dilation, slopes_log2):
    Q = BLOCK
    step = Q + np.arange(Q)[:, None] - np.arange(2 * Q)[None, :]
    band = (step >= 0) & (step <= span)
    has_prev = np.stack([np.ones_like(band), np.broadcast_to(np.arange(2 * Q)[None, :] >= Q, band.shape)])
    dist = (step * dilation).astype(np.float64)
    bias = -np.asarray(slopes_log2, np.float64)[None, :, None, None] * dist[None, None]
    keep = (band[None] & has_prev)[:, None]
    return jnp.asarray(np.where(keep, bias, NEG_BIG), F32)


def _dilated_group(qkv, window, dilation, slopes_g):
    _, d, n_sub, GW = qkv.shape
    assert d == dilation
    span = window // d
    nbs = min(DIL_BLOCKS_PER_STEP, n_sub // BLOCK)
    rows = nbs * BLOCK
    assert span <= BLOCK and n_sub % rows == 0
    bias = _dilated_bias(span, d, [s * LOG2_E for s in slopes_g])

    def cur(which):
        return pl.BlockSpec((None, None, rows, GW), lambda r, m: (which, r, m, 0))

    def prev(which):
        return pl.BlockSpec((None, None, BLOCK, GW),
                            lambda r, m: (which, r, jnp.maximum(m * nbs - 1, 0), 0))

    out_spec = pl.BlockSpec((None, rows, GW), lambda r, m: (r, m, 0))
    return pl.pallas_call(
        functools.partial(_dil_kernel, n_blocks=nbs),
        grid=(d, n_sub // rows),
        in_specs=[pl.BlockSpec(bias.shape, lambda r, m: (0, 0, 0, 0)),
                  cur(0), cur(1), prev(1), cur(2), prev(2)],
        out_specs=[out_spec, out_spec],
        out_shape=[jax.ShapeDtypeStruct((d, n_sub, GW), F32)] * 2,
        compiler_params=_cparams("parallel", "arbitrary"),
        name=f"dilated_attention_d{d}",
    )(bias, qkv, qkv, qkv, qkv, qkv)


def _layer_norm(y, g, b):
    mu = jnp.mean(y, axis=-1, keepdims=True)
    yc = y - mu
    var = jnp.mean(yc * yc, axis=-1, keepdims=True)
    return yc * lax.rsqrt(var + LN_EPS) * g + b


def _merge_kernel(oa_ref, o1_ref, o2_ref, o3_ref, l1_ref, l2_ref, l3_ref,
                  ga_ref, gb_ref, x_ref, wa_ref, wb_ref, wo_ref, g_ref, b_ref,
                  h_ref, hb_ref, nat_ref):
    tm = x_ref.shape[0]
    n_slab = DIL_OUT_WIDTH // LANES

    def natural(ref, slot):
        d = ref.shape[0]
        if d == 1:
            return [ref[0, :, c * LANES:(c + 1) * LANES] for c in range(n_slab)]
        for r in range(d):
            for c in range(n_slab):
                nat_ref[slot, c, pl.ds(r, tm // d, stride=d), :] = ref[r, :, c * LANES:(c + 1) * LANES]
        return [nat_ref[slot, c] for c in range(n_slab)]

    o1, l1 = natural(o1_ref, 0), natural(l1_ref, 1)
    o2, l2 = natural(o2_ref, 2), natural(l2_ref, 3)
    o3, l3 = natural(o3_ref, 4), natural(l3_ref, 5)
    slabs = []
    for c in range(n_slab):
        lm = jnp.maximum(jnp.maximum(l1[c], l2[c]), l3[c])
        e1, e2, e3 = jnp.exp2(l1[c] - lm), jnp.exp2(l2[c] - lm), jnp.exp2(l3[c] - lm)
        den = e1 + e2 + e3
        slabs.append((e1 / den) * o1[c] + (e2 / den) * o2[c] + (e3 / den) * o3[c])
    ob = jnp.concatenate(slabs, axis=1)
    ya = jnp.dot(oa_ref[...], wa_ref[...], preferred_element_type=F32)
    yb = jnp.dot(ob.astype(BF16), wb_ref[...], preferred_element_type=F32)
    merged = ga_ref[...] * ya + gb_ref[...] * yb
    mix = jnp.dot(merged.astype(BF16), wo_ref[...], preferred_element_type=F32)
    h = _layer_norm(DEEPNORM_ALPHA * x_ref[...] + mix, g_ref[...], b_ref[...])
    h_ref[...] = h
    hb_ref[...] = h.astype(BF16)


MERGE_TM = 256


def _merge(oa, dil, gates, x2, wa, wb, wo, ln_g, ln_b):
    S = x2.shape[0]
    tm = MERGE_TM
    (o1, l1), (o2, l2), (o3, l3) = dil
    GW = DIL_OUT_WIDTH

    def rows(w, col=0):
        return pl.BlockSpec((tm, w), lambda i: (i, col))

    def residue_major(a):
        d = a.shape[0]
        assert tm % (8 * d) == 0
        return pl.BlockSpec((d, tm // d, GW), lambda i: (0, i, 0))

    def whole(a):
        return pl.BlockSpec(a.shape, lambda i: (0,) * a.ndim, pipeline_mode=pl.Buffered(1))

    return pl.pallas_call(
        _merge_kernel,
        grid=(S // tm,),
        in_specs=[rows(SPARSE_WIDTH),
                  residue_major(o1), residue_major(o2), residue_major(o3),
                  residue_major(l1), residue_major(l2), residue_major(l3),
                  rows(D_MODEL, 0), rows(D_MODEL, 1), rows(D_MODEL),
                  whole(wa), whole(wb), whole(wo), whole(ln_g), whole(ln_b)],
        out_specs=[rows(D_MODEL), rows(D_MODEL)],
        out_shape=[jax.ShapeDtypeStruct((S, D_MODEL), F32),
                   jax.ShapeDtypeStruct((S, D_MODEL), BF16)],
        scratch_shapes=[pltpu.VMEM((6, GW // LANES, tm, LANES), F32)],
        compiler_params=_cparams("parallel"),
        name="merge_outproj_ln",
    )(oa, o1, o2, o3, l1, l2, l3, gates, gates, x2, wa, wb, wo, ln_g, ln_b)


FFN_TM = 1024
FFN_UP_TN = 512
FFN_DOWN_TN = 256


def _ffn_up_kernel(hb_ref, wg_ref, wu_ref, a_ref):
    hb = hb_ref[...]
    gate = jnp.dot(hb, wg_ref[...].astype(BF16), preferred_element_type=F32)
    up = jnp.dot(hb, wu_ref[...].astype(BF16), preferred_element_type=F32)
    a_ref[...] = (gate * jax.nn.sigmoid(gate) * up).astype(a_ref.dtype)


def _ffn_down_kernel(a_ref, h_hbm, wd_ref, g_ref, b_ref, o_ref, h_buf, h_sem):
    i = pl.program_id(0)
    j = pl.program_id(1)
    tm = h_buf.shape[0]
    h_copy = pltpu.make_async_copy(h_hbm.at[pl.ds(pl.multiple_of(i * tm, tm), tm), :],
                                   h_buf, h_sem)

    @pl.when(j == 0)
    def _():
        h_copy.start()

    tn = wd_ref.shape[1]
    cols = pl.ds(pl.multiple_of(j * tn, tn), tn)
    o_ref[:, cols] = jnp.dot(a_ref[...], wd_ref[...].astype(BF16), preferred_element_type=F32)

    @pl.when(j == pl.num_programs(1) - 1)
    def _():
        h_copy.wait()
        o_ref[...] = _layer_norm(DEEPNORM_ALPHA * h_buf[...] + o_ref[...],
                                 g_ref[...], b_ref[...])


def _ffn(hb, h, wg, wu, wd, ln_g, ln_b):
    S = h.shape[0]
    F = wg.shape[1]
    tm = min(FFN_TM, S)
    assert S % tm == 0 and F % FFN_UP_TN == 0 and D_MODEL % FFN_DOWN_TN == 0
    act = pl.pallas_call(
        _ffn_up_kernel,
        grid=(S // tm, F // FFN_UP_TN),
        in_specs=[pl.BlockSpec((tm, D_MODEL), lambda i, j: (i, 0)),
                  pl.BlockSpec((D_MODEL, FFN_UP_TN), lambda i, j: (0, j)),
                  pl.BlockSpec((D_MODEL, FFN_UP_TN), lambda i, j: (0, j))],
        out_specs=pl.BlockSpec((tm, FFN_UP_TN), lambda i, j: (i, j)),
        out_shape=jax.ShapeDtypeStruct((S, F), BF16),
        compiler_params=_cparams("parallel", "arbitrary"),
        name="ffn_gate_up",
    )(hb, wg, wu)

    def per_row_tile(width):
        return pl.BlockSpec((tm, width), lambda i, j: (i, 0), pipeline_mode=pl.Buffered(1))

    return pl.pallas_call(
        _ffn_down_kernel,
        grid=(S // tm, D_MODEL // FFN_DOWN_TN),
        in_specs=[per_row_tile(F), pl.BlockSpec(memory_space=pl.ANY),
                  pl.BlockSpec((F, FFN_DOWN_TN), lambda i, j: (0, j)),
                  pl.BlockSpec((1, D_MODEL), lambda i, j: (0, 0)),
                  pl.BlockSpec((1, D_MODEL), lambda i, j: (0, 0))],
        out_specs=pl.BlockSpec((tm, D_MODEL), lambda i, j: (i, 0)),
        out_shape=jax.ShapeDtypeStruct((S, D_MODEL), F32),
        scratch_shapes=[pltpu.VMEM((tm, D_MODEL), F32), pltpu.SemaphoreType.DMA(())],
        compiler_params=_cparams("arbitrary", "arbitrary"),
        name="ffn_down_ln",
    )(act, h, wd, ln_g, ln_b)


def _mixer_inputs(x2, w_in):
    O = IN_OFFS
    wT = jnp.transpose(w_in)
    k, xb = _project_and_cast(x2, wT, O[1], SPARSE_WIDTH, name="proj_ak")
    qT = _project(xb, wT, O[0], SPARSE_WIDTH, out_dtype=BF16,
                  scale=LOG2_E * HEAD_DIM ** -0.5, transpose_out=True, name="proj_aqT")
    vT = _project(xb, wT, O[2], SPARSE_WIDTH, out_dtype=BF16, transpose_out=True,
                  name="proj_avT")
    qiT, ki, iwT = _project_indexer(xb, wT, O[3], O[4], q_scale=IDX_HEAD_DIM ** -0.5,
                                    w_scale=N_IDX_HEADS ** -0.5)
    dil_qkv = [_project_dilated_qkv(xb, wT, O[6], g, d) for g, (_, d) in enumerate(DIL_GROUPS)]
    gates = _project(xb, wT, O[9], 2 * D_MODEL, out_dtype=F32, sigmoid=True, name="proj_gates")
    return qT, k, vT, qiT, ki, iwT, dil_qkv, gates


def _layer(x2, w_in, w_a, w_b, w_out, ln1_g, ln1_b, w_gate, w_up, w_down, ln2_g, ln2_b):
    S = x2.shape[0]
    slopes = _alibi_slopes()
    qT, k, vT, qiT, ki, iwT, dil_qkv, gates = _mixer_inputs(x2, w_in)

    topk = min(TOPK_MAX, S // 4)
    oa = _dsa_attention(qT, k, vT, qiT, ki, iwT, slopes[N_DIL_HEADS:], topk=topk)

    dil = []
    for g, (win, d) in enumerate(DIL_GROUPS):
        sl = slopes[g * HEADS_PER_DIL_GROUP:(g + 1) * HEADS_PER_DIL_GROUP]
        dil.append(_dilated_group(dil_qkv[g], win, d, sl))

    h, hb = _merge(oa, dil, gates, x2, w_a.astype(BF16), w_b.astype(BF16), w_out.astype(BF16),
                   ln1_g.reshape(1, -1), ln1_b.reshape(1, -1))
    return _ffn(hb, h, w_gate, w_up, w_down, ln2_g.reshape(1, -1), ln2_b.reshape(1, -1))


def kernel(x, w_in, w_a, w_b, w_out, ln1_g, ln1_b, w_gate, w_up, w_down, ln2_g, ln2_b):
    B, S, D = x.shape
    outs = []
    for b in range(B):
        y = x[b]
        for l in range(w_in.shape[0]):
            y = _layer(y, w_in[l], w_a[l], w_b[l], w_out[l], ln1_g[l], ln1_b[l],
                       w_gate[l], w_up[l], w_down[l], ln2_g[l], ln2_b[l])
        outs.append(y)
    return jnp.stack(outs)
```
